```python
import math
import jax, jax.numpy as jnp
from jax import lax
import numpy as np

D_MODEL = 1024
BATCH = 8
SEQ = 4096
DEPTH = 1

GRID_W = 64
CTX_LEN = 256
S5_WIDTH = D_MODEL // 2
S5_GROUP = 16
S5_GROUPS = S5_WIDTH // S5_GROUP
S5_STATE = 64
CONV_WIDTH = D_MODEL - S5_WIDTH
CONV_K = 31
MIX_WIDTH = S5_WIDTH + CONV_WIDTH
IN_COLS = S5_WIDTH + 2 * CONV_WIDTH
D_FF = 4 * D_MODEL
EPS_RMS = 1e-6
EPS_LN = 1e-5
DT_MIN = 1e-3
DT_MAX = 1e-1

kernel_name = "hybrid_s5_conformer_prefix_dit_block"


def _rms_norm(x, g):
    xf = x.astype(jnp.float32)
    y = xf * lax.rsqrt(jnp.mean(xf * xf, axis=-1, keepdims=True) + EPS_RMS)
    return (y * g.astype(jnp.float32)).astype(x.dtype)


def _layer_norm(x, g, b):
    xf = x.astype(jnp.float32)
    mu = jnp.mean(xf, axis=-1, keepdims=True)
    var = jnp.mean(jnp.square(xf - mu), axis=-1, keepdims=True)
    return (xf - mu) * lax.rsqrt(var + EPS_LN) * g.astype(jnp.float32) + b.astype(jnp.float32)


def _modulate(x, shift, scale):
    return x * (1 + scale) + shift


def _s5_discretise(lam_re, lam_im, log_dt, b_re, b_im):
    lam_re = lam_re.astype(jnp.float32)
    lam_im = lam_im.astype(jnp.float32)
    dt = jnp.exp(log_dt.astype(jnp.float32))[..., None]
    mag = jnp.exp(lam_re * dt)
    abar_re = mag * jnp.cos(lam_im * dt)
    abar_im = mag * jnp.sin(lam_im * dt)
    den = lam_re * lam_re + lam_im * lam_im
    num_re = abar_re - 1
    f_re = (num_re * lam_re + abar_im * lam_im) / den
    f_im = (abar_im * lam_re - num_re * lam_im) / den
    b_re = b_re.astype(jnp.float32)
    b_im = b_im.astype(jnp.float32)
    bbar_re = f_re[..., None] * b_re - f_im[..., None] * b_im
    bbar_im = f_re[..., None] * b_im + f_im[..., None] * b_re
    return abar_re, abar_im, bbar_re, bbar_im


def _ccombine(e1, e2):
    a1r, a1i, b1r, b1i = e1
    a2r, a2i, b2r, b2i = e2
    return (a2r * a1r - a2i * a1i,
            a2r * a1i + a2i * a1r,
            a2r * b1r - a2i * b1i + b2r,
            a2r * b1i + a2i * b1r + b2i)


def _cscan(abar_re, abar_im, bu_re, bu_im, reverse):
    shp = (1, bu_re.shape[1]) + abar_re.shape
    a_re = jnp.broadcast_to(abar_re, shp)
    a_im = jnp.broadcast_to(abar_im, shp)
    return lax.associative_scan(_ccombine, (a_re, a_im, bu_re, bu_im), reverse=reverse, axis=1)


def _s5_readout(s_re, s_im, c_re, c_im):
    return (jnp.einsum('blgp,ghp->blgh', s_re, c_re.astype(jnp.float32))
            - jnp.einsum('blgp,ghp->blgh', s_im, c_im.astype(jnp.float32)))


def _s5_glu(y, w_glu):
    g = jax.nn.gelu(y)
    return g * jax.nn.sigmoid(g @ w_glu.astype(jnp.float32))


def _s5_mixer(u_lat, u_ctx, lam_re, lam_im, log_dt, b_re, b_im, c_re, c_im, d_skip, w_glu, need_ctx):
    abar_re, abar_im, bbar_re, bbar_im = _s5_discretise(lam_re, lam_im, log_dt, b_re, b_im)
    bsz, n_lat = u_lat.shape[0], u_lat.shape[1]
    n_ctx = u_ctx.shape[1]
    u4 = u_lat.astype(jnp.float32).reshape(bsz, n_lat, S5_GROUPS, S5_GROUP)
    uc4 = u_ctx.astype(jnp.float32).reshape(bsz, n_ctx, S5_GROUPS, S5_GROUP)
    d_f = d_skip.astype(jnp.float32)
    y = d_f * u4
    yc = d_f * uc4 if need_ctx else None
    for dirn, reverse in ((0, False), (1, True)):
        ar, ai = abar_re[dirn], abar_im[dirn]
        br, bi = bbar_re[dirn], bbar_im[dirn]
        cbu_re = jnp.einsum('bngh,gph->bngp', uc4, br)
        cbu_im = jnp.einsum('bngh,gph->bngp', uc4, bi)
        _, _, cs_re, cs_im = _cscan(ar, ai, cbu_re, cbu_im, reverse)
        end = 0 if reverse else -1
        h0_re = cs_re[:, end][:, None]
        h0_im = cs_im[:, end][:, None]
        bu_re = jnp.einsum('blgh,gph->blgp', u4, br)
        bu_im = jnp.einsum('blgh,gph->blgp', u4, bi)
        ap_re, ap_im, s_re, s_im = _cscan(ar, ai, bu_re, bu_im, reverse)
        s_re = s_re + ap_re * h0_re - ap_im * h0_im
        s_im = s_im + ap_re * h0_im + ap_im * h0_re
        y = y + _s5_readout(s_re, s_im, c_re[dirn], c_im[dirn])
        if need_ctx:
            yc = yc + _s5_readout(cs_re, cs_im, c_re[dirn], c_im[dirn])
    out = _s5_glu(y.reshape(bsz, n_lat, S5_WIDTH), w_glu).astype(u_lat.dtype)
    out_c = _s5_glu(yc.reshape(bsz, n_ctx, S5_WIDTH), w_glu).astype(u_ctx.dtype) if need_ctx else None
    return out, out_c


def _conv_module(v, gate, w_dw, b_dw, ln_g, ln_b, on_grid):
    bsz, n, ch = v.shape
    h = v * jax.nn.sigmoid(gate)
    pad = (CONV_K // 2, CONV_K // 2)
    if on_grid:
        rows = n // GRID_W
        h = lax.conv_general_dilated(
            h.reshape(bsz, rows, GRID_W, ch),
            w_dw.reshape(CONV_K, 1, 1, ch).astype(h.dtype),
            window_strides=(1, 1), padding=(pad, (0, 0)),
            dimension_numbers=('NHWC', 'HWIO', 'NHWC'), feature_group_count=ch)
        h = h.reshape(bsz, n, ch)
    else:
        h = lax.conv_general_dilated(
            h, w_dw.reshape(CONV_K, 1, ch).astype(h.dtype),
            window_strides=(1,), padding=(pad,),
            dimension_numbers=('NWC', 'WIO', 'NWC'), feature_group_count=ch)
    h = h + b_dw
    return jax.nn.silu(_layer_norm(h, ln_g, ln_b)).astype(v.dtype)


def _sq_relu_mlp(h, w1, w2):
    return jnp.square(jax.nn.relu(h @ w1)) @ w2


def setup_inputs(seed: int = 0) -> dict:
    key = jax.random.key(seed)
    ks = jax.random.split(key, 26)
    f32 = jnp.float32
    G, P, H = S5_GROUPS, S5_STATE, S5_GROUP

    def nrm(k, shape, scale):
        return jax.random.normal(k, shape, f32) * scale

    return {
        'x': nrm(ks[0], (BATCH, SEQ, D_MODEL), 1.0),
        'c': nrm(ks[1], (BATCH, D_MODEL), 1.0),
        'ctx': nrm(ks[2], (BATCH, CTX_LEN, D_MODEL), 1.0),
        'c_ctx': nrm(ks[3], (D_MODEL,), 1.0),
        'ada_w': nrm(ks[4], (DEPTH, D_MODEL, 6 * D_MODEL), 0.5 * D_MODEL ** -0.5),
        'ada_b': nrm(ks[5], (DEPTH, 6 * D_MODEL), 0.01),
        'norm1_g': 1.0 + nrm(ks[6], (DEPTH, D_MODEL), 0.02),
        'w_in': nrm(ks[7], (DEPTH, D_MODEL, IN_COLS), D_MODEL ** -0.5),
        's5_lam_re': -0.5 + nrm(ks[8], (DEPTH, 2, G, P), 0.01),
        's5_lam_im': jnp.pi * jnp.arange(P, dtype=f32) + nrm(ks[9], (DEPTH, 2, G, P), 0.01),
        's5_log_dt': jax.random.uniform(ks[10], (DEPTH, 2, G), f32, math.log(DT_MIN), math.log(DT_MAX)),
        's5_b_re': nrm(ks[11], (DEPTH, 2, G, P, H), (2 * H) ** -0.5),
        's5_b_im': nrm(ks[12], (DEPTH, 2, G, P, H), (2 * H) ** -0.5),
        's5_c_re': nrm(ks[13], (DEPTH, 2, G, H, P), P ** -0.5),
        's5_c_im': nrm(ks[14], (DEPTH, 2, G, H, P), P ** -0.5),
        's5_d': nrm(ks[15], (DEPTH, G, H), 1.0),
        's5_w_glu': nrm(ks[16], (DEPTH, S5_WIDTH, S5_WIDTH), S5_WIDTH ** -0.5),
        'conv_w': nrm(ks[17], (DEPTH, CONV_K, CONV_WIDTH), CONV_K ** -0.5),
        'conv_b': nrm(ks[18], (DEPTH, CONV_WIDTH), 0.01),
        'conv_ln_g': 1.0 + nrm(ks[19], (DEPTH, CONV_WIDTH), 0.02),
        'conv_ln_b': nrm(ks[20], (DEPTH, CONV_WIDTH), 0.01),
        'w_out': nrm(ks[21], (DEPTH, MIX_WIDTH, D_MODEL), MIX_WIDTH ** -0.5),
        'norm2_g': 1.0 + nrm(ks[22], (DEPTH, D_MODEL), 0.02),
        'mlp_w1': nrm(ks[23], (DEPTH, D_MODEL, D_FF), D_MODEL ** -0.5),
        'mlp_w2': nrm(ks[24], (DEPTH, D_FF, D_MODEL), D_FF ** -0.5),
        'final_g': 1.0 + nrm(ks[25], (D_MODEL,), 0.02),
    }


def reference(x, c, ctx, c_ctx, ada_w, ada_b, norm1_g, w_in, s5_lam_re, s5_lam_im, s5_log_dt,
              s5_b_re, s5_b_im, s5_c_re, s5_c_im, s5_d, s5_w_glu, conv_w, conv_b, conv_ln_g,
              conv_ln_b, w_out, norm2_g, mlp_w1, mlp_w2, final_g):
    h = x
    hc = ctx
    for layer in range(DEPTH):
        need_ctx = layer < DEPTH - 1
        mod = jax.nn.silu(c) @ ada_w[layer] + ada_b[layer]
        sh1, sc1, g1, sh2, sc2, g2 = jnp.split(mod[:, None, :], 6, axis=-1)
        n_cmod = 6 if need_ctx else 2
        modc = jax.nn.silu(c_ctx) @ ada_w[layer][:, :n_cmod * D_MODEL] + ada_b[layer][:n_cmod * D_MODEL]
        cmods = jnp.split(modc, n_cmod)

        a = _modulate(_rms_norm(h, norm1_g[layer]), sh1, sc1)
        ac = _modulate(_rms_norm(hc, norm1_g[layer]), cmods[1 - 1], cmods[1])
        z = a @ w_in[layer]
        zc = ac @ (w_in[layer] if need_ctx else w_in[layer][:, :S5_WIDTH])
        y_s5, yc_s5 = _s5_mixer(z[..., :S5_WIDTH], zc[..., :S5_WIDTH],
                                s5_lam_re[layer], s5_lam_im[layer], s5_log_dt[layer],
                                s5_b_re[layer], s5_b_im[layer], s5_c_re[layer], s5_c_im[layer],
                                s5_d[layer], s5_w_glu[layer], need_ctx)
        y_conv = _conv_module(z[..., S5_WIDTH:S5_WIDTH + CONV_WIDTH], z[..., S5_WIDTH + CONV_WIDTH:],
                              conv_w[layer], conv_b[layer], conv_ln_g[layer], conv_ln_b[layer], True)
        h = h + g1 * (jnp.concatenate([y_s5, y_conv], axis=-1) @ w_out[layer])

        h = h + g2 * _sq_relu_mlp(_modulate(_rms_norm(h, norm2_g[layer]), sh2, sc2),
                                  mlp_w1[layer], mlp_w2[layer])

        if need_ctx:
            csh1, csc1, cg1, csh2, csc2, cg2 = cmods
            yc_conv = _conv_module(zc[..., S5_WIDTH:S5_WIDTH + CONV_WIDTH], zc[..., S5_WIDTH + CONV_WIDTH:],
                                   conv_w[layer], conv_b[layer], conv_ln_g[layer], conv_ln_b[layer], False)
            hc = hc + cg1 * (jnp.concatenate([yc_s5, yc_conv], axis=-1) @ w_out[layer])
            hc = hc + cg2 * _sq_relu_mlp(_modulate(_rms_norm(hc, norm2_g[layer]), csh2, csc2),
                                         mlp_w1[layer], mlp_w2[layer])
    return _rms_norm(h, final_g)
```

```python
import functools

import jax
import jax.numpy as jnp
from jax import lax
from jax.experimental import pallas as pl
from jax.experimental.pallas import tpu as pltpu

F32 = jnp.float32
BF16 = jnp.bfloat16

D_MODEL = 1024
S5_WIDTH = 512
S5_GROUP = 16
S5_GROUPS = 32
S5_STATE = 64
CONV_WIDTH = 512
CONV_K = 31
GRID_W = 64
IN_COLS = S5_WIDTH + 2 * CONV_WIDTH
D_FF = 4 * D_MODEL
EPS_RMS = 1e-6
EPS_LN = 1e-5

CHUNK = 16
CHUNK_COLS = CHUNK * S5_GROUP
MOD_ROWS = 16
TOK_TILE = 512
CONV_ROWS = 32
VMEM_LIMIT = 56 * 1024 * 1024


def _dot(a, b):
    return jnp.dot(a, b, preferred_element_type=F32)


def _rms(x, g):
    return x * lax.rsqrt(jnp.mean(x * x, axis=-1, keepdims=True) + EPS_RMS) * g


def _mod_kernel(c_ref, w_ref, b_ref, o_ref):
    c = c_ref[...]
    s = c * jax.nn.sigmoid(c)
    w = w_ref[...]
    s_hi = s.astype(BF16)
    s_lo = (s - s_hi.astype(F32)).astype(BF16)
    w_hi = w.astype(BF16)
    w_lo = (w - w_hi.astype(F32)).astype(BF16)
    o_ref[...] = _dot(s_hi, w_hi) + _dot(s_lo, w_hi) + _dot(s_hi, w_lo) + b_ref[...]


def _mod_table(cc, ada_w, ada_b):
    n_out = ada_w.shape[1]
    tn = 1536
    return pl.pallas_call(
        _mod_kernel,
        grid=(n_out // tn,),
        in_specs=[
            pl.BlockSpec((MOD_ROWS, D_MODEL), lambda j: (0, 0)),
            pl.BlockSpec((D_MODEL, tn), lambda j: (0, j)),
            pl.BlockSpec((1, tn), lambda j: (0, j)),
        ],
        out_specs=pl.BlockSpec((MOD_ROWS, tn), lambda j: (0, j)),
        out_shape=jax.ShapeDtypeStruct((MOD_ROWS, n_out), F32),
        compiler_params=pltpu.CompilerParams(vmem_limit_bytes=VMEM_LIMIT),
        name="mod_table",
    )(cc, ada_w, ada_b.reshape(1, n_out))


def _inproj_kernel(x_ref, sh_ref, sc_ref, g_ref, w_ref, u_ref, hc_ref):
    x = x_ref[0]
    a = _rms(x, g_ref[...]) * (1.0 + sc_ref[0]) + sh_ref[0]
    z = _dot(a.astype(BF16), w_ref[...])
    u_ref[0] = z[:, :S5_WIDTH].astype(BF16)
    v = z[:, S5_WIDTH:S5_WIDTH + CONV_WIDTH]
    gate = z[:, S5_WIDTH + CONV_WIDTH:]
    hc_ref[0] = v * jax.nn.sigmoid(gate)


def _in_proj(x, sh, sc, g, w_bf, tile):
    bsz, n, _ = x.shape
    return pl.pallas_call(
        _inproj_kernel,
        grid=(bsz, n // tile),
        in_specs=[
            pl.BlockSpec((1, tile, D_MODEL), lambda b, i: (b, i, 0)),
            pl.BlockSpec((1, 1, D_MODEL), lambda b, i: (b, 0, 0)),
            pl.BlockSpec((1, 1, D_MODEL), lambda b, i: (b, 0, 0)),
            pl.BlockSpec((1, D_MODEL), lambda b, i: (0, 0)),
            pl.BlockSpec((D_MODEL, IN_COLS), lambda b, i: (0, 0)),
        ],
        out_specs=[
            pl.BlockSpec((1, tile, S5_WIDTH), lambda b, i: (b, i, 0)),
            pl.BlockSpec((1, tile, CONV_WIDTH), lambda b, i: (b, i, 0)),
        ],
        out_shape=[
            jax.ShapeDtypeStruct((bsz, n, S5_WIDTH), BF16),
            jax.ShapeDtypeStruct((bsz, n, CONV_WIDTH), F32),
        ],
        compiler_params=pltpu.CompilerParams(
            dimension_semantics=("parallel", "parallel"), vmem_limit_bytes=VMEM_LIMIT),
        name="in_proj",
    )(x, sh, sc, g, w_bf)


def _s5_kernel(n_lat, n_ctx, xl_ref, xc_ref, t_ref, m_ref, r_ref, a_ref, y_ref, e_ref, p_ref, ec_ref):
    half = 2 * S5_STATE
    m = m_ref[0]
    ec_ref[...] = _dot(xc_ref[0], m)
    e_ref[...] = _dot(xl_ref[0], m)
    a_re = jnp.broadcast_to(a_ref[0, 0:1, :], (8, half))
    a_im = jnp.broadcast_to(a_ref[0, 1:2, :], (8, half))
    is_fwd = lax.broadcasted_iota(jnp.int32, (8, half), 1) < S5_STATE

    def advance(src_ref, i, j, s_re, s_im):
        ef = src_ref[pl.ds(pl.multiple_of(i * 8, 8), 8), :]
        eb = src_ref[pl.ds(pl.multiple_of(j * 8, 8), 8), :]
        e_re = jnp.where(is_fwd, ef[:, :half], eb[:, :half])
        e_im = jnp.where(is_fwd, ef[:, half:], eb[:, half:])
        return (a_re * s_re - a_im * s_im + e_re, a_re * s_im + a_im * s_re + e_im)

    def ctx_body(i, s):
        return advance(ec_ref, i, n_ctx - 1 - i, *s)

    zero = jnp.zeros((8, half), F32)
    s = lax.fori_loop(0, n_ctx, ctx_body, (zero, zero))

    def lat_body(i, s):
        s_re, s_im = s
        j = n_lat - 1 - i
        ri = pl.ds(pl.multiple_of(i * 8, 8), 8)
        rj = pl.ds(pl.multiple_of(j * 8, 8), 8)
        p_ref[ri, 0:S5_STATE] = s_re[:, :S5_STATE]
        p_ref[rj, S5_STATE:half] = s_re[:, S5_STATE:]
        p_ref[ri, half:half + S5_STATE] = s_im[:, :S5_STATE]
        p_ref[rj, half + S5_STATE:] = s_im[:, S5_STATE:]
        return advance(e_ref, i, j, s_re, s_im)

    lax.fori_loop(0, n_lat, lat_body, s)
    y = _dot(xl_ref[0], t_ref[0]) + _dot(p_ref[...].astype(BF16), r_ref[0])
    y_ref[0] = y.astype(BF16)


def _s5_scan(x_lat, x_ctx, t_mat, m_mat, r_mat, a_pow, bsz):
    rows = x_lat.shape[1]
    rows_c = x_ctx.shape[1]
    mat_spec = pl.BlockSpec((1, CHUNK_COLS, CHUNK_COLS), lambda g: (g, 0, 0))
    return pl.pallas_call(
        functools.partial(_s5_kernel, rows // bsz, rows_c // bsz),
        grid=(S5_GROUPS,),
        in_specs=[
            pl.BlockSpec((1, rows, CHUNK_COLS), lambda g: (g, 0, 0)),
            pl.BlockSpec((1, rows_c, CHUNK_COLS), lambda g: (g, 0, 0)),
            mat_spec, mat_spec, mat_spec,
            pl.BlockSpec((1, 2, 2 * S5_STATE), lambda g: (g, 0, 0)),
        ],
        out_specs=pl.BlockSpec((1, rows, CHUNK_COLS), lambda g: (g, 0, 0)),
        out_shape=jax.ShapeDtypeStruct((S5_GROUPS, rows, CHUNK_COLS), BF16),
        scratch_shapes=[
            pltpu.VMEM((rows, CHUNK_COLS), F32),
            pltpu.VMEM((rows, CHUNK_COLS), F32),
            pltpu.VMEM((rows_c, CHUNK_COLS), F32),
        ],
        compiler_params=pltpu.CompilerParams(
            dimension_semantics=("parallel",), vmem_limit_bytes=VMEM_LIMIT),
        name="s5_scan",
    )(x_lat, x_ctx, t_mat, m_mat, r_mat, a_pow)


def _s5_matrices(lam_re, lam_im, log_dt, b_re, b_im, c_re, c_im, d_skip):
    hp = lax.Precision.HIGHEST
    dt = jnp.exp(log_dt)[..., None]
    k = jnp.arange(CHUNK + 1, dtype=F32)[:, None, None, None]
    mag = jnp.exp(k * (lam_re * dt))
    ang = k * (lam_im * dt)
    pw_re = mag * jnp.cos(ang)
    pw_im = mag * jnp.sin(ang)
    abar_re, abar_im = pw_re[1], pw_im[1]
    den = lam_re * lam_re + lam_im * lam_im
    num_re = abar_re - 1
    f_re = (num_re * lam_re + abar_im * lam_im) / den
    f_im = (abar_im * lam_re - num_re * lam_im) / den
    bb_re = f_re[..., None] * b_re - f_im[..., None] * b_im
    bb_im = f_re[..., None] * b_im + f_im[..., None] * b_re

    ca_re = c_re[None] * pw_re[:, :, :, None, :] - c_im[None] * pw_im[:, :, :, None, :]
    ca_im = c_re[None] * pw_im[:, :, :, None, :] + c_im[None] * pw_re[:, :, :, None, :]
    kern = (jnp.einsum('kdghp,dgpj->kdghj', ca_re[:CHUNK], bb_re, precision=hp)
            - jnp.einsum('kdghp,dgpj->kdghj', ca_im[:CHUNK], bb_im, precision=hp))
    q = jnp.arange(CHUNK)
    tau = q[None, :] - q[:, None]
    kf = kern[jnp.clip(tau, 0, CHUNK - 1), 0]
    kb = kern[jnp.clip(-tau, 0, CHUNK - 1), 1]
    tmask = tau[:, :, None, None, None]
    t5 = jnp.where(tmask >= 0, kf, 0.0) + jnp.where(tmask <= 0, kb, 0.0)
    t5 = jnp.transpose(t5, (2, 0, 4, 1, 3))
    eye_q = jnp.eye(CHUNK, dtype=F32)[None, :, None, :, None]
    eye_h = jnp.eye(S5_GROUP, dtype=F32)[None, None, :, None, :]
    t5 = t5 + eye_q * eye_h * d_skip[:, None, None, None, :]
    t_mat = t5.reshape(S5_GROUPS, CHUNK_COLS, CHUNK_COLS)

    def _cmul_b(p_re, p_im, d):
        re = p_re[:, :, None, :] * jnp.swapaxes(bb_re[d], 1, 2)[None] - p_im[:, :, None, :] * jnp.swapaxes(bb_im[d], 1, 2)[None]
        im = p_re[:, :, None, :] * jnp.swapaxes(bb_im[d], 1, 2)[None] + p_im[:, :, None, :] * jnp.swapaxes(bb_re[d], 1, 2)[None]
        return jnp.transpose(re, (1, 0, 2, 3)), jnp.transpose(im, (1, 0, 2, 3))

    mf_re, mf_im = _cmul_b(pw_re[CHUNK - 1::-1, 0][:CHUNK], pw_im[CHUNK - 1::-1, 0][:CHUNK], 0)
    mb_re, mb_im = _cmul_b(pw_re[:CHUNK, 1], pw_im[:CHUNK, 1], 1)
    m_mat = jnp.concatenate([mf_re, mb_re, mf_im, mb_im], axis=-1).reshape(S5_GROUPS, CHUNK_COLS, 4 * S5_STATE)

    wf_re = jnp.transpose(ca_re[1:CHUNK + 1, 0], (1, 3, 0, 2))
    wf_im = jnp.transpose(ca_im[1:CHUNK + 1, 0], (1, 3, 0, 2))
    wb_re = jnp.transpose(ca_re[CHUNK:0:-1, 1], (1, 3, 0, 2))
    wb_im = jnp.transpose(ca_im[CHUNK:0:-1, 1], (1, 3, 0, 2))
    r_mat = jnp.concatenate([wf_re, wb_re, -wf_im, -wb_im], axis=1).reshape(S5_GROUPS, 4 * S5_STATE, CHUNK_COLS)

    a_pow = jnp.stack([
        jnp.concatenate([pw_re[CHUNK, 0], pw_re[CHUNK, 1]], axis=-1),
        jnp.concatenate([pw_im[CHUNK, 0], pw_im[CHUNK, 1]], axis=-1)], axis=1)
    return t_mat.astype(BF16), m_mat.astype(BF16), r_mat.astype(BF16), a_pow


def _conv_kernel(n_tok, h_ref, w_ref, cb_ref, lg_ref, lb_ref, o_ref, pad_ref):
    halo = (CONV_K // 2) * GRID_W
    zeros = jnp.zeros((halo, CONV_WIDTH), F32)
    pad_ref[0:halo, :] = zeros
    pad_ref[halo + n_tok:, :] = zeros
    pad_ref[halo:halo + n_tok, :] = h_ref[0]

    def body(i, carry):
        base = pl.multiple_of(i * CONV_ROWS, CONV_ROWS)
        acc = jnp.zeros((CONV_ROWS, CONV_WIDTH), F32)
        for k in range(CONV_K):
            acc = acc + w_ref[k:k + 1, :] * pad_ref[pl.ds(base + k * GRID_W, CONV_ROWS), :]
        acc = acc + cb_ref[...]
        mu = jnp.mean(acc, axis=-1, keepdims=True)
        dev = acc - mu
        var = jnp.mean(dev * dev, axis=-1, keepdims=True)
        y = dev * lax.rsqrt(var + EPS_LN) * lg_ref[...] + lb_ref[...]
        o_ref[0, pl.ds(base, CONV_ROWS), :] = (y * jax.nn.sigmoid(y)).astype(BF16)
        return carry

    lax.fori_loop(0, n_tok // CONV_ROWS, body, 0)


def _conv_module(hc, w, cb, lg, lb):
    bsz, n, _ = hc.shape
    halo = (CONV_K // 2) * GRID_W
    vec = pl.BlockSpec((1, CONV_WIDTH), lambda b: (0, 0))
    return pl.pallas_call(
        functools.partial(_conv_kernel, n),
        grid=(bsz,),
        in_specs=[
            pl.BlockSpec((1, n, CONV_WIDTH), lambda b: (b, 0, 0)),
            pl.BlockSpec((CONV_K, CONV_WIDTH), lambda b: (0, 0)),
            vec, vec, vec,
        ],
        out_specs=pl.BlockSpec((1, n, CONV_WIDTH), lambda b: (b, 0, 0)),
        out_shape=jax.ShapeDtypeStruct((bsz, n, CONV_WIDTH), BF16),
        scratch_shapes=[pltpu.VMEM((n + 2 * halo, CONV_WIDTH), F32)],
        compiler_params=pltpu.CompilerParams(
            dimension_semantics=("parallel",), vmem_limit_bytes=VMEM_LIMIT),
        name="conv_module",
    )(hc, w, cb.reshape(1, -1), lg.reshape(1, -1), lb.reshape(1, -1))


def _out_kernel(x_ref, ys_ref, yc_ref, g1_ref, sh2_ref, sc2_ref, g2_ref, n2_ref, fg_ref,
                wglu_ref, wout_ref, w1_ref, w2_ref, o_ref):
    x = x_ref[0]
    gl = jax.nn.gelu(ys_ref[0].astype(F32))
    s5o = gl * jax.nn.sigmoid(_dot(gl.astype(BF16), wglu_ref[...]))
    mix = _dot(s5o.astype(BF16), wout_ref[:S5_WIDTH, :]) + _dot(yc_ref[0], wout_ref[S5_WIDTH:, :])
    h1 = x + g1_ref[0] * mix
    a2 = (_rms(h1, n2_ref[...]) * (1.0 + sc2_ref[0]) + sh2_ref[0]).astype(BF16)
    ff_tile = 1024
    acc = jnp.zeros_like(h1)
    for j in range(D_FF // ff_tile):
        hid = jnp.maximum(_dot(a2, w1_ref[:, j * ff_tile:(j + 1) * ff_tile]), 0.0)
        acc = acc + _dot((hid * hid).astype(BF16), w2_ref[j * ff_tile:(j + 1) * ff_tile, :])
    h2 = h1 + g2_ref[0] * acc
    o_ref[0] = _rms(h2, fg_ref[...])


def _out_block(x, ys, yc, g1, sh2, sc2, g2, n2, fg, wglu, wout, w1, w2):
    bsz, n, _ = x.shape
    tile = TOK_TILE
    modv = pl.BlockSpec((1, 1, D_MODEL), lambda b, i: (b, 0, 0))
    vec = pl.BlockSpec((1, D_MODEL), lambda b, i: (0, 0))

    def const(shape):
        return pl.BlockSpec(shape, lambda b, i: (0, 0), pipeline_mode=pl.Buffered(1))

    return pl.pallas_call(
        _out_kernel,
        grid=(bsz, n // tile),
        in_specs=[
            pl.BlockSpec((1, tile, D_MODEL), lambda b, i: (b, i, 0)),
            pl.BlockSpec((1, tile, S5_WIDTH), lambda b, i: (b, i, 0)),
            pl.BlockSpec((1, tile, CONV_WIDTH), lambda b, i: (b, i, 0)),
            modv, modv, modv, modv, vec, vec,
            const((S5_WIDTH, S5_WIDTH)),
            const((D_MODEL, D_MODEL)),
            const((D_MODEL, D_FF)),
            const((D_FF, D_MODEL)),
        ],
        out_specs=pl.BlockSpec((1, tile, D_MODEL), lambda b, i: (b, i, 0)),
        out_shape=jax.ShapeDtypeStruct((bsz, n, D_MODEL), F32),
        compiler_params=pltpu.CompilerParams(
            dimension_semantics=("parallel", "parallel"), vmem_limit_bytes=VMEM_LIMIT),
        name="out_block",
    )(x, ys, yc, g1, sh2, sc2, g2, n2, fg, wglu, wout, w1, w2)


def _to_chunks(u, bsz):
    n = u.shape[1]
    u = u.reshape(bsz, n // CHUNK, CHUNK, S5_GROUPS, S5_GROUP)
    u = jnp.transpose(u, (3, 1, 0, 2, 4))
    return u.reshape(S5_GROUPS, (n // CHUNK) * bsz, CHUNK_COLS)


def _from_chunks(y, bsz):
    n = y.shape[1] // bsz * CHUNK
    y = y.reshape(S5_GROUPS, n // CHUNK, bsz, CHUNK, S5_GROUP)
    y = jnp.transpose(y, (2, 1, 3, 0, 4))
    return y.reshape(bsz, n, S5_WIDTH)


def kernel(x, c, ctx, c_ctx, ada_w, ada_b, norm1_g, w_in, s5_lam_re, s5_lam_im, s5_log_dt, s5_b_re, s5_b_im, s5_c_re, s5_c_im, s5_d, s5_w_glu, conv_w, conv_b, conv_ln_g, conv_ln_b, w_out, norm2_g, mlp_w1, mlp_w2, final_g):
    bsz = x.shape[0]
    assert bsz + 1 <= MOD_ROWS
    layer = 0

    cc = jnp.concatenate([c, c_ctx[None, :], jnp.zeros((MOD_ROWS - bsz - 1, D_MODEL), F32)], axis=0)
    mod = _mod_table(cc, ada_w[layer], ada_b[layer])
    sh1, sc1, g1, sh2, sc2, g2 = [m.reshape(bsz, 1, D_MODEL) for m in jnp.split(mod[:bsz], 6, axis=-1)]
    csh1 = jnp.broadcast_to(mod[bsz, :D_MODEL].reshape(1, 1, D_MODEL), (bsz, 1, D_MODEL))
    csc1 = jnp.broadcast_to(mod[bsz, D_MODEL:2 * D_MODEL].reshape(1, 1, D_MODEL), (bsz, 1, D_MODEL))

    n1 = norm1_g[layer].reshape(1, D_MODEL)
    w_in_bf = w_in[layer].astype(BF16)
    u, hc = _in_proj(x, sh1, sc1, n1, w_in_bf, TOK_TILE)
    uc, _ = _in_proj(ctx, csh1, csc1, n1, w_in_bf, ctx.shape[1])

    t_mat, m_mat, r_mat, a_pow = _s5_matrices(
        s5_lam_re[layer], s5_lam_im[layer], s5_log_dt[layer], s5_b_re[layer], s5_b_im[layer],
        s5_c_re[layer], s5_c_im[layer], s5_d[layer])
    y_chunks = _s5_scan(_to_chunks(u, bsz), _to_chunks(uc, bsz), t_mat, m_mat, r_mat, a_pow, bsz)
    y_s5 = _from_chunks(y_chunks, bsz)

    y_conv = _conv_module(hc, conv_w[layer], conv_b[layer], conv_ln_g[layer], conv_ln_b[layer])

    return _out_block(
        x, y_s5, y_conv, g1, sh2, sc2, g2,
        norm2_g[layer].reshape(1, D_MODEL), final_g.reshape(1, D_MODEL),
        s5_w_glu[layer].astype(BF16), w_out[layer].astype(BF16),
        mlp_w1[layer].astype(BF16), mlp_w2[layer].astype(BF16))
```

```python
import functools

import numpy as np
import jax
import jax.numpy as jnp
from jax import lax
from jax.experimental import pallas as pl
from jax.experimental.pallas import tpu as pltpu

F32 = jnp.float32
BF16 = jnp.bfloat16

D_MODEL = 1024
S5_WIDTH = 512
S5_GROUP = 16
S5_GROUPS = 32
S5_STATE = 64
CONV_WIDTH = 512
CONV_K = 31
GRID_W = 64
IN_COLS = S5_WIDTH + 2 * CONV_WIDTH
D_FF = 4 * D_MODEL
EPS_RMS = 1e-6
EPS_LN = 1e-5

CHUNK = 16
CHUNK_COLS = CHUNK * S5_GROUP
MOD_ROWS = 16
BATCH = 8
TOK_ROWS = 64
STEP_ROWS = BATCH * TOK_ROWS
STEP_CHUNKS = TOK_ROWS // CHUNK
LANES = 128
SLOTS = LANES // S5_GROUP
LANE_BLOCKS = S5_WIDTH // LANES
PITCH = 72
CONV_ROWS = 32
VMEM_LIMIT = 56 * 1024 * 1024


def _dot(a, b):
    return jnp.dot(a, b, preferred_element_type=F32)


def _rms(x, g):
    return x * lax.rsqrt(jnp.mean(x * x, axis=-1, keepdims=True) + EPS_RMS) * g


def _mod_kernel(c_ref, w_ref, b_ref, o_ref):
    c = c_ref[...]
    s = c * jax.nn.sigmoid(c)
    w = w_ref[...]
    s_hi = s.astype(BF16)
    s_lo = (s - s_hi.astype(F32)).astype(BF16)
    w_hi = w.astype(BF16)
    w_lo = (w - w_hi.astype(F32)).astype(BF16)
    o_ref[...] = _dot(s_hi, w_hi) + _dot(s_lo, w_hi) + _dot(s_hi, w_lo) + b_ref[...]


def _mod_table(cc, ada_w, ada_b):
    n_out = ada_w.shape[1]
    tn = 1536
    return pl.pallas_call(
        _mod_kernel,
        grid=(n_out // tn,),
        in_specs=[
            pl.BlockSpec((MOD_ROWS, D_MODEL), lambda j: (0, 0)),
            pl.BlockSpec((D_MODEL, tn), lambda j: (0, j)),
            pl.BlockSpec((1, tn), lambda j: (0, j)),
        ],
        out_specs=pl.BlockSpec((MOD_ROWS, tn), lambda j: (0, j)),
        out_shape=jax.ShapeDtypeStruct((MOD_ROWS, n_out), F32),
        compiler_params=pltpu.CompilerParams(vmem_limit_bytes=VMEM_LIMIT),
        name="mod_table",
    )(cc, ada_w, ada_b.reshape(1, n_out))


def _slot_masks():
    slot = lax.broadcasted_iota(jnp.int32, (BATCH, LANES), 1) // S5_GROUP
    return [slot == j for j in range(SLOTS)]


def _chunk_lane_perm():
    lane = np.arange(CHUNK_COLS)
    half, slot, h = lane // LANES, (lane % LANES) // S5_GROUP, lane % S5_GROUP
    g = np.arange(S5_GROUPS)[:, None]
    q = 8 * half[None, :] + (slot[None, :] - g) % SLOTS
    return q * S5_GROUP + h[None, :]


def _to_chunk_layout(z_scr, xs_scr):
    masks = _slot_masks()
    for cl in range(STEP_CHUNKS):
        for vb in range(LANE_BLOCKS):
            for half in range(2):
                rot = []
                for ql in range(SLOTS):
                    piece = z_scr[vb, pl.ds(cl * CHUNK + 8 * half + ql, BATCH, stride=PITCH), :]
                    rot.append(pltpu.roll(piece, ql * S5_GROUP, axis=1) if ql else piece)
                for s in range(SLOTS):
                    out = rot[(-s) % SLOTS]
                    for j in range(1, SLOTS):
                        out = jnp.where(masks[j], rot[(j - s) % SLOTS], out)
                    col = (2 * (SLOTS * vb + s) + half) * LANES
                    xs_scr[cl * BATCH:(cl + 1) * BATCH, col:col + LANES] = out


def _from_chunk_layout(ys, u_scr):
    masks = _slot_masks()
    for cl in range(STEP_CHUNKS):
        for vb in range(LANE_BLOCKS):
            for half in range(2):
                src = []
                for s in range(SLOTS):
                    col = (2 * (SLOTS * vb + s) + half) * LANES
                    src.append(ys[cl * BATCH:(cl + 1) * BATCH, col:col + LANES])
                for ql in range(SLOTS):
                    out = src[(-ql) % SLOTS]
                    for j in range(1, SLOTS):
                        out = jnp.where(masks[j], src[(j - ql) % SLOTS], out)
                    if ql:
                        out = pltpu.roll(out, LANES - ql * S5_GROUP, axis=1)
                    u_scr[vb, pl.ds(cl * CHUNK + 8 * half + ql, BATCH, stride=PITCH), :] = out


def _inproj_kernel(x_ref, sh_ref, sc_ref, g_ref, w_ref, xs_ref, hc_ref, z_scr, xs_scr):
    x = x_ref[...]
    a = _rms(x, g_ref[...]) * (1.0 + sc_ref[...]) + sh_ref[...]
    z = _dot(a.reshape(STEP_ROWS, D_MODEL).astype(BF16), w_ref[...])
    v = z[:, S5_WIDTH:S5_WIDTH + CONV_WIDTH]
    gate = z[:, S5_WIDTH + CONV_WIDTH:]
    hc_ref[...] = (v * jax.nn.sigmoid(gate)).reshape(BATCH, TOK_ROWS, CONV_WIDTH)
    for vb in range(LANE_BLOCKS):
        for b in range(BATCH):
            z_scr[vb, b * PITCH:b * PITCH + TOK_ROWS, :] = z[b * TOK_ROWS:(b + 1) * TOK_ROWS, vb * LANES:(vb + 1) * LANES]
    _to_chunk_layout(z_scr, xs_scr)
    xs_ref[...] = xs_scr[...].astype(BF16)


def _in_proj(x, sh, sc, g, w_bf):
    bsz, n, _ = x.shape
    assert bsz == BATCH and n % TOK_ROWS == 0
    steps = n // TOK_ROWS
    chunk_rows = STEP_CHUNKS * BATCH
    modv = pl.BlockSpec((BATCH, 1, D_MODEL), lambda i: (0, 0, 0))
    return pl.pallas_call(
        _inproj_kernel,
        grid=(steps,),
        in_specs=[
            pl.BlockSpec((BATCH, TOK_ROWS, D_MODEL), lambda i: (0, i, 0)),
            modv, modv,
            pl.BlockSpec((1, D_MODEL), lambda i: (0, 0)),
            pl.BlockSpec((D_MODEL, IN_COLS), lambda i: (0, 0)),
        ],
        out_specs=[
            pl.BlockSpec((chunk_rows, S5_GROUPS * CHUNK_COLS), lambda i: (i, 0)),
            pl.BlockSpec((BATCH, TOK_ROWS, CONV_WIDTH), lambda i: (0, i, 0)),
        ],
        out_shape=[
            jax.ShapeDtypeStruct((steps * chunk_rows, S5_GROUPS * CHUNK_COLS), BF16),
            jax.ShapeDtypeStruct((bsz, n, CONV_WIDTH), F32),
        ],
        scratch_shapes=[
            pltpu.VMEM((LANE_BLOCKS, BATCH * PITCH, LANES), F32),
            pltpu.VMEM((chunk_rows, S5_GROUPS * CHUNK_COLS), F32),
        ],
        compiler_params=pltpu.CompilerParams(
            dimension_semantics=("parallel",), vmem_limit_bytes=VMEM_LIMIT),
        name="in_proj",
    )(x, sh, sc, g, w_bf)


def _s5_kernel(n_lat, n_ctx, xl_ref, xc_ref, t_ref, m_ref, r_ref, a_ref, y_ref, e_ref, p_ref, ec_ref):
    half = 2 * S5_STATE
    m = m_ref[0]
    ec_ref[...] = _dot(xc_ref[...], m)
    e_ref[...] = _dot(xl_ref[...], m)
    a_re = jnp.broadcast_to(a_ref[0, 0:1, :], (8, half))
    a_im = jnp.broadcast_to(a_ref[0, 1:2, :], (8, half))
    is_fwd = lax.broadcasted_iota(jnp.int32, (8, half), 1) < S5_STATE

    def advance(src_ref, i, j, s_re, s_im):
        ef = src_ref[pl.ds(pl.multiple_of(i * 8, 8), 8), :]
        eb = src_ref[pl.ds(pl.multiple_of(j * 8, 8), 8), :]
        e_re = jnp.where(is_fwd, ef[:, :half], eb[:, :half])
        e_im = jnp.where(is_fwd, ef[:, half:], eb[:, half:])
        return (a_re * s_re - a_im * s_im + e_re, a_re * s_im + a_im * s_re + e_im)

    def ctx_body(i, s):
        return advance(ec_ref, i, n_ctx - 1 - i, *s)

    zero = jnp.zeros((8, half), F32)
    s = lax.fori_loop(0, n_ctx, ctx_body, (zero, zero))

    def lat_body(i, s):
        s_re, s_im = s
        j = n_lat - 1 - i
        ri = pl.ds(pl.multiple_of(i * 8, 8), 8)
        rj = pl.ds(pl.multiple_of(j * 8, 8), 8)
        p_ref[ri, 0:S5_STATE] = s_re[:, :S5_STATE]
        p_ref[rj, S5_STATE:half] = s_re[:, S5_STATE:]
        p_ref[ri, half:half + S5_STATE] = s_im[:, :S5_STATE]
        p_ref[rj, half + S5_STATE:] = s_im[:, S5_STATE:]
        return advance(e_ref, i, j, s_re, s_im)

    lax.fori_loop(0, n_lat, lat_body, s)
    y = _dot(xl_ref[...], t_ref[0]) + _dot(p_ref[...].astype(BF16), r_ref[0])
    y_ref[...] = y.astype(BF16)


def _s5_scan(x_lat, x_ctx, t_mat, m_mat, r_mat, a_pow, bsz):
    rows = x_lat.shape[0]
    rows_c = x_ctx.shape[0]
    mat_spec = pl.BlockSpec((1, CHUNK_COLS, CHUNK_COLS), lambda g: (g, 0, 0))
    return pl.pallas_call(
        functools.partial(_s5_kernel, rows // bsz, rows_c // bsz),
        grid=(S5_GROUPS,),
        in_specs=[
            pl.BlockSpec((rows, CHUNK_COLS), lambda g: (0, g)),
            pl.BlockSpec((rows_c, CHUNK_COLS), lambda g: (0, g)),
            mat_spec, mat_spec, mat_spec,
            pl.BlockSpec((1, 2, 2 * S5_STATE), lambda g: (g, 0, 0)),
        ],
        out_specs=pl.BlockSpec((rows, CHUNK_COLS), lambda g: (0, g)),
        out_shape=jax.ShapeDtypeStruct((rows, S5_GROUPS * CHUNK_COLS), BF16),
        scratch_shapes=[
            pltpu.VMEM((rows, CHUNK_COLS), F32),
            pltpu.VMEM((rows, CHUNK_COLS), F32),
            pltpu.VMEM((rows_c, CHUNK_COLS), F32),
        ],
        compiler_params=pltpu.CompilerParams(
            dimension_semantics=("parallel",), vmem_limit_bytes=VMEM_LIMIT),
        name="s5_scan",
    )(x_lat, x_ctx, t_mat, m_mat, r_mat, a_pow)


def _s5_matrices(lam_re, lam_im, log_dt, b_re, b_im, c_re, c_im, d_skip):
    hp = lax.Precision.HIGHEST
    dt = jnp.exp(log_dt)[..., None]
    k = jnp.arange(CHUNK + 1, dtype=F32)[:, None, None, None]
    mag = jnp.exp(k * (lam_re * dt))
    ang = k * (lam_im * dt)
    pw_re = mag * jnp.cos(ang)
    pw_im = mag * jnp.sin(ang)
    abar_re, abar_im = pw_re[1], pw_im[1]
    den = lam_re * lam_re + lam_im * lam_im
    num_re = abar_re - 1
    f_re = (num_re * lam_re + abar_im * lam_im) / den
    f_im = (abar_im * lam_re - num_re * lam_im) / den
    bb_re = f_re[..., None] * b_re - f_im[..., None] * b_im
    bb_im = f_re[..., None] * b_im + f_im[..., None] * b_re

    ca_re = c_re[None] * pw_re[:, :, :, None, :] - c_im[None] * pw_im[:, :, :, None, :]
    ca_im = c_re[None] * pw_im[:, :, :, None, :] + c_im[None] * pw_re[:, :, :, None, :]
    kern = (jnp.einsum('kdghp,dgpj->kdghj', ca_re[:CHUNK], bb_re, precision=hp)
            - jnp.einsum('kdghp,dgpj->kdghj', ca_im[:CHUNK], bb_im, precision=hp))
    q = jnp.arange(CHUNK)
    tau = q[None, :] - q[:, None]
    kf = kern[jnp.clip(tau, 0, CHUNK - 1), 0]
    kb = kern[jnp.clip(-tau, 0, CHUNK - 1), 1]
    tmask = tau[:, :, None, None, None]
    t5 = jnp.where(tmask >= 0, kf, 0.0) + jnp.where(tmask <= 0, kb, 0.0)
    t5 = jnp.transpose(t5, (2, 0, 4, 1, 3))
    eye_q = jnp.eye(CHUNK, dtype=F32)[None, :, None, :, None]
    eye_h = jnp.eye(S5_GROUP, dtype=F32)[None, None, :, None, :]
    t5 = t5 + eye_q * eye_h * d_skip[:, None, None, None, :]
    t_mat = t5.reshape(S5_GROUPS, CHUNK_COLS, CHUNK_COLS)

    def _cmul_b(p_re, p_im, d):
        bt_re = jnp.swapaxes(bb_re[d], 1, 2)[None]
        bt_im = jnp.swapaxes(bb_im[d], 1, 2)[None]
        re = p_re[:, :, None, :] * bt_re - p_im[:, :, None, :] * bt_im
        im = p_re[:, :, None, :] * bt_im + p_im[:, :, None, :] * bt_re
        return jnp.transpose(re, (1, 0, 2, 3)), jnp.transpose(im, (1, 0, 2, 3))

    mf_re, mf_im = _cmul_b(pw_re[CHUNK - 1::-1, 0], pw_im[CHUNK - 1::-1, 0], 0)
    mb_re, mb_im = _cmul_b(pw_re[:CHUNK, 1], pw_im[:CHUNK, 1], 1)
    m_mat = jnp.concatenate([mf_re, mb_re, mf_im, mb_im], axis=-1).reshape(S5_GROUPS, CHUNK_COLS, 4 * S5_STATE)

    wf_re = jnp.transpose(ca_re[1:CHUNK + 1, 0], (1, 3, 0, 2))
    wf_im = jnp.transpose(ca_im[1:CHUNK + 1, 0], (1, 3, 0, 2))
    wb_re = jnp.transpose(ca_re[CHUNK:0:-1, 1], (1, 3, 0, 2))
    wb_im = jnp.transpose(ca_im[CHUNK:0:-1, 1], (1, 3, 0, 2))
    r_mat = jnp.concatenate([wf_re, wb_re, -wf_im, -wb_im], axis=1).reshape(S5_GROUPS, 4 * S5_STATE, CHUNK_COLS)

    a_pow = jnp.stack([
        jnp.concatenate([pw_re[CHUNK, 0], pw_re[CHUNK, 1]], axis=-1),
        jnp.concatenate([pw_im[CHUNK, 0], pw_im[CHUNK, 1]], axis=-1)], axis=1)

    perm = _chunk_lane_perm()
    gi = np.arange(S5_GROUPS)
    t_mat = t_mat[gi[:, None, None], perm[:, :, None], perm[:, None, :]]
    m_mat = m_mat[gi[:, None], perm]
    r_mat = r_mat[gi[:, None, None], np.arange(4 * S5_STATE)[None, :, None], perm[:, None, :]]
    return t_mat.astype(BF16), m_mat.astype(BF16), r_mat.astype(BF16), a_pow


def _conv_kernel(n_tok, h_ref, w_ref, cb_ref, lg_ref, lb_ref, o_ref, pad_ref):
    halo = (CONV_K // 2) * GRID_W
    zeros = jnp.zeros((halo, CONV_WIDTH), F32)
    pad_ref[0:halo, :] = zeros
    pad_ref[halo + n_tok:, :] = zeros
    pad_ref[halo:halo + n_tok, :] = h_ref[0]

    def body(i, carry):
        base = pl.multiple_of(i * CONV_ROWS, CONV_ROWS)
        acc = jnp.zeros((CONV_ROWS, CONV_WIDTH), F32)
        for k in range(CONV_K):
            acc = acc + w_ref[k:k + 1, :] * pad_ref[pl.ds(base + k * GRID_W, CONV_ROWS), :]
        acc = acc + cb_ref[...]
        mu = jnp.mean(acc, axis=-1, keepdims=True)
        dev = acc - mu
        var = jnp.mean(dev * dev, axis=-1, keepdims=True)
        y = dev * lax.rsqrt(var + EPS_LN) * lg_ref[...] + lb_ref[...]
        o_ref[0, pl.ds(base, CONV_ROWS), :] = (y * jax.nn.sigmoid(y)).astype(BF16)
        return carry

    lax.fori_loop(0, n_tok // CONV_ROWS, body, 0)


def _conv_module(hc, w, cb, lg, lb):
    bsz, n, _ = hc.shape
    halo = (CONV_K // 2) * GRID_W
    vec = pl.BlockSpec((1, CONV_WIDTH), lambda b: (0, 0))
    return pl.pallas_call(
        functools.partial(_conv_kernel, n),
        grid=(bsz,),
        in_specs=[
            pl.BlockSpec((1, n, CONV_WIDTH), lambda b: (b, 0, 0)),
            pl.BlockSpec((CONV_K, CONV_WIDTH), lambda b: (0, 0)),
            vec, vec, vec,
        ],
        out_specs=pl.BlockSpec((1, n, CONV_WIDTH), lambda b: (b, 0, 0)),
        out_shape=jax.ShapeDtypeStruct((bsz, n, CONV_WIDTH), BF16),
        scratch_shapes=[pltpu.VMEM((n + 2 * halo, CONV_WIDTH), F32)],
        compiler_params=pltpu.CompilerParams(
            dimension_semantics=("parallel",), vmem_limit_bytes=VMEM_LIMIT),
        name="conv_module",
    )(hc, w, cb.reshape(1, -1), lg.reshape(1, -1), lb.reshape(1, -1))


def _out_kernel(x_ref, ys_ref, yc_ref, g1_ref, sh2_ref, sc2_ref, g2_ref, n2_ref, fg_ref,
                wglu_ref, wout_ref, w1_ref, w2_ref, o_ref, u_scr):
    _from_chunk_layout(ys_ref[...].astype(F32), u_scr)
    y_s5 = jnp.concatenate(
        [jnp.concatenate([u_scr[vb, b * PITCH:b * PITCH + TOK_ROWS, :] for b in range(BATCH)], axis=0)
         for vb in range(LANE_BLOCKS)], axis=1)
    gl = jax.nn.gelu(y_s5)
    s5o = gl * jax.nn.sigmoid(_dot(gl.astype(BF16), wglu_ref[...]))
    yc = yc_ref[...].reshape(STEP_ROWS, CONV_WIDTH)
    mix = _dot(s5o.astype(BF16), wout_ref[:S5_WIDTH, :]) + _dot(yc, wout_ref[S5_WIDTH:, :])
    h1 = x_ref[...] + g1_ref[...] * mix.reshape(BATCH, TOK_ROWS, D_MODEL)
    a2 = _rms(h1, n2_ref[...]) * (1.0 + sc2_ref[...]) + sh2_ref[...]
    a2 = a2.reshape(STEP_ROWS, D_MODEL).astype(BF16)
    ff_tile = 1024
    acc = jnp.zeros((STEP_ROWS, D_MODEL), F32)
    for j in range(D_FF // ff_tile):
        hid = jnp.maximum(_dot(a2, w1_ref[:, j * ff_tile:(j + 1) * ff_tile]), 0.0)
        acc = acc + _dot((hid * hid).astype(BF16), w2_ref[j * ff_tile:(j + 1) * ff_tile, :])
    h2 = h1 + g2_ref[...] * acc.reshape(BATCH, TOK_ROWS, D_MODEL)
    o_ref[...] = _rms(h2, fg_ref[...])


def _out_block(x, ys, yc, g1, sh2, sc2, g2, n2, fg, wglu, wout, w1, w2):
    bsz, n, _ = x.shape
    assert bsz == BATCH and n % TOK_ROWS == 0
    chunk_rows = STEP_CHUNKS * BATCH
    modv = pl.BlockSpec((BATCH, 1, D_MODEL), lambda i: (0, 0, 0))
    vec = pl.BlockSpec((1, D_MODEL), lambda i: (0, 0))

    def const(shape):
        return pl.BlockSpec(shape, lambda i: (0, 0), pipeline_mode=pl.Buffered(1))

    return pl.pallas_call(
        _out_kernel,
        grid=(n // TOK_ROWS,),
        in_specs=[
            pl.BlockSpec((BATCH, TOK_ROWS, D_MODEL), lambda i: (0, i, 0)),
            pl.BlockSpec((chunk_rows, S5_GROUPS * CHUNK_COLS), lambda i: (i, 0)),
            pl.BlockSpec((BATCH, TOK_ROWS, CONV_WIDTH), lambda i: (0, i, 0)),
            modv, modv, modv, modv, vec, vec,
            const((S5_WIDTH, S5_WIDTH)),
            const((D_MODEL, D_MODEL)),
            const((D_MODEL, D_FF)),
            const((D_FF, D_MODEL)),
        ],
        out_specs=pl.BlockSpec((BATCH, TOK_ROWS, D_MODEL), lambda i: (0, i, 0)),
        out_shape=jax.ShapeDtypeStruct((bsz, n, D_MODEL), F32),
        scratch_shapes=[pltpu.VMEM((LANE_BLOCKS, BATCH * PITCH, LANES), F32)],
        compiler_params=pltpu.CompilerParams(
            dimension_semantics=("parallel",), vmem_limit_bytes=VMEM_LIMIT),
        name="out_block",
    )(x, ys, yc, g1, sh2, sc2, g2, n2, fg, wglu, wout, w1, w2)


def kernel(x, c, ctx, c_ctx, ada_w, ada_b, norm1_g, w_in, s5_lam_re, s5_lam_im, s5_log_dt, s5_b_re, s5_b_im, s5_c_re, s5_c_im, s5_d, s5_w_glu, conv_w, conv_b, conv_ln_g, conv_ln_b, w_out, norm2_g, mlp_w1, mlp_w2, final_g):
    bsz = x.shape[0]
    assert bsz == BATCH and bsz + 1 <= MOD_ROWS
    layer = 0

    cc = jnp.concatenate([c, c_ctx[None, :], jnp.zeros((MOD_ROWS - bsz - 1, D_MODEL), F32)], axis=0)
    mod = _mod_table(cc, ada_w[layer], ada_b[layer])
    sh1, sc1, g1, sh2, sc2, g2 = [m.reshape(bsz, 1, D_MODEL) for m in jnp.split(mod[:bsz], 6, axis=-1)]
    csh1 = jnp.broadcast_to(mod[bsz, :D_MODEL].reshape(1, 1, D_MODEL), (bsz, 1, D_MODEL))
    csc1 = jnp.broadcast_to(mod[bsz, D_MODEL:2 * D_MODEL].reshape(1, 1, D_MODEL), (bsz, 1, D_MODEL))

    n1 = norm1_g[layer].reshape(1, D_MODEL)
    w_in_bf = w_in[layer].astype(BF16)
    x_lat, hc = _in_proj(x, sh1, sc1, n1, w_in_bf)
    x_ctx, _ = _in_proj(ctx, csh1, csc1, n1, w_in_bf)

    t_mat, m_mat, r_mat, a_pow = _s5_matrices(
        s5_lam_re[layer], s5_lam_im[layer], s5_log_dt[layer], s5_b_re[layer], s5_b_im[layer],
        s5_c_re[layer], s5_c_im[layer], s5_d[layer])
    y_s5 = _s5_scan(x_lat, x_ctx, t_mat, m_mat, r_mat, a_pow, bsz)

    y_conv = _conv_module(hc, conv_w[layer], conv_b[layer], conv_ln_g[layer], conv_ln_b[layer])

    return _out_block(
        x, y_s5, y_conv, g1, sh2, sc2, g2,
        norm2_g[layer].reshape(1, D_MODEL), final_g.reshape(1, D_MODEL),
        s5_w_glu[layer].astype(BF16), w_out[layer].astype(BF16),
        mlp_w1[layer].astype(BF16), mlp_w2[layer].astype(BF16))
```

```python
import functools

import jax
import jax.numpy as jnp
from jax import lax
from jax.experimental import pallas as pl
from jax.experimental.pallas import tpu as pltpu

F32 = jnp.float32
BF16 = jnp.bfloat16

D_MODEL = 1024
S5_WIDTH = 512
S5_GROUP = 16
S5_GROUPS = 32
S5_STATE = 64
CONV_WIDTH = 512
CONV_K = 31
GRID_W = 64
IN_COLS = S5_WIDTH + 2 * CONV_WIDTH
D_FF = 4 * D_MODEL
EPS_RMS = 1e-6
EPS_LN = 1e-5

CHUNK = 16
CHUNK_COLS = CHUNK * S5_GROUP
MOD_ROWS = 16
BATCH = 8
TOK_ROWS = 64
STEP_ROWS = BATCH * TOK_ROWS
STEP_CHUNKS = TOK_ROWS // CHUNK
LANES = 128
SLOTS = LANES // S5_GROUP
LANE_BLOCKS = S5_WIDTH // LANES
PITCH = 72
CONV_ROWS = 32
VMEM_LIMIT = 56 * 1024 * 1024


def _dot(a, b):
    return jnp.dot(a, b, preferred_element_type=F32)


def _rms(x, g):
    return x * lax.rsqrt(jnp.mean(x * x, axis=-1, keepdims=True) + EPS_RMS) * g


def _mod_kernel(c_ref, w_ref, b_ref, o_ref):
    c = c_ref[...]
    s = c * jax.nn.sigmoid(c)
    w = w_ref[...]
    s_hi = s.astype(BF16)
    s_lo = (s - s_hi.astype(F32)).astype(BF16)
    w_hi = w.astype(BF16)
    w_lo = (w - w_hi.astype(F32)).astype(BF16)
    o_ref[...] = _dot(s_hi, w_hi) + _dot(s_lo, w_hi) + _dot(s_hi, w_lo) + b_ref[...]


def _mod_table(cc, ada_w, ada_b):
    n_out = ada_w.shape[1]
    tn = 1536
    return pl.pallas_call(
        _mod_kernel,
        grid=(n_out // tn,),
        in_specs=[
            pl.BlockSpec((MOD_ROWS, D_MODEL), lambda j: (0, 0)),
            pl.BlockSpec((D_MODEL, tn), lambda j: (0, j)),
            pl.BlockSpec((1, tn), lambda j: (0, j)),
        ],
        out_specs=pl.BlockSpec((MOD_ROWS, tn), lambda j: (0, j)),
        out_shape=jax.ShapeDtypeStruct((MOD_ROWS, n_out), F32),
        compiler_params=pltpu.CompilerParams(vmem_limit_bytes=VMEM_LIMIT),
        name="mod_table",
    )(cc, ada_w, ada_b.reshape(1, n_out))


def _slot_masks():
    slot = lax.broadcasted_iota(jnp.int32, (BATCH, LANES), 1) // S5_GROUP
    return [slot == j for j in range(SLOTS)]


def _to_chunk_layout(z_scr, xs_scr):
    masks = _slot_masks()
    for cl in range(STEP_CHUNKS):
        for vb in range(LANE_BLOCKS):
            for half in range(2):
                rot = []
                for ql in range(SLOTS):
                    piece = z_scr[vb, pl.ds(cl * CHUNK + 8 * half + ql, BATCH, stride=PITCH), :]
                    rot.append(pltpu.roll(piece, ql * S5_GROUP, axis=1) if ql else piece)
                for s in range(SLOTS):
                    out = rot[(-s) % SLOTS]
                    for j in range(1, SLOTS):
                        out = jnp.where(masks[j], rot[(j - s) % SLOTS], out)
                    col = (2 * (SLOTS * vb + s) + half) * LANES
                    xs_scr[cl * BATCH:(cl + 1) * BATCH, col:col + LANES] = out


def _from_chunk_layout(ys, u_scr):
    masks = _slot_masks()
    for cl in range(STEP_CHUNKS):
        for vb in range(LANE_BLOCKS):
            for half in range(2):
                src = []
                for s in range(SLOTS):
                    col = (2 * (SLOTS * vb + s) + half) * LANES
                    src.append(ys[cl * BATCH:(cl + 1) * BATCH, col:col + LANES])
                for ql in range(SLOTS):
                    out = src[(-ql) % SLOTS]
                    for j in range(1, SLOTS):
                        out = jnp.where(masks[j], src[(j - ql) % SLOTS], out)
                    if ql:
                        out = pltpu.roll(out, LANES - ql * S5_GROUP, axis=1)
                    u_scr[vb, pl.ds(cl * CHUNK + 8 * half + ql, BATCH, stride=PITCH), :] = out


def _inproj_kernel(x_ref, sh_ref, sc_ref, g_ref, w_ref, xs_ref, hc_ref, z_scr, xs_scr):
    x = x_ref[...]
    a = _rms(x, g_ref[...]) * (1.0 + sc_ref[...]) + sh_ref[...]
    z = _dot(a.reshape(STEP_ROWS, D_MODEL).astype(BF16), w_ref[...])
    v = z[:, S5_WIDTH:S5_WIDTH + CONV_WIDTH]
    gate = z[:, S5_WIDTH + CONV_WIDTH:]
    hc_ref[...] = (v * jax.nn.sigmoid(gate)).reshape(BATCH, TOK_ROWS, CONV_WIDTH)
    for vb in range(LANE_BLOCKS):
        for b in range(BATCH):
            z_scr[vb, b * PITCH:b * PITCH + TOK_ROWS, :] = z[b * TOK_ROWS:(b + 1) * TOK_ROWS, vb * LANES:(vb + 1) * LANES]
    _to_chunk_layout(z_scr, xs_scr)
    xs_ref[...] = xs_scr[...].astype(BF16)


def _in_proj(x, sh, sc, g, w_bf):
    bsz, n, _ = x.shape
    assert bsz == BATCH and n % TOK_ROWS == 0
    steps = n // TOK_ROWS
    chunk_rows = STEP_CHUNKS * BATCH
    modv = pl.BlockSpec((BATCH, 1, D_MODEL), lambda i: (0, 0, 0))
    return pl.pallas_call(
        _inproj_kernel,
        grid=(steps,),
        in_specs=[
            pl.BlockSpec((BATCH, TOK_ROWS, D_MODEL), lambda i: (0, i, 0)),
            modv, modv,
            pl.BlockSpec((1, D_MODEL), lambda i: (0, 0)),
            pl.BlockSpec((D_MODEL, IN_COLS), lambda i: (0, 0)),
        ],
        out_specs=[
            pl.BlockSpec((chunk_rows, S5_GROUPS * CHUNK_COLS), lambda i: (i, 0)),
            pl.BlockSpec((BATCH, TOK_ROWS, CONV_WIDTH), lambda i: (0, i, 0)),
        ],
        out_shape=[
            jax.ShapeDtypeStruct((steps * chunk_rows, S5_GROUPS * CHUNK_COLS), BF16),
            jax.ShapeDtypeStruct((bsz, n, CONV_WIDTH), F32),
        ],
        scratch_shapes=[
            pltpu.VMEM((LANE_BLOCKS, BATCH * PITCH, LANES), F32),
            pltpu.VMEM((chunk_rows, S5_GROUPS * CHUNK_COLS), F32),
        ],
        compiler_params=pltpu.CompilerParams(
            dimension_semantics=("parallel",), vmem_limit_bytes=VMEM_LIMIT),
        name="in_proj",
    )(x, sh, sc, g, w_bf)


def _s5_kernel(n_lat, n_ctx, xl_ref, xc_ref, t_ref, m_ref, r_ref, a_ref, y_ref, e_ref, p_ref, ec_ref):
    half = 2 * S5_STATE
    m = m_ref[0]
    ec_ref[...] = _dot(xc_ref[...], m)
    e_ref[...] = _dot(xl_ref[...], m)
    a_re = jnp.broadcast_to(a_ref[0, 0:1, :], (8, half))
    a_im = jnp.broadcast_to(a_ref[0, 1:2, :], (8, half))
    is_fwd = lax.broadcasted_iota(jnp.int32, (8, half), 1) < S5_STATE

    def advance(src_ref, i, j, s_re, s_im):
        ef = src_ref[pl.ds(pl.multiple_of(i * 8, 8), 8), :]
        eb = src_ref[pl.ds(pl.multiple_of(j * 8, 8), 8), :]
        e_re = jnp.where(is_fwd, ef[:, :half], eb[:, :half])
        e_im = jnp.where(is_fwd, ef[:, half:], eb[:, half:])
        return (a_re * s_re - a_im * s_im + e_re, a_re * s_im + a_im * s_re + e_im)

    def ctx_body(i, s):
        return advance(ec_ref, i, n_ctx - 1 - i, *s)

    zero = jnp.zeros((8, half), F32)
    s = lax.fori_loop(0, n_ctx, ctx_body, (zero, zero))

    def lat_body(i, s):
        s_re, s_im = s
        j = n_lat - 1 - i
        ri = pl.ds(pl.multiple_of(i * 8, 8), 8)
        rj = pl.ds(pl.multiple_of(j * 8, 8), 8)
        p_ref[ri, 0:S5_STATE] = s_re[:, :S5_STATE]
        p_ref[rj, S5_STATE:half] = s_re[:, S5_STATE:]
        p_ref[ri, half:half + S5_STATE] = s_im[:, :S5_STATE]
        p_ref[rj, half + S5_STATE:] = s_im[:, S5_STATE:]
        return advance(e_ref, i, j, s_re, s_im)

    lax.fori_loop(0, n_lat, lat_body, s)
    y = _dot(xl_ref[...], t_ref[0]) + _dot(p_ref[...].astype(BF16), r_ref[0])
    y_ref[...] = y.astype(BF16)


def _s5_scan(x_lat, x_ctx, t_mat, m_mat, r_mat, a_pow, bsz):
    rows = x_lat.shape[0]
    rows_c = x_ctx.shape[0]
    mat_spec = pl.BlockSpec((1, CHUNK_COLS, CHUNK_COLS), lambda g: (g, 0, 0))
    return pl.pallas_call(
        functools.partial(_s5_kernel, rows // bsz, rows_c // bsz),
        grid=(S5_GROUPS,),
        in_specs=[
            pl.BlockSpec((rows, CHUNK_COLS), lambda g: (0, g)),
            pl.BlockSpec((rows_c, CHUNK_COLS), lambda g: (0, g)),
            mat_spec, mat_spec, mat_spec,
            pl.BlockSpec((1, 2, 2 * S5_STATE), lambda g: (g, 0, 0)),
        ],
        out_specs=pl.BlockSpec((rows, CHUNK_COLS), lambda g: (0, g)),
        out_shape=jax.ShapeDtypeStruct((rows, S5_GROUPS * CHUNK_COLS), BF16),
        scratch_shapes=[
            pltpu.VMEM((rows, CHUNK_COLS), F32),
            pltpu.VMEM((rows, CHUNK_COLS), F32),
            pltpu.VMEM((rows_c, CHUNK_COLS), F32),
        ],
        compiler_params=pltpu.CompilerParams(
            dimension_semantics=("parallel",), vmem_limit_bytes=VMEM_LIMIT),
        name="s5_scan",
    )(x_lat, x_ctx, t_mat, m_mat, r_mat, a_pow)


GROUPS_PER_STEP = SLOTS


def _cpow(kk, log_mag, ang):
    mag = jnp.exp(kk * log_mag)
    return mag * jnp.cos(kk * ang), mag * jnp.sin(kk * ang)


def _repeat_rows(a, reps):
    return jnp.concatenate(
        [jnp.broadcast_to(a[r:r + 1, :], (reps, a.shape[1])) for r in range(a.shape[0])], axis=0)


def _tile_rows(a, reps):
    return jnp.concatenate([a] * reps, axis=0)


def _dot3_nt(a, b):
    dn = (((1,), (1,)), ((), ()))
    a_hi = a.astype(BF16)
    a_lo = (a - a_hi.astype(F32)).astype(BF16)
    b_hi = b.astype(BF16)
    b_lo = (b - b_hi.astype(F32)).astype(BF16)

    def nt(u, v):
        return lax.dot_general(u, v, dn, preferred_element_type=F32)

    return nt(a_hi, b_hi) + nt(a_lo, b_hi) + nt(a_hi, b_lo)


def _chunk_row_order(a, s):
    blocks = []
    for half in range(2):
        for slot in range(SLOTS):
            q = 8 * half + (slot - s) % SLOTS
            blocks.append(a[q * S5_GROUP:(q + 1) * S5_GROUP, :])
    return jnp.concatenate(blocks, axis=0)


def _s5_prep_kernel(lre_ref, lim_ref, ldt_ref, btr_ref, bti_ref, ctr_ref, cti_ref, dw_ref,
                    t_ref, m_ref, r_ref, a_ref):
    half_l = 2 * S5_STATE
    n_lag = 2 * CHUNK - 1
    is_fwd16 = lax.broadcasted_iota(jnp.int32, (CHUNK, half_l), 1) < S5_STATE
    is_fwd32 = lax.broadcasted_iota(jnp.int32, (2 * CHUNK, half_l), 1) < S5_STATE
    row16 = lax.broadcasted_iota(jnp.int32, (CHUNK, half_l), 0).astype(F32)
    row32 = lax.broadcasted_iota(jnp.int32, (2 * CHUNK, half_l), 0)
    slot_of_lane = lax.broadcasted_iota(jnp.int32, (CHUNK_COLS, LANES), 1) // S5_GROUP
    eye = (lax.broadcasted_iota(jnp.int32, (CHUNK_COLS, CHUNK_COLS), 0)
           == lax.broadcasted_iota(jnp.int32, (CHUNK_COLS, CHUNK_COLS), 1))

    for s in range(GROUPS_PER_STEP):
        lre, lim = lre_ref[s], lim_ref[s]
        dt = jnp.exp(ldt_ref[s])
        log_mag, ang = lre * dt, lim * dt
        a1_re, a1_im = _cpow(jnp.ones((1, half_l), F32), log_mag, ang)
        den = lre * lre + lim * lim
        num_re = a1_re - 1.0
        f_re = (num_re * lre + a1_im * lim) / den
        f_im = (a1_im * lre - num_re * lim) / den
        bt_re, bt_im = btr_ref[s], bti_ref[s]
        bb_re = f_re * bt_re - f_im * bt_im
        bb_im = f_re * bt_im + f_im * bt_re
        ct_re, ct_im = ctr_ref[s], cti_ref[s]

        lag = jnp.abs(row32 - (CHUNK - 1))
        aj_re, aj_im = _cpow(lag.astype(F32), log_mag, ang)
        aj_re = jnp.where(is_fwd32, jnp.where(row32 <= CHUNK - 1, aj_re, 0.0),
                          jnp.where(row32 >= CHUNK - 1, aj_re, 0.0))[:n_lag]
        aj_im = jnp.where(is_fwd32, jnp.where(row32 <= CHUNK - 1, aj_im, 0.0),
                          jnp.where(row32 >= CHUNK - 1, aj_im, 0.0))[:n_lag]
        aj_re, aj_im = _repeat_rows(aj_re, S5_GROUP), _repeat_rows(aj_im, S5_GROUP)
        bl_re, bl_im = _tile_rows(bb_re, n_lag), _tile_rows(bb_im, n_lag)
        l_re = aj_re * bl_re - aj_im * bl_im
        l_im = aj_re * bl_im + aj_im * bl_re
        cw_re, cw_im = _tile_rows(ct_re, CHUNK), _tile_rows(ct_im, CHUNK)
        kwide = _dot3_nt(jnp.concatenate([l_re, -l_im], axis=1),
                         jnp.concatenate([cw_re, cw_im], axis=1))
        halves = []
        for half in range(2):
            col = kwide[:, half * LANES:(half + 1) * LANES]
            out = None
            for slot in range(SLOTS):
                q = 8 * half + (slot - s) % SLOTS
                start = (CHUNK - 1 - q) * S5_GROUP
                win = col[start:start + CHUNK_COLS, :]
                out = win if out is None else jnp.where(slot_of_lane == slot, win, out)
            halves.append(out)
        t_nat = jnp.concatenate(halves, axis=1)
        t_mat = _chunk_row_order(t_nat, s) + jnp.where(eye, dw_ref[s], 0.0)
        t_ref[s] = t_mat.astype(BF16)

        am_re, am_im = _cpow(jnp.where(is_fwd16, CHUNK - 1.0 - row16, row16), log_mag, ang)
        am_re, am_im = _repeat_rows(am_re, S5_GROUP), _repeat_rows(am_im, S5_GROUP)
        bm_re, bm_im = _tile_rows(bb_re, CHUNK), _tile_rows(bb_im, CHUNK)
        m_nat = jnp.concatenate([am_re * bm_re - am_im * bm_im, am_re * bm_im + am_im * bm_re], axis=1)
        m_ref[s] = _chunk_row_order(m_nat, s).astype(BF16)

        ar_re, ar_im = _cpow(jnp.where(is_fwd16, row16 + 1.0, CHUNK - row16), log_mag, ang)
        ar_re, ar_im = _repeat_rows(ar_re, S5_GROUP), _repeat_rows(ar_im, S5_GROUP)
        rt_nat = jnp.concatenate([ar_re * cw_re - ar_im * cw_im, -(ar_re * cw_im + ar_im * cw_re)], axis=1)
        r_ref[s] = _chunk_row_order(rt_nat, s).T.astype(BF16)

        ap_re, ap_im = _cpow(jnp.full((1, half_l), float(CHUNK), F32), log_mag, ang)
        a_ref[s] = jnp.concatenate([ap_re, ap_im], axis=0)


def _s5_operators(lam_re, lam_im, log_dt, b_re, b_im, c_re, c_im, d_skip):
    half_l = 2 * S5_STATE

    def lanes(v):
        return jnp.transpose(v, (1, 0, 2)).reshape(S5_GROUPS, 1, half_l)

    ldt = lanes(jnp.broadcast_to(log_dt[..., None], lam_re.shape))
    bt_re = jnp.transpose(b_re, (1, 3, 0, 2)).reshape(S5_GROUPS, S5_GROUP, half_l)
    bt_im = jnp.transpose(b_im, (1, 3, 0, 2)).reshape(S5_GROUPS, S5_GROUP, half_l)
    ct_re = jnp.transpose(c_re, (1, 2, 0, 3)).reshape(S5_GROUPS, S5_GROUP, half_l)
    ct_im = jnp.transpose(c_im, (1, 2, 0, 3)).reshape(S5_GROUPS, S5_GROUP, half_l)
    dw = jnp.tile(d_skip, (1, CHUNK)).reshape(S5_GROUPS, 1, CHUNK_COLS)

    gs = GROUPS_PER_STEP
    vec = pl.BlockSpec((gs, 1, half_l), lambda i: (i, 0, 0))
    par = pl.BlockSpec((gs, S5_GROUP, half_l), lambda i: (i, 0, 0))
    mat = pl.BlockSpec((gs, CHUNK_COLS, CHUNK_COLS), lambda i: (i, 0, 0))
    mat_shape = jax.ShapeDtypeStruct((S5_GROUPS, CHUNK_COLS, CHUNK_COLS), BF16)
    return pl.pallas_call(
        _s5_prep_kernel,
        grid=(S5_GROUPS // gs,),
        in_specs=[vec, vec, vec, par, par, par, par,
                  pl.BlockSpec((gs, 1, CHUNK_COLS), lambda i: (i, 0, 0))],
        out_specs=[mat, mat, mat, pl.BlockSpec((gs, 2, half_l), lambda i: (i, 0, 0))],
        out_shape=[mat_shape, mat_shape, mat_shape,
                   jax.ShapeDtypeStruct((S5_GROUPS, 2, half_l), F32)],
        compiler_params=pltpu.CompilerParams(
            dimension_semantics=("parallel",), vmem_limit_bytes=VMEM_LIMIT),
        name="s5_operators",
    )(lanes(lam_re), lanes(lam_im), ldt, bt_re, bt_im, ct_re, ct_im, dw)


def _conv_kernel(n_tok, h_ref, w_ref, cb_ref, lg_ref, lb_ref, o_ref, pad_ref):
    halo = (CONV_K // 2) * GRID_W
    zeros = jnp.zeros((halo, CONV_WIDTH), F32)
    pad_ref[0:halo, :] = zeros
    pad_ref[halo + n_tok:, :] = zeros
    pad_ref[halo:halo + n_tok, :] = h_ref[0]

    def body(i, carry):
        base = pl.multiple_of(i * CONV_ROWS, CONV_ROWS)
        acc = jnp.zeros((CONV_ROWS, CONV_WIDTH), F32)
        for k in range(CONV_K):
            acc = acc + w_ref[k:k + 1, :] * pad_ref[pl.ds(base + k * GRID_W, CONV_ROWS), :]
        acc = acc + cb_ref[...]
        mu = jnp.mean(acc, axis=-1, keepdims=True)
        dev = acc - mu
        var = jnp.mean(dev * dev, axis=-1, keepdims=True)
        y = dev * lax.rsqrt(var + EPS_LN) * lg_ref[...] + lb_ref[...]
        o_ref[0, pl.ds(base, CONV_ROWS), :] = (y * jax.nn.sigmoid(y)).astype(BF16)
        return carry

    lax.fori_loop(0, n_tok // CONV_ROWS, body, 0)


def _conv_module(hc, w, cb, lg, lb):
    bsz, n, _ = hc.shape
    halo = (CONV_K // 2) * GRID_W
    vec = pl.BlockSpec((1, CONV_WIDTH), lambda b: (0, 0))
    return pl.pallas_call(
        functools.partial(_conv_kernel, n),
        grid=(bsz,),
        in_specs=[
            pl.BlockSpec((1, n, CONV_WIDTH), lambda b: (b, 0, 0)),
            pl.BlockSpec((CONV_K, CONV_WIDTH), lambda b: (0, 0)),
            vec, vec, vec,
        ],
        out_specs=pl.BlockSpec((1, n, CONV_WIDTH), lambda b: (b, 0, 0)),
        out_shape=jax.ShapeDtypeStruct((bsz, n, CONV_WIDTH), BF16),
        scratch_shapes=[pltpu.VMEM((n + 2 * halo, CONV_WIDTH), F32)],
        compiler_params=pltpu.CompilerParams(
            dimension_semantics=("parallel",), vmem_limit_bytes=VMEM_LIMIT),
        name="conv_module",
    )(hc, w, cb.reshape(1, -1), lg.reshape(1, -1), lb.reshape(1, -1))


def _out_kernel(x_ref, ys_ref, yc_ref, g1_ref, sh2_ref, sc2_ref, g2_ref, n2_ref, fg_ref,
                wglu_ref, wout_ref, w1_ref, w2_ref, o_ref, u_scr):
    _from_chunk_layout(ys_ref[...].astype(F32), u_scr)
    y_s5 = jnp.concatenate(
        [jnp.concatenate([u_scr[vb, b * PITCH:b * PITCH + TOK_ROWS, :] for b in range(BATCH)], axis=0)
         for vb in range(LANE_BLOCKS)], axis=1)
    gl = jax.nn.gelu(y_s5)
    s5o = gl * jax.nn.sigmoid(_dot(gl.astype(BF16), wglu_ref[...]))
    yc = yc_ref[...].reshape(STEP_ROWS, CONV_WIDTH)
    mix = _dot(s5o.astype(BF16), wout_ref[:S5_WIDTH, :]) + _dot(yc, wout_ref[S5_WIDTH:, :])
    h1 = x_ref[...] + g1_ref[...] * mix.reshape(BATCH, TOK_ROWS, D_MODEL)
    a2 = _rms(h1, n2_ref[...]) * (1.0 + sc2_ref[...]) + sh2_ref[...]
    a2 = a2.reshape(STEP_ROWS, D_MODEL).astype(BF16)
    ff_tile = 1024
    acc = jnp.zeros((STEP_ROWS, D_MODEL), F32)
    for j in range(D_FF // ff_tile):
        hid = jnp.maximum(_dot(a2, w1_ref[:, j * ff_tile:(j + 1) * ff_tile]), 0.0)
        acc = acc + _dot((hid * hid).astype(BF16), w2_ref[j * ff_tile:(j + 1) * ff_tile, :])
    h2 = h1 + g2_ref[...] * acc.reshape(BATCH, TOK_ROWS, D_MODEL)
    o_ref[...] = _rms(h2, fg_ref[...])


def _out_block(x, ys, yc, g1, sh2, sc2, g2, n2, fg, wglu, wout, w1, w2):
    bsz, n, _ = x.shape
    assert bsz == BATCH and n % TOK_ROWS == 0
    chunk_rows = STEP_CHUNKS * BATCH
    modv = pl.BlockSpec((BATCH, 1, D_MODEL), lambda i: (0, 0, 0))
    vec = pl.BlockSpec((1, D_MODEL), lambda i: (0, 0))

    def const(shape):
        return pl.BlockSpec(shape, lambda i: (0, 0), pipeline_mode=pl.Buffered(1))

    return pl.pallas_call(
        _out_kernel,
        grid=(n // TOK_ROWS,),
        in_specs=[
            pl.BlockSpec((BATCH, TOK_ROWS, D_MODEL), lambda i: (0, i, 0)),
            pl.BlockSpec((chunk_rows, S5_GROUPS * CHUNK_COLS), lambda i: (i, 0)),
            pl.BlockSpec((BATCH, TOK_ROWS, CONV_WIDTH), lambda i: (0, i, 0)),
            modv, modv, modv, modv, vec, vec,
            const((S5_WIDTH, S5_WIDTH)),
            const((D_MODEL, D_MODEL)),
            const((D_MODEL, D_FF)),
            const((D_FF, D_MODEL)),
        ],
        out_specs=pl.BlockSpec((BATCH, TOK_ROWS, D_MODEL), lambda i: (0, i, 0)),
        out_shape=jax.ShapeDtypeStruct((bsz, n, D_MODEL), F32),
        scratch_shapes=[pltpu.VMEM((LANE_BLOCKS, BATCH * PITCH, LANES), F32)],
        compiler_params=pltpu.CompilerParams(
            dimension_semantics=("parallel",), vmem_limit_bytes=VMEM_LIMIT),
        name="out_block",
    )(x, ys, yc, g1, sh2, sc2, g2, n2, fg, wglu, wout, w1, w2)


def kernel(x, c, ctx, c_ctx, ada_w, ada_b, norm1_g, w_in, s5_lam_re, s5_lam_im, s5_log_dt, s5_b_re, s5_b_im, s5_c_re, s5_c_im, s5_d, s5_w_glu, conv_w, conv_b, conv_ln_g, conv_ln_b, w_out, norm2_g, mlp_w1, mlp_w2, final_g):
    bsz = x.shape[0]
    assert bsz == BATCH and bsz + 1 <= MOD_ROWS
    layer = 0

    cc = jnp.concatenate([c, c_ctx[None, :], jnp.zeros((MOD_ROWS - bsz - 1, D_MODEL), F32)], axis=0)
    mod = _mod_table(cc, ada_w[layer], ada_b[layer])
    sh1, sc1, g1, sh2, sc2, g2 = [m.reshape(bsz, 1, D_MODEL) for m in jnp.split(mod[:bsz], 6, axis=-1)]
    csh1 = jnp.broadcast_to(mod[bsz, :D_MODEL].reshape(1, 1, D_MODEL), (bsz, 1, D_MODEL))
    csc1 = jnp.broadcast_to(mod[bsz, D_MODEL:2 * D_MODEL].reshape(1, 1, D_MODEL), (bsz, 1, D_MODEL))

    n1 = norm1_g[layer].reshape(1, D_MODEL)
    w_in_bf = w_in[layer].astype(BF16)
    x_lat, hc = _in_proj(x, sh1, sc1, n1, w_in_bf)
    x_ctx, _ = _in_proj(ctx, csh1, csc1, n1, w_in_bf)

    t_mat, m_mat, r_mat, a_pow = _s5_operators(
        s5_lam_re[layer], s5_lam_im[layer], s5_log_dt[layer], s5_b_re[layer], s5_b_im[layer],
        s5_c_re[layer], s5_c_im[layer], s5_d[layer])
    y_s5 = _s5_scan(x_lat, x_ctx, t_mat, m_mat, r_mat, a_pow, bsz)

    y_conv = _conv_module(hc, conv_w[layer], conv_b[layer], conv_ln_g[layer], conv_ln_b[layer])

    return _out_block(
        x, y_s5, y_conv, g1, sh2, sc2, g2,
        norm2_g[layer].reshape(1, D_MODEL), final_g.reshape(1, D_MODEL),
        s5_w_glu[layer].astype(BF16), w_out[layer].astype(BF16),
        mlp_w1[layer].astype(BF16), mlp_w2[layer].astype(BF16))
```

```python
import functools

import jax
import jax.numpy as jnp
from jax import lax
from jax.experimental import pallas as pl
from jax.experimental.pallas import tpu as pltpu

F32 = jnp.float32
BF16 = jnp.bfloat16

D_MODEL = 1024
S5_WIDTH = 512
S5_GROUP = 16
S5_GROUPS = 32
S5_STATE = 64
CONV_WIDTH = 512
CONV_K = 31
GRID_W = 64
IN_COLS = S5_WIDTH + 2 * CONV_WIDTH
D_FF = 4 * D_MODEL
EPS_RMS = 1e-6
EPS_LN = 1e-5

CHUNK = 16
CHUNK_COLS = CHUNK * S5_GROUP
MOD_ROWS = 16
BATCH = 8
TOK_ROWS = 64
STEP_ROWS = BATCH * TOK_ROWS
STEP_CHUNKS = TOK_ROWS // CHUNK
LANES = 128
SLOTS = LANES // S5_GROUP
LANE_BLOCKS = S5_WIDTH // LANES
PITCH = 72
CONV_ROWS = 32
VMEM_LIMIT = 56 * 1024 * 1024
OUT_VMEM_LIMIT = 60 * 1024 * 1024


def _dot(a, b):
    return jnp.dot(a, b, preferred_element_type=F32)


def _rms(x, g):
    return x * lax.rsqrt(jnp.mean(x * x, axis=-1, keepdims=True) + EPS_RMS) * g


def _mod_kernel(c_ref, w_ref, b_ref, o_ref):
    c = c_ref[...]
    s = c * jax.nn.sigmoid(c)
    w = w_ref[...]
    s_hi = s.astype(BF16)
    s_lo = (s - s_hi.astype(F32)).astype(BF16)
    w_hi = w.astype(BF16)
    w_lo = (w - w_hi.astype(F32)).astype(BF16)
    o_ref[...] = _dot(s_hi, w_hi) + _dot(s_lo, w_hi) + _dot(s_hi, w_lo) + b_ref[...]


def _mod_table(cc, ada_w, ada_b):
    n_out = ada_w.shape[1]
    tn = 1536
    return pl.pallas_call(
        _mod_kernel,
        grid=(n_out // tn,),
        in_specs=[
            pl.BlockSpec((MOD_ROWS, D_MODEL), lambda j: (0, 0)),
            pl.BlockSpec((D_MODEL, tn), lambda j: (0, j)),
            pl.BlockSpec((1, tn), lambda j: (0, j)),
        ],
        out_specs=pl.BlockSpec((MOD_ROWS, tn), lambda j: (0, j)),
        out_shape=jax.ShapeDtypeStruct((MOD_ROWS, n_out), F32),
        compiler_params=pltpu.CompilerParams(vmem_limit_bytes=VMEM_LIMIT),
        name="mod_table",
    )(cc, ada_w, ada_b.reshape(1, n_out))


def _slot_masks():
    slot = lax.broadcasted_iota(jnp.int32, (BATCH, LANES), 1) // S5_GROUP
    return [slot == j for j in range(SLOTS)]


def _to_chunk_layout(z_scr, xs_scr):
    masks = _slot_masks()
    for cl in range(STEP_CHUNKS):
        for vb in range(LANE_BLOCKS):
            for half in range(2):
                rot = []
                for ql in range(SLOTS):
                    piece = z_scr[vb, pl.ds(cl * CHUNK + 8 * half + ql, BATCH, stride=PITCH), :]
                    rot.append(pltpu.roll(piece, ql * S5_GROUP, axis=1) if ql else piece)
                for s in range(SLOTS):
                    out = rot[(-s) % SLOTS]
                    for j in range(1, SLOTS):
                        out = jnp.where(masks[j], rot[(j - s) % SLOTS], out)
                    col = (2 * (SLOTS * vb + s) + half) * LANES
                    xs_scr[cl * BATCH:(cl + 1) * BATCH, col:col + LANES] = out


def _from_chunk_layout(ys, u_scr):
    masks = _slot_masks()
    for cl in range(STEP_CHUNKS):
        for vb in range(LANE_BLOCKS):
            for half in range(2):
                src = []
                for s in range(SLOTS):
                    col = (2 * (SLOTS * vb + s) + half) * LANES
                    src.append(ys[cl * BATCH:(cl + 1) * BATCH, col:col + LANES])
                for ql in range(SLOTS):
                    out = src[(-ql) % SLOTS]
                    for j in range(1, SLOTS):
                        out = jnp.where(masks[j], src[(j - ql) % SLOTS], out)
                    if ql:
                        out = pltpu.roll(out, LANES - ql * S5_GROUP, axis=1)
                    u_scr[vb, pl.ds(cl * CHUNK + 8 * half + ql, BATCH, stride=PITCH), :] = out


def _inproj_kernel(x_ref, sh_ref, sc_ref, g_ref, w_ref, xs_ref, hc_ref, z_scr, xs_scr):
    x = x_ref[...]
    a = _rms(x, g_ref[...]) * (1.0 + sc_ref[...]) + sh_ref[...]
    z = _dot(a.reshape(STEP_ROWS, D_MODEL).astype(BF16), w_ref[...])
    v = z[:, S5_WIDTH:S5_WIDTH + CONV_WIDTH]
    gate = z[:, S5_WIDTH + CONV_WIDTH:]
    hc_ref[...] = (v * jax.nn.sigmoid(gate)).astype(BF16).reshape(BATCH, TOK_ROWS, CONV_WIDTH)
    for vb in range(LANE_BLOCKS):
        for b in range(BATCH):
            z_scr[vb, b * PITCH:b * PITCH + TOK_ROWS, :] = z[b * TOK_ROWS:(b + 1) * TOK_ROWS, vb * LANES:(vb + 1) * LANES]
    _to_chunk_layout(z_scr, xs_scr)
    xs_ref[...] = xs_scr[...].astype(BF16)


def _in_proj(x, sh, sc, g, w_bf):
    bsz, n, _ = x.shape
    assert bsz == BATCH and n % TOK_ROWS == 0
    steps = n // TOK_ROWS
    chunk_rows = STEP_CHUNKS * BATCH
    modv = pl.BlockSpec((BATCH, 1, D_MODEL), lambda i: (0, 0, 0))
    return pl.pallas_call(
        _inproj_kernel,
        grid=(steps,),
        in_specs=[
            pl.BlockSpec((BATCH, TOK_ROWS, D_MODEL), lambda i: (0, i, 0)),
            modv, modv,
            pl.BlockSpec((1, D_MODEL), lambda i: (0, 0)),
            pl.BlockSpec((D_MODEL, IN_COLS), lambda i: (0, 0)),
        ],
        out_specs=[
            pl.BlockSpec((chunk_rows, S5_GROUPS * CHUNK_COLS), lambda i: (i, 0)),
            pl.BlockSpec((BATCH, TOK_ROWS, CONV_WIDTH), lambda i: (0, i, 0)),
        ],
        out_shape=[
            jax.ShapeDtypeStruct((steps * chunk_rows, S5_GROUPS * CHUNK_COLS), BF16),
            jax.ShapeDtypeStruct((bsz, n, CONV_WIDTH), BF16),
        ],
        scratch_shapes=[
            pltpu.VMEM((LANE_BLOCKS, BATCH * PITCH, LANES), F32),
            pltpu.VMEM((chunk_rows, S5_GROUPS * CHUNK_COLS), F32),
        ],
        compiler_params=pltpu.CompilerParams(
            dimension_semantics=("parallel",), vmem_limit_bytes=VMEM_LIMIT),
        name="in_proj",
    )(x, sh, sc, g, w_bf)


def _s5_kernel(n_lat, n_ctx, xl_ref, xc_ref, t_ref, m_ref, r_ref, a_ref, y_ref, e_ref, p_ref, ec_ref):
    half = 2 * S5_STATE
    m = m_ref[0]
    ec_ref[...] = _dot(xc_ref[...], m)
    e_ref[...] = _dot(xl_ref[...], m)
    a_re = jnp.broadcast_to(a_ref[0, 0:1, :], (8, half))
    a_im = jnp.broadcast_to(a_ref[0, 1:2, :], (8, half))
    is_fwd = lax.broadcasted_iota(jnp.int32, (8, half), 1) < S5_STATE

    def advance(src_ref, i, j, s_re, s_im):
        ef = src_ref[pl.ds(pl.multiple_of(i * 8, 8), 8), :]
        eb = src_ref[pl.ds(pl.multiple_of(j * 8, 8), 8), :]
        e_re = jnp.where(is_fwd, ef[:, :half], eb[:, :half])
        e_im = jnp.where(is_fwd, ef[:, half:], eb[:, half:])
        return (a_re * s_re - a_im * s_im + e_re, a_re * s_im + a_im * s_re + e_im)

    def ctx_body(i, s):
        return advance(ec_ref, i, n_ctx - 1 - i, *s)

    zero = jnp.zeros((8, half), F32)
    s = lax.fori_loop(0, n_ctx, ctx_body, (zero, zero))

    def lat_body(i, s):
        s_re, s_im = s
        j = n_lat - 1 - i
        ri = pl.ds(pl.multiple_of(i * 8, 8), 8)
        rj = pl.ds(pl.multiple_of(j * 8, 8), 8)
        p_ref[ri, 0:S5_STATE] = s_re[:, :S5_STATE]
        p_ref[rj, S5_STATE:half] = s_re[:, S5_STATE:]
        p_ref[ri, half:half + S5_STATE] = s_im[:, :S5_STATE]
        p_ref[rj, half + S5_STATE:] = s_im[:, S5_STATE:]
        return advance(e_ref, i, j, s_re, s_im)

    lax.fori_loop(0, n_lat, lat_body, s)
    y = _dot(xl_ref[...], t_ref[0]) + _dot(p_ref[...].astype(BF16), r_ref[0])
    y_ref[...] = y.astype(BF16)


def _s5_scan(x_lat, x_ctx, t_mat, m_mat, r_mat, a_pow, bsz):
    rows = x_lat.shape[0]
    rows_c = x_ctx.shape[0]
    mat_spec = pl.BlockSpec((1, CHUNK_COLS, CHUNK_COLS), lambda g: (g, 0, 0))
    return pl.pallas_call(
        functools.partial(_s5_kernel, rows // bsz, rows_c // bsz),
        grid=(S5_GROUPS,),
        in_specs=[
            pl.BlockSpec((rows, CHUNK_COLS), lambda g: (0, g)),
            pl.BlockSpec((rows_c, CHUNK_COLS), lambda g: (0, g)),
            mat_spec, mat_spec, mat_spec,
            pl.BlockSpec((1, 2, 2 * S5_STATE), lambda g: (g, 0, 0)),
        ],
        out_specs=pl.BlockSpec((rows, CHUNK_COLS), lambda g: (0, g)),
        out_shape=jax.ShapeDtypeStruct((rows, S5_GROUPS * CHUNK_COLS), BF16),
        scratch_shapes=[
            pltpu.VMEM((rows, CHUNK_COLS), F32),
            pltpu.VMEM((rows, CHUNK_COLS), F32),
            pltpu.VMEM((rows_c, CHUNK_COLS), F32),
        ],
        compiler_params=pltpu.CompilerParams(
            dimension_semantics=("parallel",), vmem_limit_bytes=VMEM_LIMIT),
        name="s5_scan",
    )(x_lat, x_ctx, t_mat, m_mat, r_mat, a_pow)


GROUPS_PER_STEP = SLOTS


def _cpow(kk, log_mag, ang):
    mag = jnp.exp(kk * log_mag)
    return mag * jnp.cos(kk * ang), mag * jnp.sin(kk * ang)


def _repeat_rows(a, reps):
    return jnp.concatenate(
        [jnp.broadcast_to(a[r:r + 1, :], (reps, a.shape[1])) for r in range(a.shape[0])], axis=0)


def _tile_rows(a, reps):
    return jnp.concatenate([a] * reps, axis=0)


def _dot3_nt(a, b):
    dn = (((1,), (1,)), ((), ()))
    a_hi = a.astype(BF16)
    a_lo = (a - a_hi.astype(F32)).astype(BF16)
    b_hi = b.astype(BF16)
    b_lo = (b - b_hi.astype(F32)).astype(BF16)

    def nt(u, v):
        return lax.dot_general(u, v, dn, preferred_element_type=F32)

    return nt(a_hi, b_hi) + nt(a_lo, b_hi) + nt(a_hi, b_lo)


def _chunk_row_order(a, s):
    blocks = []
    for half in range(2):
        for slot in range(SLOTS):
            q = 8 * half + (slot - s) % SLOTS
            blocks.append(a[q * S5_GROUP:(q + 1) * S5_GROUP, :])
    return jnp.concatenate(blocks, axis=0)


def _s5_prep_kernel(lre_ref, lim_ref, ldt_ref, btr_ref, bti_ref, ctr_ref, cti_ref, dw_ref,
                    t_ref, m_ref, r_ref, a_ref):
    half_l = 2 * S5_STATE
    n_lag = 2 * CHUNK - 1
    is_fwd16 = lax.broadcasted_iota(jnp.int32, (CHUNK, half_l), 1) < S5_STATE
    is_fwd32 = lax.broadcasted_iota(jnp.int32, (2 * CHUNK, half_l), 1) < S5_STATE
    row16 = lax.broadcasted_iota(jnp.int32, (CHUNK, half_l), 0).astype(F32)
    row32 = lax.broadcasted_iota(jnp.int32, (2 * CHUNK, half_l), 0)
    slot_of_lane = lax.broadcasted_iota(jnp.int32, (CHUNK_COLS, LANES), 1) // S5_GROUP
    eye = (lax.broadcasted_iota(jnp.int32, (CHUNK_COLS, CHUNK_COLS), 0)
           == lax.broadcasted_iota(jnp.int32, (CHUNK_COLS, CHUNK_COLS), 1))

    for s in range(GROUPS_PER_STEP):
        lre, lim = lre_ref[s], lim_ref[s]
        dt = jnp.exp(ldt_ref[s])
        log_mag, ang = lre * dt, lim * dt
        a1_re, a1_im = _cpow(jnp.ones((1, half_l), F32), log_mag, ang)
        den = lre * lre + lim * lim
        num_re = a1_re - 1.0
        f_re = (num_re * lre + a1_im * lim) / den
        f_im = (a1_im * lre - num_re * lim) / den
        bt_re, bt_im = btr_ref[s], bti_ref[s]
        bb_re = f_re * bt_re - f_im * bt_im
        bb_im = f_re * bt_im + f_im * bt_re
        ct_re, ct_im = ctr_ref[s], cti_ref[s]

        lag = jnp.abs(row32 - (CHUNK - 1))
        aj_re, aj_im = _cpow(lag.astype(F32), log_mag, ang)
        aj_re = jnp.where(is_fwd32, jnp.where(row32 <= CHUNK - 1, aj_re, 0.0),
                          jnp.where(row32 >= CHUNK - 1, aj_re, 0.0))[:n_lag]
        aj_im = jnp.where(is_fwd32, jnp.where(row32 <= CHUNK - 1, aj_im, 0.0),
                          jnp.where(row32 >= CHUNK - 1, aj_im, 0.0))[:n_lag]
        aj_re, aj_im = _repeat_rows(aj_re, S5_GROUP), _repeat_rows(aj_im, S5_GROUP)
        bl_re, bl_im = _tile_rows(bb_re, n_lag), _tile_rows(bb_im, n_lag)
        l_re = aj_re * bl_re - aj_im * bl_im
        l_im = aj_re * bl_im + aj_im * bl_re
        cw_re, cw_im = _tile_rows(ct_re, CHUNK), _tile_rows(ct_im, CHUNK)
        kwide = _dot3_nt(jnp.concatenate([l_re, -l_im], axis=1),
                         jnp.concatenate([cw_re, cw_im], axis=1))
        halves = []
        for half in range(2):
            col = kwide[:, half * LANES:(half + 1) * LANES]
            out = None
            for slot in range(SLOTS):
                q = 8 * half + (slot - s) % SLOTS
                start = (CHUNK - 1 - q) * S5_GROUP
                win = col[start:start + CHUNK_COLS, :]
                out = win if out is None else jnp.where(slot_of_lane == slot, win, out)
            halves.append(out)
        t_nat = jnp.concatenate(halves, axis=1)
        t_mat = _chunk_row_order(t_nat, s) + jnp.where(eye, dw_ref[s], 0.0)
        t_ref[s] = t_mat.astype(BF16)

        am_re, am_im = _cpow(jnp.where(is_fwd16, CHUNK - 1.0 - row16, row16), log_mag, ang)
        am_re, am_im = _repeat_rows(am_re, S5_GROUP), _repeat_rows(am_im, S5_GROUP)
        bm_re, bm_im = _tile_rows(bb_re, CHUNK), _tile_rows(bb_im, CHUNK)
        m_nat = jnp.concatenate([am_re * bm_re - am_im * bm_im, am_re * bm_im + am_im * bm_re], axis=1)
        m_ref[s] = _chunk_row_order(m_nat, s).astype(BF16)

        ar_re, ar_im = _cpow(jnp.where(is_fwd16, row16 + 1.0, CHUNK - row16), log_mag, ang)
        ar_re, ar_im = _repeat_rows(ar_re, S5_GROUP), _repeat_rows(ar_im, S5_GROUP)
        rt_nat = jnp.concatenate([ar_re * cw_re - ar_im * cw_im, -(ar_re * cw_im + ar_im * cw_re)], axis=1)
        r_ref[s] = _chunk_row_order(rt_nat, s).T.astype(BF16)

        ap_re, ap_im = _cpow(jnp.full((1, half_l), float(CHUNK), F32), log_mag, ang)
        a_ref[s] = jnp.concatenate([ap_re, ap_im], axis=0)


def _s5_operators(lam_re, lam_im, log_dt, b_re, b_im, c_re, c_im, d_skip):
    half_l = 2 * S5_STATE

    def lanes(v):
        return jnp.transpose(v, (1, 0, 2)).reshape(S5_GROUPS, 1, half_l)

    ldt = lanes(jnp.broadcast_to(log_dt[..., None], lam_re.shape))
    bt_re = jnp.transpose(b_re, (1, 3, 0, 2)).reshape(S5_GROUPS, S5_GROUP, half_l)
    bt_im = jnp.transpose(b_im, (1, 3, 0, 2)).reshape(S5_GROUPS, S5_GROUP, half_l)
    ct_re = jnp.transpose(c_re, (1, 2, 0, 3)).reshape(S5_GROUPS, S5_GROUP, half_l)
    ct_im = jnp.transpose(c_im, (1, 2, 0, 3)).reshape(S5_GROUPS, S5_GROUP, half_l)
    dw = jnp.tile(d_skip, (1, CHUNK)).reshape(S5_GROUPS, 1, CHUNK_COLS)

    gs = GROUPS_PER_STEP
    vec = pl.BlockSpec((gs, 1, half_l), lambda i: (i, 0, 0))
    par = pl.BlockSpec((gs, S5_GROUP, half_l), lambda i: (i, 0, 0))
    mat = pl.BlockSpec((gs, CHUNK_COLS, CHUNK_COLS), lambda i: (i, 0, 0))
    mat_shape = jax.ShapeDtypeStruct((S5_GROUPS, CHUNK_COLS, CHUNK_COLS), BF16)
    return pl.pallas_call(
        _s5_prep_kernel,
        grid=(S5_GROUPS // gs,),
        in_specs=[vec, vec, vec, par, par, par, par,
                  pl.BlockSpec((gs, 1, CHUNK_COLS), lambda i: (i, 0, 0))],
        out_specs=[mat, mat, mat, pl.BlockSpec((gs, 2, half_l), lambda i: (i, 0, 0))],
        out_shape=[mat_shape, mat_shape, mat_shape,
                   jax.ShapeDtypeStruct((S5_GROUPS, 2, half_l), F32)],
        compiler_params=pltpu.CompilerParams(
            dimension_semantics=("parallel",), vmem_limit_bytes=VMEM_LIMIT),
        name="s5_operators",
    )(lanes(lam_re), lanes(lam_im), ldt, bt_re, bt_im, ct_re, ct_im, dw)


CONV_HALF = CONV_K // 2
CONV_LAG = CONV_HALF + 1
RING_SLOTS = 32
CONV_SUB = CONV_ROWS // 8
CONV_BLOCKS = STEP_ROWS // CONV_ROWS


def _conv_block(ring, slots, blk, wb_ref, cb_ref, lg_ref, lb_ref):
    b = blk // (TOK_ROWS // CONV_ROWS)
    t0 = (blk % (TOK_ROWS // CONV_ROWS)) * CONV_ROWS
    if not isinstance(blk, int):
        t0 = pl.multiple_of(t0, CONV_ROWS)
    acc = jnp.zeros((CONV_SUB, 8, CONV_WIDTH), F32)
    for k in range(CONV_K):
        tile = ring[slots[k], b, pl.ds(t0, CONV_ROWS), :].astype(F32)
        acc = acc + wb_ref[k] * tile.reshape(CONV_SUB, 8, CONV_WIDTH)
    acc = acc.reshape(CONV_ROWS, CONV_WIDTH) + cb_ref[...]
    mu = jnp.mean(acc, axis=-1, keepdims=True)
    dev = acc - mu
    var = jnp.mean(dev * dev, axis=-1, keepdims=True)
    y = dev * lax.rsqrt(var + EPS_LN) * lg_ref[...] + lb_ref[...]
    return (y * jax.nn.sigmoid(y)).astype(BF16)


def _out_kernel(n_rows, x_ref, ys_ref, hc_ref, g1_ref, sh2_ref, sc2_ref, g2_ref, n2_ref, fg_ref,
                wb_ref, cb_ref, lg_ref, lb_ref, wglu_ref, wout_ref, w1_ref, w2_ref, o_ref,
                u_scr, ring, yc_scr):
    s = pl.program_id(0)
    zero_row = jnp.zeros((BATCH, TOK_ROWS, CONV_WIDTH), BF16)

    @pl.when(s == 0)
    def _():
        for i in range(RING_SLOTS - CONV_HALF, RING_SLOTS):
            ring[i] = zero_row

    @pl.when(s < n_rows)
    def _():
        ring[s % RING_SLOTS] = hc_ref[...]

    @pl.when(s >= n_rows)
    def _():
        ring[s % RING_SLOTS] = zero_row

    slots = [(s + (RING_SLOTS - 2 * CONV_HALF) + k) % RING_SLOTS for k in range(CONV_K)]
    conv_args = (wb_ref, cb_ref, lg_ref, lb_ref)

    @pl.when(s == CONV_HALF)
    def _():
        def body(blk, carry):
            yc_scr[pl.ds(pl.multiple_of(blk * CONV_ROWS, CONV_ROWS), CONV_ROWS), :] = _conv_block(
                ring, slots, blk, *conv_args)
            return carry
        lax.fori_loop(0, CONV_BLOCKS, body, 0)

    @pl.when(s >= CONV_LAG)
    def _():
        yc = yc_scr[...]
        _from_chunk_layout(ys_ref[...].astype(F32), u_scr)
        y_s5 = jnp.concatenate(
            [jnp.concatenate([u_scr[vb, b * PITCH:b * PITCH + TOK_ROWS, :] for b in range(BATCH)], axis=0)
             for vb in range(LANE_BLOCKS)], axis=1)
        gl = jax.nn.gelu(y_s5)
        s5o = gl * jax.nn.sigmoid(_dot(gl.astype(BF16), wglu_ref[...]))
        mix = _dot(s5o.astype(BF16), wout_ref[:S5_WIDTH, :]) + _dot(yc, wout_ref[S5_WIDTH:, :])
        h1 = x_ref[...] + g1_ref[...] * mix.reshape(BATCH, TOK_ROWS, D_MODEL)
        a2 = _rms(h1, n2_ref[...]) * (1.0 + sc2_ref[...]) + sh2_ref[...]
        a2 = a2.reshape(STEP_ROWS, D_MODEL).astype(BF16)
        ff_tile = 1024
        acc = jnp.zeros((STEP_ROWS, D_MODEL), F32)
        for j in range(D_FF // ff_tile):
            hid = jnp.maximum(_dot(a2, w1_ref[:, j * ff_tile:(j + 1) * ff_tile]), 0.0)
            acc = acc + _dot((hid * hid).astype(BF16), w2_ref[j * ff_tile:(j + 1) * ff_tile, :])
        h2 = h1 + g2_ref[...] * acc.reshape(BATCH, TOK_ROWS, D_MODEL)
        o_ref[...] = _rms(h2, fg_ref[...])
        for blk in range(CONV_BLOCKS):
            yc_scr[blk * CONV_ROWS:(blk + 1) * CONV_ROWS, :] = _conv_block(ring, slots, blk, *conv_args)


def _out_block(x, ys, hc, g1, sh2, sc2, g2, n2, fg, wb, cb, lg, lb, wglu, wout, w1, w2):
    bsz, n, _ = x.shape
    assert bsz == BATCH and n % TOK_ROWS == 0
    n_rows = n // TOK_ROWS
    chunk_rows = STEP_CHUNKS * BATCH
    modv = pl.BlockSpec((BATCH, 1, D_MODEL), lambda s: (0, 0, 0))
    vec = pl.BlockSpec((1, D_MODEL), lambda s: (0, 0))
    cvec = pl.BlockSpec((1, CONV_WIDTH), lambda s: (0, 0))

    def const(shape):
        return pl.BlockSpec(shape, lambda s: (0,) * len(shape), pipeline_mode=pl.Buffered(1))

    def lagged(s):
        return jnp.maximum(s - CONV_LAG, 0)

    return pl.pallas_call(
        functools.partial(_out_kernel, n_rows),
        grid=(n_rows + CONV_LAG,),
        in_specs=[
            pl.BlockSpec((BATCH, TOK_ROWS, D_MODEL), lambda s: (0, lagged(s), 0)),
            pl.BlockSpec((chunk_rows, S5_GROUPS * CHUNK_COLS), lambda s: (lagged(s), 0)),
            pl.BlockSpec((BATCH, TOK_ROWS, CONV_WIDTH), lambda s: (0, jnp.minimum(s, n_rows - 1), 0)),
            modv, modv, modv, modv, vec, vec,
            const((CONV_K, 8, CONV_WIDTH)), cvec, cvec, cvec,
            const((S5_WIDTH, S5_WIDTH)),
            const((D_MODEL, D_MODEL)),
            const((D_MODEL, D_FF)),
            const((D_FF, D_MODEL)),
        ],
        out_specs=pl.BlockSpec((BATCH, TOK_ROWS, D_MODEL), lambda s: (0, lagged(s), 0)),
        out_shape=jax.ShapeDtypeStruct((bsz, n, D_MODEL), F32),
        scratch_shapes=[
            pltpu.VMEM((LANE_BLOCKS, BATCH * PITCH, LANES), F32),
            pltpu.VMEM((RING_SLOTS, BATCH, TOK_ROWS, CONV_WIDTH), BF16),
            pltpu.VMEM((STEP_ROWS, CONV_WIDTH), BF16),
        ],
        compiler_params=pltpu.CompilerParams(
            dimension_semantics=("arbitrary",), vmem_limit_bytes=OUT_VMEM_LIMIT),
        name="out_block",
    )(x, ys, hc, g1, sh2, sc2, g2, n2, fg, wb, cb, lg, lb, wglu, wout, w1, w2)


def kernel(x, c, ctx, c_ctx, ada_w, ada_b, norm1_g, w_in, s5_lam_re, s5_lam_im, s5_log_dt, s5_b_re, s5_b_im, s5_c_re, s5_c_im, s5_d, s5_w_glu, conv_w, conv_b, conv_ln_g, conv_ln_b, w_out, norm2_g, mlp_w1, mlp_w2, final_g):
    bsz = x.shape[0]
    assert bsz == BATCH and bsz + 1 <= MOD_ROWS
    layer = 0

    cc = jnp.concatenate([c, c_ctx[None, :], jnp.zeros((MOD_ROWS - bsz - 1, D_MODEL), F32)], axis=0)
    mod = _mod_table(cc, ada_w[layer], ada_b[layer])
    sh1, sc1, g1, sh2, sc2, g2 = [m.reshape(bsz, 1, D_MODEL) for m in jnp.split(mod[:bsz], 6, axis=-1)]
    csh1 = jnp.broadcast_to(mod[bsz, :D_MODEL].reshape(1, 1, D_MODEL), (bsz, 1, D_MODEL))
    csc1 = jnp.broadcast_to(mod[bsz, D_MODEL:2 * D_MODEL].reshape(1, 1, D_MODEL), (bsz, 1, D_MODEL))

    n1 = norm1_g[layer].reshape(1, D_MODEL)
    w_in_bf = w_in[layer].astype(BF16)
    x_lat, hc = _in_proj(x, sh1, sc1, n1, w_in_bf)
    x_ctx, _ = _in_proj(ctx, csh1, csc1, n1, w_in_bf)

    t_mat, m_mat, r_mat, a_pow = _s5_operators(
        s5_lam_re[layer], s5_lam_im[layer], s5_log_dt[layer], s5_b_re[layer], s5_b_im[layer],
        s5_c_re[layer], s5_c_im[layer], s5_d[layer])
    y_s5 = _s5_scan(x_lat, x_ctx, t_mat, m_mat, r_mat, a_pow, bsz)

    conv_wb = jnp.broadcast_to(conv_w[layer][:, None, :], (CONV_K, 8, CONV_WIDTH))
    return _out_block(
        x, y_s5, hc, g1, sh2, sc2, g2,
        norm2_g[layer].reshape(1, D_MODEL), final_g.reshape(1, D_MODEL),
        conv_wb, conv_b[layer].reshape(1, CONV_WIDTH),
        conv_ln_g[layer].reshape(1, CONV_WIDTH), conv_ln_b[layer].reshape(1, CONV_WIDTH),
        s5_w_glu[layer].astype(BF16), w_out[layer].astype(BF16),
        mlp_w1[layer].astype(BF16), mlp_w2[layer].astype(BF16))
```

```python
import functools

import jax
import jax.numpy as jnp
from jax import lax
from jax.experimental import pallas as pl
from jax.experimental.pallas import tpu as pltpu

F32 = jnp.float32
BF16 = jnp.bfloat16

D_MODEL = 1024
S5_WIDTH = 512
S5_GROUP = 16
S5_GROUPS = 32
S5_STATE = 64
CONV_WIDTH = 512
CONV_K = 31
GRID_W = 64
IN_COLS = S5_WIDTH + 2 * CONV_WIDTH
D_FF = 4 * D_MODEL
EPS_RMS = 1e-6
EPS_LN = 1e-5

CHUNK = 16
CHUNK_COLS = CHUNK * S5_GROUP
MOD_ROWS = 16
BATCH = 8
TOK_ROWS = 64
STEP_ROWS = BATCH * TOK_ROWS
STEP_CHUNKS = TOK_ROWS // CHUNK
LANES = 128
SLOTS = LANES // S5_GROUP
LANE_BLOCKS = S5_WIDTH // LANES
PITCH = 72
CONV_ROWS = 16
VMEM_LIMIT = 56 * 1024 * 1024
OUT_VMEM_LIMIT = 60 * 1024 * 1024


def _dot(a, b):
    return jnp.dot(a, b, preferred_element_type=F32)


def _rms(x, g):
    return x * lax.rsqrt(jnp.mean(x * x, axis=-1, keepdims=True) + EPS_RMS) * g


def _mod_kernel(c_ref, w_ref, b_ref, o_ref):
    c = c_ref[...]
    s = c * jax.nn.sigmoid(c)
    w = w_ref[...]
    s_hi = s.astype(BF16)
    s_lo = (s - s_hi.astype(F32)).astype(BF16)
    w_hi = w.astype(BF16)
    w_lo = (w - w_hi.astype(F32)).astype(BF16)
    o_ref[...] = _dot(s_hi, w_hi) + _dot(s_lo, w_hi) + _dot(s_hi, w_lo) + b_ref[...]


def _mod_table(cc, ada_w, ada_b):
    n_out = ada_w.shape[1]
    tn = 1536
    return pl.pallas_call(
        _mod_kernel,
        grid=(n_out // tn,),
        in_specs=[
            pl.BlockSpec((MOD_ROWS, D_MODEL), lambda j: (0, 0)),
            pl.BlockSpec((D_MODEL, tn), lambda j: (0, j)),
            pl.BlockSpec((1, tn), lambda j: (0, j)),
        ],
        out_specs=pl.BlockSpec((MOD_ROWS, tn), lambda j: (0, j)),
        out_shape=jax.ShapeDtypeStruct((MOD_ROWS, n_out), F32),
        compiler_params=pltpu.CompilerParams(vmem_limit_bytes=VMEM_LIMIT),
        name="mod_table",
    )(cc, ada_w, ada_b.reshape(1, n_out))


def _slot_masks():
    slot = lax.broadcasted_iota(jnp.int32, (BATCH, LANES), 1) // S5_GROUP
    return [slot == j for j in range(SLOTS)]


def _to_chunk_layout(z_scr, xs_scr):
    masks = _slot_masks()
    for cl in range(STEP_CHUNKS):
        for vb in range(LANE_BLOCKS):
            for half in range(2):
                rot = []
                for ql in range(SLOTS):
                    piece = z_scr[vb, pl.ds(cl * CHUNK + 8 * half + ql, BATCH, stride=PITCH), :]
                    rot.append(pltpu.roll(piece, ql * S5_GROUP, axis=1) if ql else piece)
                for s in range(SLOTS):
                    out = rot[(-s) % SLOTS]
                    for j in range(1, SLOTS):
                        out = jnp.where(masks[j], rot[(j - s) % SLOTS], out)
                    col = (2 * (SLOTS * vb + s) + half) * LANES
                    xs_scr[cl * BATCH:(cl + 1) * BATCH, col:col + LANES] = out


def _from_chunk_layout(ys, u_scr):
    masks = _slot_masks()
    for cl in range(STEP_CHUNKS):
        for vb in range(LANE_BLOCKS):
            for half in range(2):
                src = []
                for s in range(SLOTS):
                    col = (2 * (SLOTS * vb + s) + half) * LANES
                    src.append(ys[cl * BATCH:(cl + 1) * BATCH, col:col + LANES])
                for ql in range(SLOTS):
                    out = src[(-ql) % SLOTS]
                    for j in range(1, SLOTS):
                        out = jnp.where(masks[j], src[(j - ql) % SLOTS], out)
                    if ql:
                        out = pltpu.roll(out, LANES - ql * S5_GROUP, axis=1)
                    u_scr[vb, pl.ds(cl * CHUNK + 8 * half + ql, BATCH, stride=PITCH), :] = out


def _inproj_kernel(x_ref, sh_ref, sc_ref, g_ref, w_ref, xs_ref, hc_ref, z_scr, xs_scr):
    x = x_ref[...]
    a = _rms(x, g_ref[...]) * (1.0 + sc_ref[...]) + sh_ref[...]
    z = _dot(a.reshape(STEP_ROWS, D_MODEL).astype(BF16), w_ref[...])
    v = z[:, S5_WIDTH:S5_WIDTH + CONV_WIDTH]
    gate = z[:, S5_WIDTH + CONV_WIDTH:]
    hc_ref[...] = (v * jax.nn.sigmoid(gate)).astype(BF16).reshape(BATCH, TOK_ROWS, CONV_WIDTH)
    for vb in range(LANE_BLOCKS):
        for b in range(BATCH):
            z_scr[vb, b * PITCH:b * PITCH + TOK_ROWS, :] = z[b * TOK_ROWS:(b + 1) * TOK_ROWS, vb * LANES:(vb + 1) * LANES]
    _to_chunk_layout(z_scr, xs_scr)
    xs_ref[...] = xs_scr[...].astype(BF16)


def _in_proj(x, sh, sc, g, w_bf):
    bsz, n, _ = x.shape
    assert bsz == BATCH and n % TOK_ROWS == 0
    steps = n // TOK_ROWS
    chunk_rows = STEP_CHUNKS * BATCH
    modv = pl.BlockSpec((BATCH, 1, D_MODEL), lambda i: (0, 0, 0))
    return pl.pallas_call(
        _inproj_kernel,
        grid=(steps,),
        in_specs=[
            pl.BlockSpec((BATCH, TOK_ROWS, D_MODEL), lambda i: (0, i, 0)),
            modv, modv,
            pl.BlockSpec((1, D_MODEL), lambda i: (0, 0)),
            pl.BlockSpec((D_MODEL, IN_COLS), lambda i: (0, 0)),
        ],
        out_specs=[
            pl.BlockSpec((chunk_rows, S5_GROUPS * CHUNK_COLS), lambda i: (i, 0)),
            pl.BlockSpec((BATCH, TOK_ROWS, CONV_WIDTH), lambda i: (0, i, 0)),
        ],
        out_shape=[
            jax.ShapeDtypeStruct((steps * chunk_rows, S5_GROUPS * CHUNK_COLS), BF16),
            jax.ShapeDtypeStruct((bsz, n, CONV_WIDTH), BF16),
        ],
        scratch_shapes=[
            pltpu.VMEM((LANE_BLOCKS, BATCH * PITCH, LANES), F32),
            pltpu.VMEM((chunk_rows, S5_GROUPS * CHUNK_COLS), F32),
        ],
        compiler_params=pltpu.CompilerParams(
            dimension_semantics=("parallel",), vmem_limit_bytes=VMEM_LIMIT),
        name="in_proj",
    )(x, sh, sc, g, w_bf)


def _s5_kernel(n_lat, n_ctx, xl_ref, xc_ref, t_ref, m_ref, r_ref, a_ref, y_ref, e_ref, p_ref, ec_ref):
    half = 2 * S5_STATE
    m = m_ref[0]
    ec_ref[...] = _dot(xc_ref[...], m)
    e_ref[...] = _dot(xl_ref[...], m)
    a_re = jnp.broadcast_to(a_ref[0, 0:1, :], (8, half))
    a_im = jnp.broadcast_to(a_ref[0, 1:2, :], (8, half))
    is_fwd = lax.broadcasted_iota(jnp.int32, (8, half), 1) < S5_STATE

    def advance(src_ref, i, j, s_re, s_im):
        ef = src_ref[pl.ds(pl.multiple_of(i * 8, 8), 8), :]
        eb = src_ref[pl.ds(pl.multiple_of(j * 8, 8), 8), :]
        e_re = jnp.where(is_fwd, ef[:, :half], eb[:, :half])
        e_im = jnp.where(is_fwd, ef[:, half:], eb[:, half:])
        return (a_re * s_re - a_im * s_im + e_re, a_re * s_im + a_im * s_re + e_im)

    def ctx_body(i, s):
        return advance(ec_ref, i, n_ctx - 1 - i, *s)

    zero = jnp.zeros((8, half), F32)
    s = lax.fori_loop(0, n_ctx, ctx_body, (zero, zero))

    def lat_body(i, s):
        s_re, s_im = s
        j = n_lat - 1 - i
        ri = pl.ds(pl.multiple_of(i * 8, 8), 8)
        rj = pl.ds(pl.multiple_of(j * 8, 8), 8)
        p_ref[ri, 0:S5_STATE] = s_re[:, :S5_STATE]
        p_ref[rj, S5_STATE:half] = s_re[:, S5_STATE:]
        p_ref[ri, half:half + S5_STATE] = s_im[:, :S5_STATE]
        p_ref[rj, half + S5_STATE:] = s_im[:, S5_STATE:]
        return advance(e_ref, i, j, s_re, s_im)

    lax.fori_loop(0, n_lat, lat_body, s)
    y = _dot(xl_ref[...], t_ref[0]) + _dot(p_ref[...].astype(BF16), r_ref[0])
    y_ref[...] = y.astype(BF16)


def _s5_scan(x_lat, x_ctx, t_mat, m_mat, r_mat, a_pow, bsz):
    rows = x_lat.shape[0]
    rows_c = x_ctx.shape[0]
    mat_spec = pl.BlockSpec((1, CHUNK_COLS, CHUNK_COLS), lambda g: (g, 0, 0))
    return pl.pallas_call(
        functools.partial(_s5_kernel, rows // bsz, rows_c // bsz),
        grid=(S5_GROUPS,),
        in_specs=[
            pl.BlockSpec((rows, CHUNK_COLS), lambda g: (0, g)),
            pl.BlockSpec((rows_c, CHUNK_COLS), lambda g: (0, g)),
            mat_spec, mat_spec, mat_spec,
            pl.BlockSpec((1, 2, 2 * S5_STATE), lambda g: (g, 0, 0)),
        ],
        out_specs=pl.BlockSpec((rows, CHUNK_COLS), lambda g: (0, g)),
        out_shape=jax.ShapeDtypeStruct((rows, S5_GROUPS * CHUNK_COLS), BF16),
        scratch_shapes=[
            pltpu.VMEM((rows, CHUNK_COLS), F32),
            pltpu.VMEM((rows, CHUNK_COLS), F32),
            pltpu.VMEM((rows_c, CHUNK_COLS), F32),
        ],
        compiler_params=pltpu.CompilerParams(
            dimension_semantics=("parallel",), vmem_limit_bytes=VMEM_LIMIT),
        name="s5_scan",
    )(x_lat, x_ctx, t_mat, m_mat, r_mat, a_pow)


GROUPS_PER_STEP = SLOTS


def _cpow(kk, log_mag, ang):
    mag = jnp.exp(kk * log_mag)
    return mag * jnp.cos(kk * ang), mag * jnp.sin(kk * ang)


def _repeat_rows(a, reps):
    return jnp.concatenate(
        [jnp.broadcast_to(a[r:r + 1, :], (reps, a.shape[1])) for r in range(a.shape[0])], axis=0)


def _tile_rows(a, reps):
    return jnp.concatenate([a] * reps, axis=0)


def _dot3_nt(a, b):
    dn = (((1,), (1,)), ((), ()))
    a_hi = a.astype(BF16)
    a_lo = (a - a_hi.astype(F32)).astype(BF16)
    b_hi = b.astype(BF16)
    b_lo = (b - b_hi.astype(F32)).astype(BF16)

    def nt(u, v):
        return lax.dot_general(u, v, dn, preferred_element_type=F32)

    return nt(a_hi, b_hi) + nt(a_lo, b_hi) + nt(a_hi, b_lo)


def _chunk_row_order(a, s):
    blocks = []
    for half in range(2):
        for slot in range(SLOTS):
            q = 8 * half + (slot - s) % SLOTS
            blocks.append(a[q * S5_GROUP:(q + 1) * S5_GROUP, :])
    return jnp.concatenate(blocks, axis=0)


def _s5_prep_kernel(lre_ref, lim_ref, ldt_ref, btr_ref, bti_ref, ctr_ref, cti_ref, dw_ref,
                    t_ref, m_ref, r_ref, a_ref):
    half_l = 2 * S5_STATE
    n_lag = 2 * CHUNK - 1
    is_fwd16 = lax.broadcasted_iota(jnp.int32, (CHUNK, half_l), 1) < S5_STATE
    is_fwd32 = lax.broadcasted_iota(jnp.int32, (2 * CHUNK, half_l), 1) < S5_STATE
    row16 = lax.broadcasted_iota(jnp.int32, (CHUNK, half_l), 0).astype(F32)
    row32 = lax.broadcasted_iota(jnp.int32, (2 * CHUNK, half_l), 0)
    slot_of_lane = lax.broadcasted_iota(jnp.int32, (CHUNK_COLS, LANES), 1) // S5_GROUP
    eye = (lax.broadcasted_iota(jnp.int32, (CHUNK_COLS, CHUNK_COLS), 0)
           == lax.broadcasted_iota(jnp.int32, (CHUNK_COLS, CHUNK_COLS), 1))

    for s in range(GROUPS_PER_STEP):
        lre, lim = lre_ref[s], lim_ref[s]
        dt = jnp.exp(ldt_ref[s])
        log_mag, ang = lre * dt, lim * dt
        a1_re, a1_im = _cpow(jnp.ones((1, half_l), F32), log_mag, ang)
        den = lre * lre + lim * lim
        num_re = a1_re - 1.0
        f_re = (num_re * lre + a1_im * lim) / den
        f_im = (a1_im * lre - num_re * lim) / den
        bt_re, bt_im = btr_ref[s], bti_ref[s]
        bb_re = f_re * bt_re - f_im * bt_im
        bb_im = f_re * bt_im + f_im * bt_re
        ct_re, ct_im = ctr_ref[s], cti_ref[s]

        lag = jnp.abs(row32 - (CHUNK - 1))
        aj_re, aj_im = _cpow(lag.astype(F32), log_mag, ang)
        aj_re = jnp.where(is_fwd32, jnp.where(row32 <= CHUNK - 1, aj_re, 0.0),
                          jnp.where(row32 >= CHUNK - 1, aj_re, 0.0))[:n_lag]
        aj_im = jnp.where(is_fwd32, jnp.where(row32 <= CHUNK - 1, aj_im, 0.0),
                          jnp.where(row32 >= CHUNK - 1, aj_im, 0.0))[:n_lag]
        aj_re, aj_im = _repeat_rows(aj_re, S5_GROUP), _repeat_rows(aj_im, S5_GROUP)
        bl_re, bl_im = _tile_rows(bb_re, n_lag), _tile_rows(bb_im, n_lag)
        l_re = aj_re * bl_re - aj_im * bl_im
        l_im = aj_re * bl_im + aj_im * bl_re
        cw_re, cw_im = _tile_rows(ct_re, CHUNK), _tile_rows(ct_im, CHUNK)
        kwide = _dot3_nt(jnp.concatenate([l_re, -l_im], axis=1),
                         jnp.concatenate([cw_re, cw_im], axis=1))
        halves = []
        for half in range(2):
            col = kwide[:, half * LANES:(half + 1) * LANES]
            out = None
            for slot in range(SLOTS):
                q = 8 * half + (slot - s) % SLOTS
                start = (CHUNK - 1 - q) * S5_GROUP
                win = col[start:start + CHUNK_COLS, :]
                out = win if out is None else jnp.where(slot_of_lane == slot, win, out)
            halves.append(out)
        t_nat = jnp.concatenate(halves, axis=1)
        t_mat = _chunk_row_order(t_nat, s) + jnp.where(eye, dw_ref[s], 0.0)
        t_ref[s] = t_mat.astype(BF16)

        am_re, am_im = _cpow(jnp.where(is_fwd16, CHUNK - 1.0 - row16, row16), log_mag, ang)
        am_re, am_im = _repeat_rows(am_re, S5_GROUP), _repeat_rows(am_im, S5_GROUP)
        bm_re, bm_im = _tile_rows(bb_re, CHUNK), _tile_rows(bb_im, CHUNK)
        m_nat = jnp.concatenate([am_re * bm_re - am_im * bm_im, am_re * bm_im + am_im * bm_re], axis=1)
        m_ref[s] = _chunk_row_order(m_nat, s).astype(BF16)

        ar_re, ar_im = _cpow(jnp.where(is_fwd16, row16 + 1.0, CHUNK - row16), log_mag, ang)
        ar_re, ar_im = _repeat_rows(ar_re, S5_GROUP), _repeat_rows(ar_im, S5_GROUP)
        rt_nat = jnp.concatenate([ar_re * cw_re - ar_im * cw_im, -(ar_re * cw_im + ar_im * cw_re)], axis=1)
        r_ref[s] = _chunk_row_order(rt_nat, s).T.astype(BF16)

        ap_re, ap_im = _cpow(jnp.full((1, half_l), float(CHUNK), F32), log_mag, ang)
        a_ref[s] = jnp.concatenate([ap_re, ap_im], axis=0)


def _s5_operators(lam_re, lam_im, log_dt, b_re, b_im, c_re, c_im, d_skip):
    half_l = 2 * S5_STATE

    def lanes(v):
        return jnp.transpose(v, (1, 0, 2)).reshape(S5_GROUPS, 1, half_l)

    ldt = lanes(jnp.broadcast_to(log_dt[..., None], lam_re.shape))
    bt_re = jnp.transpose(b_re, (1, 3, 0, 2)).reshape(S5_GROUPS, S5_GROUP, half_l)
    bt_im = jnp.transpose(b_im, (1, 3, 0, 2)).reshape(S5_GROUPS, S5_GROUP, half_l)
    ct_re = jnp.transpose(c_re, (1, 2, 0, 3)).reshape(S5_GROUPS, S5_GROUP, half_l)
    ct_im = jnp.transpose(c_im, (1, 2, 0, 3)).reshape(S5_GROUPS, S5_GROUP, half_l)
    dw = jnp.tile(d_skip, (1, CHUNK)).reshape(S5_GROUPS, 1, CHUNK_COLS)

    gs = GROUPS_PER_STEP
    vec = pl.BlockSpec((gs, 1, half_l), lambda i: (i, 0, 0))
    par = pl.BlockSpec((gs, S5_GROUP, half_l), lambda i: (i, 0, 0))
    mat = pl.BlockSpec((gs, CHUNK_COLS, CHUNK_COLS), lambda i: (i, 0, 0))
    mat_shape = jax.ShapeDtypeStruct((S5_GROUPS, CHUNK_COLS, CHUNK_COLS), BF16)
    return pl.pallas_call(
        _s5_prep_kernel,
        grid=(S5_GROUPS // gs,),
        in_specs=[vec, vec, vec, par, par, par, par,
                  pl.BlockSpec((gs, 1, CHUNK_COLS), lambda i: (i, 0, 0))],
        out_specs=[mat, mat, mat, pl.BlockSpec((gs, 2, half_l), lambda i: (i, 0, 0))],
        out_shape=[mat_shape, mat_shape, mat_shape,
                   jax.ShapeDtypeStruct((S5_GROUPS, 2, half_l), F32)],
        compiler_params=pltpu.CompilerParams(
            dimension_semantics=("parallel",), vmem_limit_bytes=VMEM_LIMIT),
        name="s5_operators",
    )(lanes(lam_re), lanes(lam_im), ldt, bt_re, bt_im, ct_re, ct_im, dw)


CONV_HALF = CONV_K // 2
CONV_LAG = CONV_HALF + 1
RING_SLOTS = 32
CONV_SUB = CONV_ROWS // 8
CONV_BLOCKS = STEP_ROWS // CONV_ROWS


BF16_ROWS = 16
TAP_GROUP = 4


def _aligned(x, m):
    return x if isinstance(x, int) else pl.multiple_of(x, m)


def _conv_block(ring, slots, blk, wb_ref, cb_ref, lg_ref, lb_ref):
    b = blk // (TOK_ROWS // CONV_ROWS)
    t0 = _aligned((blk % (TOK_ROWS // CONV_ROWS)) * CONV_ROWS, CONV_ROWS)
    packed = (CONV_ROWS // BF16_ROWS, BF16_ROWS, CONV_WIDTH)
    acc = jnp.zeros(packed, F32)
    for k0 in range(0, CONV_K, TAP_GROUP):
        group = None
        for k in range(k0, min(k0 + TAP_GROUP, CONV_K)):
            prod = wb_ref[k] * ring[slots[k], b, pl.ds(t0, CONV_ROWS), :].reshape(packed)
            group = prod if group is None else group + prod
        acc = acc + group.astype(F32)
    acc = acc.reshape(CONV_ROWS, CONV_WIDTH) + cb_ref[...]
    mu = jnp.mean(acc, axis=-1, keepdims=True)
    dev = acc - mu
    var = jnp.mean(dev * dev, axis=-1, keepdims=True)
    y = dev * lax.rsqrt(var + EPS_LN) * lg_ref[...] + lb_ref[...]
    return (y * jax.nn.sigmoid(y)).astype(BF16)


def _out_kernel(n_rows, x_ref, ys_ref, hc_ref, g1_ref, sh2_ref, sc2_ref, g2_ref, n2_ref, fg_ref,
                wb_ref, cb_ref, lg_ref, lb_ref, wglu_ref, wout_ref, w1_ref, w2_ref, o_ref,
                u_scr, ring, yc_scr):
    s = pl.program_id(0)
    zero_row = jnp.zeros((BATCH, TOK_ROWS, CONV_WIDTH), BF16)

    @pl.when(s == 0)
    def _():
        for i in range(RING_SLOTS - CONV_HALF, RING_SLOTS):
            ring[i] = zero_row

    @pl.when(s < n_rows)
    def _():
        ring[s % RING_SLOTS] = hc_ref[...]

    @pl.when(s >= n_rows)
    def _():
        ring[s % RING_SLOTS] = zero_row

    slots = [(s + (RING_SLOTS - 2 * CONV_HALF) + k) % RING_SLOTS for k in range(CONV_K)]
    conv_args = (wb_ref, cb_ref, lg_ref, lb_ref)

    @pl.when(s == CONV_HALF)
    def _():
        def body(blk, carry):
            rows = pl.ds(_aligned(blk * CONV_ROWS, CONV_ROWS), CONV_ROWS)
            yc_scr[rows, :] = _conv_block(ring, slots, blk, *conv_args)
            return carry
        lax.fori_loop(0, CONV_BLOCKS, body, 0)

    @pl.when(s >= CONV_LAG)
    def _():
        yc = yc_scr[...]
        _from_chunk_layout(ys_ref[...].astype(F32), u_scr)
        y_s5 = jnp.concatenate(
            [jnp.concatenate([u_scr[vb, b * PITCH:b * PITCH + TOK_ROWS, :] for b in range(BATCH)], axis=0)
             for vb in range(LANE_BLOCKS)], axis=1)
        gl = jax.nn.gelu(y_s5)
        s5o = gl * jax.nn.sigmoid(_dot(gl.astype(BF16), wglu_ref[...]))
        mix = _dot(s5o.astype(BF16), wout_ref[:S5_WIDTH, :]) + _dot(yc, wout_ref[S5_WIDTH:, :])
        h1 = x_ref[...] + g1_ref[...] * mix.reshape(BATCH, TOK_ROWS, D_MODEL)
        a2 = _rms(h1, n2_ref[...]) * (1.0 + sc2_ref[...]) + sh2_ref[...]
        a2 = a2.reshape(STEP_ROWS, D_MODEL).astype(BF16)
        ff_tile = 1024
        acc = jnp.zeros((STEP_ROWS, D_MODEL), F32)
        for j in range(D_FF // ff_tile):
            hid = jnp.maximum(_dot(a2, w1_ref[:, j * ff_tile:(j + 1) * ff_tile]), 0.0)
            acc = acc + _dot((hid * hid).astype(BF16), w2_ref[j * ff_tile:(j + 1) * ff_tile, :])
        h2 = h1 + g2_ref[...] * acc.reshape(BATCH, TOK_ROWS, D_MODEL)
        o_ref[...] = _rms(h2, fg_ref[...])
        for blk in range(CONV_BLOCKS):
            yc_scr[blk * CONV_ROWS:(blk + 1) * CONV_ROWS, :] = _conv_block(ring, slots, blk, *conv_args)


def _out_block(x, ys, hc, g1, sh2, sc2, g2, n2, fg, wb, cb, lg, lb, wglu, wout, w1, w2):
    bsz, n, _ = x.shape
    assert bsz == BATCH and n % TOK_ROWS == 0
    n_rows = n // TOK_ROWS
    chunk_rows = STEP_CHUNKS * BATCH
    modv = pl.BlockSpec((BATCH, 1, D_MODEL), lambda s: (0, 0, 0))
    vec = pl.BlockSpec((1, D_MODEL), lambda s: (0, 0))
    cvec = pl.BlockSpec((1, CONV_WIDTH), lambda s: (0, 0))

    def const(shape):
        return pl.BlockSpec(shape, lambda s: (0,) * len(shape), pipeline_mode=pl.Buffered(1))

    def lagged(s):
        return jnp.maximum(s - CONV_LAG, 0)

    return pl.pallas_call(
        functools.partial(_out_kernel, n_rows),
        grid=(n_rows + CONV_LAG,),
        in_specs=[
            pl.BlockSpec((BATCH, TOK_ROWS, D_MODEL), lambda s: (0, lagged(s), 0)),
            pl.BlockSpec((chunk_rows, S5_GROUPS * CHUNK_COLS), lambda s: (lagged(s), 0)),
            pl.BlockSpec((BATCH, TOK_ROWS, CONV_WIDTH), lambda s: (0, jnp.minimum(s, n_rows - 1), 0)),
            modv, modv, modv, modv, vec, vec,
            const((CONV_K, BF16_ROWS, CONV_WIDTH)), cvec, cvec, cvec,
            const((S5_WIDTH, S5_WIDTH)),
            const((D_MODEL, D_MODEL)),
            const((D_MODEL, D_FF)),
            const((D_FF, D_MODEL)),
        ],
        out_specs=pl.BlockSpec((BATCH, TOK_ROWS, D_MODEL), lambda s: (0, lagged(s), 0)),
        out_shape=jax.ShapeDtypeStruct((bsz, n, D_MODEL), F32),
        scratch_shapes=[
            pltpu.VMEM((LANE_BLOCKS, BATCH * PITCH, LANES), F32),
            pltpu.VMEM((RING_SLOTS, BATCH, TOK_ROWS, CONV_WIDTH), BF16),
            pltpu.VMEM((STEP_ROWS, CONV_WIDTH), BF16),
        ],
        compiler_params=pltpu.CompilerParams(
            dimension_semantics=("arbitrary",), vmem_limit_bytes=OUT_VMEM_LIMIT),
        name="out_block",
    )(x, ys, hc, g1, sh2, sc2, g2, n2, fg, wb, cb, lg, lb, wglu, wout, w1, w2)


def kernel(x, c, ctx, c_ctx, ada_w, ada_b, norm1_g, w_in, s5_lam_re, s5_lam_im, s5_log_dt, s5_b_re, s5_b_im, s5_c_re, s5_c_im, s5_d, s5_w_glu, conv_w, conv_b, conv_ln_g, conv_ln_b, w_out, norm2_g, mlp_w1, mlp_w2, final_g):
    bsz = x.shape[0]
    assert bsz == BATCH and bsz + 1 <= MOD_ROWS
    layer = 0

    cc = jnp.concatenate([c, c_ctx[None, :], jnp.zeros((MOD_ROWS - bsz - 1, D_MODEL), F32)], axis=0)
    mod = _mod_table(cc, ada_w[layer], ada_b[layer])
    sh1, sc1, g1, sh2, sc2, g2 = [m.reshape(bsz, 1, D_MODEL) for m in jnp.split(mod[:bsz], 6, axis=-1)]
    csh1 = jnp.broadcast_to(mod[bsz, :D_MODEL].reshape(1, 1, D_MODEL), (bsz, 1, D_MODEL))
    csc1 = jnp.broadcast_to(mod[bsz, D_MODEL:2 * D_MODEL].reshape(1, 1, D_MODEL), (bsz, 1, D_MODEL))

    n1 = norm1_g[layer].reshape(1, D_MODEL)
    w_in_bf = w_in[layer].astype(BF16)
    x_lat, hc = _in_proj(x, sh1, sc1, n1, w_in_bf)
    x_ctx, _ = _in_proj(ctx, csh1, csc1, n1, w_in_bf)

    t_mat, m_mat, r_mat, a_pow = _s5_operators(
        s5_lam_re[layer], s5_lam_im[layer], s5_log_dt[layer], s5_b_re[layer], s5_b_im[layer],
        s5_c_re[layer], s5_c_im[layer], s5_d[layer])
    y_s5 = _s5_scan(x_lat, x_ctx, t_mat, m_mat, r_mat, a_pow, bsz)

    conv_wb = jnp.broadcast_to(conv_w[layer].astype(BF16)[:, None, :], (CONV_K, BF16_ROWS, CONV_WIDTH))
    return _out_block(
        x, y_s5, hc, g1, sh2, sc2, g2,
        norm2_g[layer].reshape(1, D_MODEL), final_g.reshape(1, D_MODEL),
        conv_wb, conv_b[layer].reshape(1, CONV_WIDTH),
        conv_ln_g[layer].reshape(1, CONV_WIDTH), conv_ln_b[layer].reshape(1, CONV_WIDTH),
        s5_w_glu[layer].astype(BF16), w_out[layer].astype(BF16),
        mlp_w1[layer].astype(BF16), mlp_w2[layer].astype(BF16))
```

```python
import functools

import jax
import jax.numpy as jnp
from jax import lax
from jax.experimental import pallas as pl
from jax.experimental.pallas import tpu as pltpu

F32 = jnp.float32
BF16 = jnp.bfloat16

D_MODEL = 1024
S5_WIDTH = 512
S5_GROUP = 16
S5_GROUPS = 32
S5_STATE = 64
CONV_WIDTH = 512
CONV_K = 31
GRID_W = 64
IN_COLS = S5_WIDTH + 2 * CONV_WIDTH
D_FF = 4 * D_MODEL
EPS_RMS = 1e-6
EPS_LN = 1e-5

CHUNK = 16
CHUNK_COLS = CHUNK * S5_GROUP
MOD_ROWS = 16
BATCH = 8
TOK_ROWS = 64
STEP_ROWS = BATCH * TOK_ROWS
STEP_CHUNKS = TOK_ROWS // CHUNK
LANES = 128
SLOTS = LANES // S5_GROUP
LANE_BLOCKS = S5_WIDTH // LANES
PITCH = 72
CONV_ROWS = 16
VMEM_LIMIT = 56 * 1024 * 1024
OUT_VMEM_LIMIT = 60 * 1024 * 1024


def _dot(a, b):
    return jnp.dot(a, b, preferred_element_type=F32)


def _rms(x, g):
    return x * lax.rsqrt(jnp.mean(x * x, axis=-1, keepdims=True) + EPS_RMS) * g


def _mod_kernel(c_ref, w_ref, b_ref, o_ref):
    c = c_ref[...]
    s = c * jax.nn.sigmoid(c)
    w = w_ref[...]
    s_hi = s.astype(BF16)
    s_lo = (s - s_hi.astype(F32)).astype(BF16)
    w_hi = w.astype(BF16)
    w_lo = (w - w_hi.astype(F32)).astype(BF16)
    o_ref[...] = _dot(s_hi, w_hi) + _dot(s_lo, w_hi) + _dot(s_hi, w_lo) + b_ref[...]


def _mod_table(cc, ada_w, ada_b):
    n_out = ada_w.shape[1]
    tn = 1536
    return pl.pallas_call(
        _mod_kernel,
        grid=(n_out // tn,),
        in_specs=[
            pl.BlockSpec((MOD_ROWS, D_MODEL), lambda j: (0, 0)),
            pl.BlockSpec((D_MODEL, tn), lambda j: (0, j)),
            pl.BlockSpec((1, tn), lambda j: (0, j)),
        ],
        out_specs=pl.BlockSpec((MOD_ROWS, tn), lambda j: (0, j)),
        out_shape=jax.ShapeDtypeStruct((MOD_ROWS, n_out), F32),
        compiler_params=pltpu.CompilerParams(vmem_limit_bytes=VMEM_LIMIT),
        name="mod_table",
    )(cc, ada_w, ada_b.reshape(1, n_out))


def _slot_masks():
    slot = lax.broadcasted_iota(jnp.int32, (BATCH, LANES), 1) // S5_GROUP
    return [slot == j for j in range(SLOTS)]


def _to_chunk_layout(z_scr, xs_scr):
    masks = _slot_masks()
    for cl in range(STEP_CHUNKS):
        for vb in range(LANE_BLOCKS):
            for half in range(2):
                rot = []
                for ql in range(SLOTS):
                    piece = z_scr[vb, pl.ds(cl * CHUNK + 8 * half + ql, BATCH, stride=PITCH), :]
                    rot.append(pltpu.roll(piece, ql * S5_GROUP, axis=1) if ql else piece)
                for s in range(SLOTS):
                    out = rot[(-s) % SLOTS]
                    for j in range(1, SLOTS):
                        out = jnp.where(masks[j], rot[(j - s) % SLOTS], out)
                    col = (2 * (SLOTS * vb + s) + half) * LANES
                    xs_scr[cl * BATCH:(cl + 1) * BATCH, col:col + LANES] = out


def _from_chunk_layout(ys, u_scr):
    masks = _slot_masks()
    for cl in range(STEP_CHUNKS):
        for vb in range(LANE_BLOCKS):
            for half in range(2):
                src = []
                for s in range(SLOTS):
                    col = (2 * (SLOTS * vb + s) + half) * LANES
                    src.append(ys[cl * BATCH:(cl + 1) * BATCH, col:col + LANES])
                for ql in range(SLOTS):
                    out = src[(-ql) % SLOTS]
                    for j in range(1, SLOTS):
                        out = jnp.where(masks[j], src[(j - ql) % SLOTS], out)
                    if ql:
                        out = pltpu.roll(out, LANES - ql * S5_GROUP, axis=1)
                    u_scr[vb, pl.ds(cl * CHUNK + 8 * half + ql, BATCH, stride=PITCH), :] = out


def _inproj_kernel(x_ref, sh_ref, sc_ref, g_ref, w_ref, xs_ref, hc_ref, z_scr, xs_scr):
    x = x_ref[...]
    a = _rms(x, g_ref[...]) * (1.0 + sc_ref[...]) + sh_ref[...]
    z = _dot(a.reshape(STEP_ROWS, D_MODEL).astype(BF16), w_ref[...])
    v = z[:, S5_WIDTH:S5_WIDTH + CONV_WIDTH]
    gate = z[:, S5_WIDTH + CONV_WIDTH:]
    hc_ref[...] = (v * jax.nn.sigmoid(gate)).astype(BF16).reshape(BATCH, TOK_ROWS, CONV_WIDTH)
    for vb in range(LANE_BLOCKS):
        for b in range(BATCH):
            z_scr[vb, b * PITCH:b * PITCH + TOK_ROWS, :] = z[b * TOK_ROWS:(b + 1) * TOK_ROWS, vb * LANES:(vb + 1) * LANES]
    _to_chunk_layout(z_scr, xs_scr)
    for g in range(S5_GROUPS):
        xs_ref[g] = xs_scr[:, g * CHUNK_COLS:(g + 1) * CHUNK_COLS].astype(BF16)


def _in_proj(x, sh, sc, g, w_bf):
    bsz, n, _ = x.shape
    assert bsz == BATCH and n % TOK_ROWS == 0
    steps = n // TOK_ROWS
    chunk_rows = STEP_CHUNKS * BATCH
    modv = pl.BlockSpec((BATCH, 1, D_MODEL), lambda i: (0, 0, 0))
    return pl.pallas_call(
        _inproj_kernel,
        grid=(steps,),
        in_specs=[
            pl.BlockSpec((BATCH, TOK_ROWS, D_MODEL), lambda i: (0, i, 0)),
            modv, modv,
            pl.BlockSpec((1, D_MODEL), lambda i: (0, 0)),
            pl.BlockSpec((D_MODEL, IN_COLS), lambda i: (0, 0)),
        ],
        out_specs=[
            pl.BlockSpec((S5_GROUPS, chunk_rows, CHUNK_COLS), lambda i: (0, i, 0)),
            pl.BlockSpec((BATCH, TOK_ROWS, CONV_WIDTH), lambda i: (0, i, 0)),
        ],
        out_shape=[
            jax.ShapeDtypeStruct((S5_GROUPS, steps * chunk_rows, CHUNK_COLS), BF16),
            jax.ShapeDtypeStruct((bsz, n, CONV_WIDTH), BF16),
        ],
        scratch_shapes=[
            pltpu.VMEM((LANE_BLOCKS, BATCH * PITCH, LANES), F32),
            pltpu.VMEM((chunk_rows, S5_GROUPS * CHUNK_COLS), F32),
        ],
        compiler_params=pltpu.CompilerParams(
            dimension_semantics=("parallel",), vmem_limit_bytes=VMEM_LIMIT),
        name="in_proj",
    )(x, sh, sc, g, w_bf)


SCAN_GROUPS = 4


def _s5_kernel(n_lat, n_ctx, xl_ref, xc_ref, t_ref, m_ref, r_ref, a_ref, y_ref, e_ref, p_ref, ec_ref):
    half = 2 * S5_STATE
    groups = range(SCAN_GROUPS)
    for g in groups:
        ec_ref[g] = _dot(xc_ref[g], m_ref[g])
        e_ref[g] = _dot(xl_ref[g], m_ref[g])
    a_re = [jnp.broadcast_to(a_ref[g, 0:1, :], (8, half)) for g in groups]
    a_im = [jnp.broadcast_to(a_ref[g, 1:2, :], (8, half)) for g in groups]
    is_fwd = lax.broadcasted_iota(jnp.int32, (8, half), 1) < S5_STATE

    def advance(src_ref, g, i, j, s_re, s_im):
        ef = src_ref[g, pl.ds(pl.multiple_of(i * 8, 8), 8), :]
        eb = src_ref[g, pl.ds(pl.multiple_of(j * 8, 8), 8), :]
        e_re = jnp.where(is_fwd, ef[:, :half], eb[:, :half])
        e_im = jnp.where(is_fwd, ef[:, half:], eb[:, half:])
        return (a_re[g] * s_re - a_im[g] * s_im + e_re, a_re[g] * s_im + a_im[g] * s_re + e_im)

    def ctx_body(i, s):
        return tuple(advance(ec_ref, g, i, n_ctx - 1 - i, *s[g]) for g in groups)

    zero = jnp.zeros((8, half), F32)
    s = lax.fori_loop(0, n_ctx, ctx_body, tuple((zero, zero) for _ in groups))

    def lat_body(i, s):
        j = n_lat - 1 - i
        ri = pl.ds(pl.multiple_of(i * 8, 8), 8)
        rj = pl.ds(pl.multiple_of(j * 8, 8), 8)
        out = []
        for g in groups:
            s_re, s_im = s[g]
            p_ref[g, ri, 0:S5_STATE] = s_re[:, :S5_STATE]
            p_ref[g, rj, S5_STATE:half] = s_re[:, S5_STATE:]
            p_ref[g, ri, half:half + S5_STATE] = s_im[:, :S5_STATE]
            p_ref[g, rj, half + S5_STATE:] = s_im[:, S5_STATE:]
            out.append(advance(e_ref, g, i, j, s_re, s_im))
        return tuple(out)

    lax.fori_loop(0, n_lat, lat_body, s)
    for g in groups:
        y = _dot(xl_ref[g], t_ref[g]) + _dot(p_ref[g].astype(BF16), r_ref[g])
        y_ref[g] = y.astype(BF16)


def _s5_scan(x_lat, x_ctx, t_mat, m_mat, r_mat, a_pow, bsz):
    rows = x_lat.shape[1]
    rows_c = x_ctx.shape[1]
    gs = SCAN_GROUPS

    def per_group(*shape):
        return pl.BlockSpec((gs,) + shape, lambda i: (i,) + (0,) * len(shape))

    mat_spec = per_group(CHUNK_COLS, CHUNK_COLS)
    return pl.pallas_call(
        functools.partial(_s5_kernel, rows // bsz, rows_c // bsz),
        grid=(S5_GROUPS // gs,),
        in_specs=[
            per_group(rows, CHUNK_COLS), per_group(rows_c, CHUNK_COLS),
            mat_spec, mat_spec, mat_spec,
            per_group(2, 2 * S5_STATE),
        ],
        out_specs=per_group(rows, CHUNK_COLS),
        out_shape=jax.ShapeDtypeStruct((S5_GROUPS, rows, CHUNK_COLS), BF16),
        scratch_shapes=[
            pltpu.VMEM((gs, rows, CHUNK_COLS), F32),
            pltpu.VMEM((gs, rows, CHUNK_COLS), F32),
            pltpu.VMEM((gs, rows_c, CHUNK_COLS), F32),
        ],
        compiler_params=pltpu.CompilerParams(
            dimension_semantics=("parallel",), vmem_limit_bytes=VMEM_LIMIT),
        name="s5_scan",
    )(x_lat, x_ctx, t_mat, m_mat, r_mat, a_pow)


GROUPS_PER_STEP = SLOTS


def _cpow(kk, log_mag, ang):
    mag = jnp.exp(kk * log_mag)
    return mag * jnp.cos(kk * ang), mag * jnp.sin(kk * ang)


def _repeat_rows(a, reps):
    return jnp.concatenate(
        [jnp.broadcast_to(a[r:r + 1, :], (reps, a.shape[1])) for r in range(a.shape[0])], axis=0)


def _tile_rows(a, reps):
    return jnp.concatenate([a] * reps, axis=0)


def _dot3_nt(a, b):
    dn = (((1,), (1,)), ((), ()))
    a_hi = a.astype(BF16)
    a_lo = (a - a_hi.astype(F32)).astype(BF16)
    b_hi = b.astype(BF16)
    b_lo = (b - b_hi.astype(F32)).astype(BF16)

    def nt(u, v):
        return lax.dot_general(u, v, dn, preferred_element_type=F32)

    return nt(a_hi, b_hi) + nt(a_lo, b_hi) + nt(a_hi, b_lo)


def _chunk_row_order(a, s):
    blocks = []
    for half in range(2):
        for slot in range(SLOTS):
            q = 8 * half + (slot - s) % SLOTS
            blocks.append(a[q * S5_GROUP:(q + 1) * S5_GROUP, :])
    return jnp.concatenate(blocks, axis=0)


def _s5_prep_kernel(lre_ref, lim_ref, ldt_ref, btr_ref, bti_ref, ctr_ref, cti_ref, dw_ref,
                    t_ref, m_ref, r_ref, a_ref):
    half_l = 2 * S5_STATE
    n_lag = 2 * CHUNK - 1
    is_fwd16 = lax.broadcasted_iota(jnp.int32, (CHUNK, half_l), 1) < S5_STATE
    is_fwd32 = lax.broadcasted_iota(jnp.int32, (2 * CHUNK, half_l), 1) < S5_STATE
    row16 = lax.broadcasted_iota(jnp.int32, (CHUNK, half_l), 0).astype(F32)
    row32 = lax.broadcasted_iota(jnp.int32, (2 * CHUNK, half_l), 0)
    slot_of_lane = lax.broadcasted_iota(jnp.int32, (CHUNK_COLS, LANES), 1) // S5_GROUP
    eye = (lax.broadcasted_iota(jnp.int32, (CHUNK_COLS, CHUNK_COLS), 0)
           == lax.broadcasted_iota(jnp.int32, (CHUNK_COLS, CHUNK_COLS), 1))

    for s in range(GROUPS_PER_STEP):
        lre, lim = lre_ref[s], lim_ref[s]
        dt = jnp.exp(ldt_ref[s])
        log_mag, ang = lre * dt, lim * dt
        a1_re, a1_im = _cpow(jnp.ones((1, half_l), F32), log_mag, ang)
        den = lre * lre + lim * lim
        num_re = a1_re - 1.0
        f_re = (num_re * lre + a1_im * lim) / den
        f_im = (a1_im * lre - num_re * lim) / den
        bt_re, bt_im = btr_ref[s], bti_ref[s]
        bb_re = f_re * bt_re - f_im * bt_im
        bb_im = f_re * bt_im + f_im * bt_re
        ct_re, ct_im = ctr_ref[s], cti_ref[s]

        lag = jnp.abs(row32 - (CHUNK - 1))
        aj_re, aj_im = _cpow(lag.astype(F32), log_mag, ang)
        aj_re = jnp.where(is_fwd32, jnp.where(row32 <= CHUNK - 1, aj_re, 0.0),
                          jnp.where(row32 >= CHUNK - 1, aj_re, 0.0))[:n_lag]
        aj_im = jnp.where(is_fwd32, jnp.where(row32 <= CHUNK - 1, aj_im, 0.0),
                          jnp.where(row32 >= CHUNK - 1, aj_im, 0.0))[:n_lag]
        aj_re, aj_im = _repeat_rows(aj_re, S5_GROUP), _repeat_rows(aj_im, S5_GROUP)
        bl_re, bl_im = _tile_rows(bb_re, n_lag), _tile_rows(bb_im, n_lag)
        l_re = aj_re * bl_re - aj_im * bl_im
        l_im = aj_re * bl_im + aj_im * bl_re
        cw_re, cw_im = _tile_rows(ct_re, CHUNK), _tile_rows(ct_im, CHUNK)
        kwide = _dot3_nt(jnp.concatenate([l_re, -l_im], axis=1),
                         jnp.concatenate([cw_re, cw_im], axis=1))
        halves = []
        for half in range(2):
            col = kwide[:, half * LANES:(half + 1) * LANES]
            out = None
            for slot in range(SLOTS):
                q = 8 * half + (slot - s) % SLOTS
                start = (CHUNK - 1 - q) * S5_GROUP
                win = col[start:start + CHUNK_COLS, :]
                out = win if out is None else jnp.where(slot_of_lane == slot, win, out)
            halves.append(out)
        t_nat = jnp.concatenate(halves, axis=1)
        t_mat = _chunk_row_order(t_nat, s) + jnp.where(eye, dw_ref[s], 0.0)
        t_ref[s] = t_mat.astype(BF16)

        am_re, am_im = _cpow(jnp.where(is_fwd16, CHUNK - 1.0 - row16, row16), log_mag, ang)
        am_re, am_im = _repeat_rows(am_re, S5_GROUP), _repeat_rows(am_im, S5_GROUP)
        bm_re, bm_im = _tile_rows(bb_re, CHUNK), _tile_rows(bb_im, CHUNK)
        m_nat = jnp.concatenate([am_re * bm_re - am_im * bm_im, am_re * bm_im + am_im * bm_re], axis=1)
        m_ref[s] = _chunk_row_order(m_nat, s).astype(BF16)

        ar_re, ar_im = _cpow(jnp.where(is_fwd16, row16 + 1.0, CHUNK - row16), log_mag, ang)
        ar_re, ar_im = _repeat_rows(ar_re, S5_GROUP), _repeat_rows(ar_im, S5_GROUP)
        rt_nat = jnp.concatenate([ar_re * cw_re - ar_im * cw_im, -(ar_re * cw_im + ar_im * cw_re)], axis=1)
        r_ref[s] = _chunk_row_order(rt_nat, s).T.astype(BF16)

        ap_re, ap_im = _cpow(jnp.full((1, half_l), float(CHUNK), F32), log_mag, ang)
        a_ref[s] = jnp.concatenate([ap_re, ap_im], axis=0)


def _s5_operators(lam_re, lam_im, log_dt, b_re, b_im, c_re, c_im, d_skip):
    half_l = 2 * S5_STATE

    def lanes(v):
        return jnp.transpose(v, (1, 0, 2)).reshape(S5_GROUPS, 1, half_l)

    ldt = lanes(jnp.broadcast_to(log_dt[..., None], lam_re.shape))
    bt_re = jnp.transpose(b_re, (1, 3, 0, 2)).reshape(S5_GROUPS, S5_GROUP, half_l)
    bt_im = jnp.transpose(b_im, (1, 3, 0, 2)).reshape(S5_GROUPS, S5_GROUP, half_l)
    ct_re = jnp.transpose(c_re, (1, 2, 0, 3)).reshape(S5_GROUPS, S5_GROUP, half_l)
    ct_im = jnp.transpose(c_im, (1, 2, 0, 3)).reshape(S5_GROUPS, S5_GROUP, half_l)
    dw = jnp.tile(d_skip, (1, CHUNK)).reshape(S5_GROUPS, 1, CHUNK_COLS)

    gs = GROUPS_PER_STEP
    vec = pl.BlockSpec((gs, 1, half_l), lambda i: (i, 0, 0))
    par = pl.BlockSpec((gs, S5_GROUP, half_l), lambda i: (i, 0, 0))
    mat = pl.BlockSpec((gs, CHUNK_COLS, CHUNK_COLS), lambda i: (i, 0, 0))
    mat_shape = jax.ShapeDtypeStruct((S5_GROUPS, CHUNK_COLS, CHUNK_COLS), BF16)
    return pl.pallas_call(
        _s5_prep_kernel,
        grid=(S5_GROUPS // gs,),
        in_specs=[vec, vec, vec, par, par, par, par,
                  pl.BlockSpec((gs, 1, CHUNK_COLS), lambda i: (i, 0, 0))],
        out_specs=[mat, mat, mat, pl.BlockSpec((gs, 2, half_l), lambda i: (i, 0, 0))],
        out_shape=[mat_shape, mat_shape, mat_shape,
                   jax.ShapeDtypeStruct((S5_GROUPS, 2, half_l), F32)],
        compiler_params=pltpu.CompilerParams(
            dimension_semantics=("parallel",), vmem_limit_bytes=VMEM_LIMIT),
        name="s5_operators",
    )(lanes(lam_re), lanes(lam_im), ldt, bt_re, bt_im, ct_re, ct_im, dw)


CONV_HALF = CONV_K // 2
CONV_LAG = CONV_HALF + 1
RING_SLOTS = 32
CONV_SUB = CONV_ROWS // 8
CONV_BLOCKS = STEP_ROWS // CONV_ROWS


BF16_ROWS = 16
TAP_GROUP = 4


def _aligned(x, m):
    return x if isinstance(x, int) else pl.multiple_of(x, m)


def _conv_block(ring, slots, blk, wb_ref, cb_ref, lg_ref, lb_ref):
    b = blk // (TOK_ROWS // CONV_ROWS)
    t0 = _aligned((blk % (TOK_ROWS // CONV_ROWS)) * CONV_ROWS, CONV_ROWS)
    packed = (CONV_ROWS // BF16_ROWS, BF16_ROWS, CONV_WIDTH)
    acc = jnp.zeros(packed, F32)
    for k0 in range(0, CONV_K, TAP_GROUP):
        group = None
        for k in range(k0, min(k0 + TAP_GROUP, CONV_K)):
            prod = wb_ref[k] * ring[slots[k], b, pl.ds(t0, CONV_ROWS), :].reshape(packed)
            group = prod if group is None else group + prod
        acc = acc + group.astype(F32)
    acc = acc.reshape(CONV_ROWS, CONV_WIDTH) + cb_ref[...]
    mu = jnp.mean(acc, axis=-1, keepdims=True)
    dev = acc - mu
    var = jnp.mean(dev * dev, axis=-1, keepdims=True)
    y = dev * lax.rsqrt(var + EPS_LN) * lg_ref[...] + lb_ref[...]
    return (y * jax.nn.sigmoid(y)).astype(BF16)


def _out_kernel(n_rows, x_ref, ys_ref, hc_ref, g1_ref, sh2_ref, sc2_ref, g2_ref, n2_ref, fg_ref,
                wb_ref, cb_ref, lg_ref, lb_ref, wglu_ref, wout_ref, w1_ref, w2_ref, o_ref,
                u_scr, ring, yc_scr):
    s = pl.program_id(0)
    zero_row = jnp.zeros((BATCH, TOK_ROWS, CONV_WIDTH), BF16)

    @pl.when(s == 0)
    def _():
        for i in range(RING_SLOTS - CONV_HALF, RING_SLOTS):
            ring[i] = zero_row

    @pl.when(s < n_rows)
    def _():
        ring[s % RING_SLOTS] = hc_ref[...]

    @pl.when(s >= n_rows)
    def _():
        ring[s % RING_SLOTS] = zero_row

    slots = [(s + (RING_SLOTS - 2 * CONV_HALF) + k) % RING_SLOTS for k in range(CONV_K)]
    conv_args = (wb_ref, cb_ref, lg_ref, lb_ref)

    @pl.when(s == CONV_HALF)
    def _():
        def body(blk, carry):
            rows = pl.ds(_aligned(blk * CONV_ROWS, CONV_ROWS), CONV_ROWS)
            yc_scr[rows, :] = _conv_block(ring, slots, blk, *conv_args)
            return carry
        lax.fori_loop(0, CONV_BLOCKS, body, 0)

    @pl.when(s >= CONV_LAG)
    def _():
        yc = yc_scr[...]
        ys = jnp.concatenate([ys_ref[g].astype(F32) for g in range(S5_GROUPS)], axis=1)
        _from_chunk_layout(ys, u_scr)
        y_s5 = jnp.concatenate(
            [jnp.concatenate([u_scr[vb, b * PITCH:b * PITCH + TOK_ROWS, :] for b in range(BATCH)], axis=0)
             for vb in range(LANE_BLOCKS)], axis=1)
        gl = jax.nn.gelu(y_s5)
        s5o = gl * jax.nn.sigmoid(_dot(gl.astype(BF16), wglu_ref[...]))
        mix = _dot(s5o.astype(BF16), wout_ref[:S5_WIDTH, :]) + _dot(yc, wout_ref[S5_WIDTH:, :])
        h1 = x_ref[...] + g1_ref[...] * mix.reshape(BATCH, TOK_ROWS, D_MODEL)
        a2 = _rms(h1, n2_ref[...]) * (1.0 + sc2_ref[...]) + sh2_ref[...]
        a2 = a2.reshape(STEP_ROWS, D_MODEL).astype(BF16)
        ff_tile = 1024
        acc = jnp.zeros((STEP_ROWS, D_MODEL), F32)
        for j in range(D_FF // ff_tile):
            hid = jnp.maximum(_dot(a2, w1_ref[:, j * ff_tile:(j + 1) * ff_tile]), 0.0)
            acc = acc + _dot((hid * hid).astype(BF16), w2_ref[j * ff_tile:(j + 1) * ff_tile, :])
        h2 = h1 + g2_ref[...] * acc.reshape(BATCH, TOK_ROWS, D_MODEL)
        o_ref[...] = _rms(h2, fg_ref[...])
        for blk in range(CONV_BLOCKS):
            yc_scr[blk * CONV_ROWS:(blk + 1) * CONV_ROWS, :] = _conv_block(ring, slots, blk, *conv_args)


def _out_block(x, ys, hc, g1, sh2, sc2, g2, n2, fg, wb, cb, lg, lb, wglu, wout, w1, w2):
    bsz, n, _ = x.shape
    assert bsz == BATCH and n % TOK_ROWS == 0
    n_rows = n // TOK_ROWS
    chunk_rows = STEP_CHUNKS * BATCH
    modv = pl.BlockSpec((BATCH, 1, D_MODEL), lambda s: (0, 0, 0))
    vec = pl.BlockSpec((1, D_MODEL), lambda s: (0, 0))
    cvec = pl.BlockSpec((1, CONV_WIDTH), lambda s: (0, 0))

    def const(shape):
        return pl.BlockSpec(shape, lambda s: (0,) * len(shape), pipeline_mode=pl.Buffered(1))

    def lagged(s):
        return jnp.maximum(s - CONV_LAG, 0)

    return pl.pallas_call(
        functools.partial(_out_kernel, n_rows),
        grid=(n_rows + CONV_LAG,),
        in_specs=[
            pl.BlockSpec((BATCH, TOK_ROWS, D_MODEL), lambda s: (0, lagged(s), 0)),
            pl.BlockSpec((S5_GROUPS, chunk_rows, CHUNK_COLS), lambda s: (0, lagged(s), 0)),
            pl.BlockSpec((BATCH, TOK_ROWS, CONV_WIDTH), lambda s: (0, jnp.minimum(s, n_rows - 1), 0)),
            modv, modv, modv, modv, vec, vec,
            const((CONV_K, BF16_ROWS, CONV_WIDTH)), cvec, cvec, cvec,
            const((S5_WIDTH, S5_WIDTH)),
            const((D_MODEL, D_MODEL)),
            const((D_MODEL, D_FF)),
            const((D_FF, D_MODEL)),
        ],
        out_specs=pl.BlockSpec((BATCH, TOK_ROWS, D_MODEL), lambda s: (0, lagged(s), 0)),
        out_shape=jax.ShapeDtypeStruct((bsz, n, D_MODEL), F32),
        scratch_shapes=[
            pltpu.VMEM((LANE_BLOCKS, BATCH * PITCH, LANES), F32),
            pltpu.VMEM((RING_SLOTS, BATCH, TOK_ROWS, CONV_WIDTH), BF16),
            pltpu.VMEM((STEP_ROWS, CONV_WIDTH), BF16),
        ],
        compiler_params=pltpu.CompilerParams(
            dimension_semantics=("arbitrary",), vmem_limit_bytes=OUT_VMEM_LIMIT),
        name="out_block",
    )(x, ys, hc, g1, sh2, sc2, g2, n2, fg, wb, cb, lg, lb, wglu, wout, w1, w2)


def kernel(x, c, ctx, c_ctx, ada_w, ada_b, norm1_g, w_in, s5_lam_re, s5_lam_im, s5_log_dt, s5_b_re, s5_b_im, s5_c_re, s5_c_im, s5_d, s5_w_glu, conv_w, conv_b, conv_ln_g, conv_ln_b, w_out, norm2_g, mlp_w1, mlp_w2, final_g):
    bsz = x.shape[0]
    assert bsz == BATCH and bsz + 1 <= MOD_ROWS
    layer = 0

    cc = jnp.concatenate([c, c_ctx[None, :], jnp.zeros((MOD_ROWS - bsz - 1, D_MODEL), F32)], axis=0)
    mod = _mod_table(cc, ada_w[layer], ada_b[layer])
    sh1, sc1, g1, sh2, sc2, g2 = [m.reshape(bsz, 1, D_MODEL) for m in jnp.split(mod[:bsz], 6, axis=-1)]
    csh1 = jnp.broadcast_to(mod[bsz, :D_MODEL].reshape(1, 1, D_MODEL), (bsz, 1, D_MODEL))
    csc1 = jnp.broadcast_to(mod[bsz, D_MODEL:2 * D_MODEL].reshape(1, 1, D_MODEL), (bsz, 1, D_MODEL))

    n1 = norm1_g[layer].reshape(1, D_MODEL)
    w_in_bf = w_in[layer].astype(BF16)
    x_lat, hc = _in_proj(x, sh1, sc1, n1, w_in_bf)
    x_ctx, _ = _in_proj(ctx, csh1, csc1, n1, w_in_bf)

    t_mat, m_mat, r_mat, a_pow = _s5_operators(
        s5_lam_re[layer], s5_lam_im[layer], s5_log_dt[layer], s5_b_re[layer], s5_b_im[layer],
        s5_c_re[layer], s5_c_im[layer], s5_d[layer])
    y_s5 = _s5_scan(x_lat, x_ctx, t_mat, m_mat, r_mat, a_pow, bsz)

    conv_wb = jnp.broadcast_to(conv_w[layer].astype(BF16)[:, None, :], (CONV_K, BF16_ROWS, CONV_WIDTH))
    return _out_block(
        x, y_s5, hc, g1, sh2, sc2, g2,
        norm2_g[layer].reshape(1, D_MODEL), final_g.reshape(1, D_MODEL),
        conv_wb, conv_b[layer].reshape(1, CONV_WIDTH),
        conv_ln_g[layer].reshape(1, CONV_WIDTH), conv_ln_b[layer].reshape(1, CONV_WIDTH),
        s5_w_glu[layer].astype(BF16), w_out[layer].astype(BF16),
        mlp_w1[layer].astype(BF16), mlp_w2[layer].astype(BF16))
```

```python
import functools

import jax
import jax.numpy as jnp
from jax import lax
from jax.experimental import pallas as pl
from jax.experimental.pallas import tpu as pltpu

F32 = jnp.float32
BF16 = jnp.bfloat16

D_MODEL = 1024
S5_WIDTH = 512
S5_GROUP = 16
S5_GROUPS = 32
S5_STATE = 64
CONV_WIDTH = 512
CONV_K = 31
GRID_W = 64
IN_COLS = S5_WIDTH + 2 * CONV_WIDTH
D_FF = 4 * D_MODEL
EPS_RMS = 1e-6
EPS_LN = 1e-5

CHUNK = 16
CHUNK_COLS = CHUNK * S5_GROUP
MOD_ROWS = 16
BATCH = 8
TOK_ROWS = 64
STEP_ROWS = BATCH * TOK_ROWS
STEP_CHUNKS = TOK_ROWS // CHUNK
LANES = 128
SLOTS = LANES // S5_GROUP
LANE_BLOCKS = S5_WIDTH // LANES
PITCH = 72
CONV_ROWS = 16
VMEM_LIMIT = 56 * 1024 * 1024
OUT_VMEM_LIMIT = 60 * 1024 * 1024


def _dot(a, b):
    return jnp.dot(a, b, preferred_element_type=F32)


def _rms(x, g):
    return x * lax.rsqrt(jnp.mean(x * x, axis=-1, keepdims=True) + EPS_RMS) * g


def _mod_kernel(c_ref, w_ref, b_ref, o_ref):
    c = c_ref[...]
    s = c * jax.nn.sigmoid(c)
    w = w_ref[...]
    s_hi = s.astype(BF16)
    s_lo = (s - s_hi.astype(F32)).astype(BF16)
    w_hi = w.astype(BF16)
    w_lo = (w - w_hi.astype(F32)).astype(BF16)
    o_ref[...] = _dot(s_hi, w_hi) + _dot(s_lo, w_hi) + _dot(s_hi, w_lo) + b_ref[...]


def _mod_table(cc, ada_w, ada_b):
    n_out = ada_w.shape[1]
    tn = 1536
    return pl.pallas_call(
        _mod_kernel,
        grid=(n_out // tn,),
        in_specs=[
            pl.BlockSpec((MOD_ROWS, D_MODEL), lambda j: (0, 0)),
            pl.BlockSpec((D_MODEL, tn), lambda j: (0, j)),
            pl.BlockSpec((1, tn), lambda j: (0, j)),
        ],
        out_specs=pl.BlockSpec((MOD_ROWS, tn), lambda j: (0, j)),
        out_shape=jax.ShapeDtypeStruct((MOD_ROWS, n_out), F32),
        compiler_params=pltpu.CompilerParams(vmem_limit_bytes=VMEM_LIMIT),
        name="mod_table",
    )(cc, ada_w, ada_b.reshape(1, n_out))


def _slot_skew(vregs):
    slot = lax.broadcasted_iota(jnp.int32, (BATCH, LANES), 1) // S5_GROUP
    cur = list(vregs)
    for b in range(3):
        bit_set = (slot & (1 << b)) != 0
        cur = [jnp.where(bit_set, cur[(t + (1 << b)) % SLOTS], cur[t]) for t in range(SLOTS)]
    return cur


def _to_chunk_layout(z_scr, xs_scr):
    for cl in range(STEP_CHUNKS):
        for vb in range(LANE_BLOCKS):
            for half in range(2):
                rot = []
                for ql in range(SLOTS):
                    piece = z_scr[vb, pl.ds(cl * CHUNK + 8 * half + ql, BATCH, stride=PITCH), :]
                    rot.append(pltpu.roll(piece, ql * S5_GROUP, axis=1) if ql else piece)
                skew = _slot_skew(rot)
                for s in range(SLOTS):
                    col = (2 * (SLOTS * vb + s) + half) * LANES
                    xs_scr[cl * BATCH:(cl + 1) * BATCH, col:col + LANES] = skew[(-s) % SLOTS]


def _from_chunk_layout(ys, u_scr):
    for cl in range(STEP_CHUNKS):
        for vb in range(LANE_BLOCKS):
            for half in range(2):
                src = []
                for s in range(SLOTS):
                    col = (2 * (SLOTS * vb + s) + half) * LANES
                    src.append(ys[cl * BATCH:(cl + 1) * BATCH, col:col + LANES])
                skew = _slot_skew(src)
                for ql in range(SLOTS):
                    out = skew[(-ql) % SLOTS]
                    if ql:
                        out = pltpu.roll(out, LANES - ql * S5_GROUP, axis=1)
                    u_scr[vb, pl.ds(cl * CHUNK + 8 * half + ql, BATCH, stride=PITCH), :] = out


def _inproj_kernel(with_conv, x_ref, sh_ref, sc_ref, g_ref, w_ref, xs_ref, *rest):
    hc_ref = rest[0] if with_conv else None
    z_scr, xs_scr, w_bf = rest[-3:]

    @pl.when(pl.program_id(0) == 0)
    def _():
        w_bf[...] = w_ref[...].astype(BF16)

    x = x_ref[...]
    a = _rms(x, g_ref[...]) * (1.0 + sc_ref[...]) + sh_ref[...]
    z = _dot(a.reshape(STEP_ROWS, D_MODEL).astype(BF16), w_bf[...])
    if with_conv:
        v = z[:, S5_WIDTH:S5_WIDTH + CONV_WIDTH]
        gate = z[:, S5_WIDTH + CONV_WIDTH:]
        hc_ref[...] = (v * jax.nn.sigmoid(gate)).astype(BF16).reshape(BATCH, TOK_ROWS, CONV_WIDTH)
    for vb in range(LANE_BLOCKS):
        for b in range(BATCH):
            z_scr[vb, b * PITCH:b * PITCH + TOK_ROWS, :] = z[b * TOK_ROWS:(b + 1) * TOK_ROWS, vb * LANES:(vb + 1) * LANES]
    _to_chunk_layout(z_scr, xs_scr)
    for g in range(S5_GROUPS):
        xs_ref[g] = xs_scr[:, g * CHUNK_COLS:(g + 1) * CHUNK_COLS].astype(BF16)


def _in_proj(x, sh, sc, g, w_in, with_conv):
    bsz, n, _ = x.shape
    assert bsz == BATCH and n % TOK_ROWS == 0
    steps = n // TOK_ROWS
    chunk_rows = STEP_CHUNKS * BATCH
    n_cols = IN_COLS if with_conv else S5_WIDTH
    modv = pl.BlockSpec((BATCH, 1, D_MODEL), lambda i: (0, 0, 0))
    out_specs = [pl.BlockSpec((S5_GROUPS, chunk_rows, CHUNK_COLS), lambda i: (0, i, 0))]
    out_shape = [jax.ShapeDtypeStruct((S5_GROUPS, steps * chunk_rows, CHUNK_COLS), BF16)]
    if with_conv:
        out_specs.append(pl.BlockSpec((BATCH, TOK_ROWS, CONV_WIDTH), lambda i: (0, i, 0)))
        out_shape.append(jax.ShapeDtypeStruct((bsz, n, CONV_WIDTH), BF16))
    return pl.pallas_call(
        functools.partial(_inproj_kernel, with_conv),
        grid=(steps,),
        in_specs=[
            pl.BlockSpec((BATCH, TOK_ROWS, D_MODEL), lambda i: (0, i, 0)),
            modv, modv,
            pl.BlockSpec((1, D_MODEL), lambda i: (0, 0)),
            pl.BlockSpec((D_MODEL, n_cols), lambda i: (0, 0), pipeline_mode=pl.Buffered(1)),
        ],
        out_specs=out_specs,
        out_shape=out_shape,
        scratch_shapes=[
            pltpu.VMEM((LANE_BLOCKS, BATCH * PITCH, LANES), F32),
            pltpu.VMEM((chunk_rows, S5_GROUPS * CHUNK_COLS), F32),
            pltpu.VMEM((D_MODEL, n_cols), BF16),
        ],
        compiler_params=pltpu.CompilerParams(
            dimension_semantics=("arbitrary",), vmem_limit_bytes=VMEM_LIMIT),
        name="in_proj" if with_conv else "in_proj_ctx",
    )(x, sh, sc, g, w_in)


SCAN_GROUPS = 4


def _s5_kernel(n_lat, n_ctx, xl_ref, xc_ref, t_ref, m_ref, r_ref, a_ref, y_ref, e_ref, p_ref, ec_ref):
    half = 2 * S5_STATE
    groups = range(SCAN_GROUPS)
    for g in groups:
        ec_ref[g] = _dot(xc_ref[g], m_ref[g])
        e_ref[g] = _dot(xl_ref[g], m_ref[g])
    a_re = [jnp.broadcast_to(a_ref[g, 0:1, :], (8, half)) for g in groups]
    a_im = [jnp.broadcast_to(a_ref[g, 1:2, :], (8, half)) for g in groups]
    is_fwd = lax.broadcasted_iota(jnp.int32, (8, half), 1) < S5_STATE

    def advance(src_ref, g, i, j, s_re, s_im):
        ef = src_ref[g, pl.ds(pl.multiple_of(i * 8, 8), 8), :]
        eb = src_ref[g, pl.ds(pl.multiple_of(j * 8, 8), 8), :]
        e_re = jnp.where(is_fwd, ef[:, :half], eb[:, :half])
        e_im = jnp.where(is_fwd, ef[:, half:], eb[:, half:])
        return (a_re[g] * s_re - a_im[g] * s_im + e_re, a_re[g] * s_im + a_im[g] * s_re + e_im)

    def ctx_body(i, s):
        return tuple(advance(ec_ref, g, i, n_ctx - 1 - i, *s[g]) for g in groups)

    zero = jnp.zeros((8, half), F32)
    s = lax.fori_loop(0, n_ctx, ctx_body, tuple((zero, zero) for _ in groups))

    def lat_body(i, s):
        j = n_lat - 1 - i
        ri = pl.ds(pl.multiple_of(i * 8, 8), 8)
        rj = pl.ds(pl.multiple_of(j * 8, 8), 8)
        out = []
        for g in groups:
            s_re, s_im = s[g]
            p_ref[g, ri, 0:S5_STATE] = s_re[:, :S5_STATE]
            p_ref[g, rj, S5_STATE:half] = s_re[:, S5_STATE:]
            p_ref[g, ri, half:half + S5_STATE] = s_im[:, :S5_STATE]
            p_ref[g, rj, half + S5_STATE:] = s_im[:, S5_STATE:]
            out.append(advance(e_ref, g, i, j, s_re, s_im))
        return tuple(out)

    lax.fori_loop(0, n_lat, lat_body, s)
    for g in groups:
        y = _dot(xl_ref[g], t_ref[g]) + _dot(p_ref[g].astype(BF16), r_ref[g])
        y_ref[g] = y.astype(BF16)


def _s5_scan(x_lat, x_ctx, t_mat, m_mat, r_mat, a_pow, bsz):
    rows = x_lat.shape[1]
    rows_c = x_ctx.shape[1]
    gs = SCAN_GROUPS

    def per_group(*shape):
        return pl.BlockSpec((gs,) + shape, lambda i: (i,) + (0,) * len(shape))

    mat_spec = per_group(CHUNK_COLS, CHUNK_COLS)
    return pl.pallas_call(
        functools.partial(_s5_kernel, rows // bsz, rows_c // bsz),
        grid=(S5_GROUPS // gs,),
        in_specs=[
            per_group(rows, CHUNK_COLS), per_group(rows_c, CHUNK_COLS),
            mat_spec, mat_spec, mat_spec,
            per_group(2, 2 * S5_STATE),
        ],
        out_specs=per_group(rows, CHUNK_COLS),
        out_shape=jax.ShapeDtypeStruct((S5_GROUPS, rows, CHUNK_COLS), BF16),
        scratch_shapes=[
            pltpu.VMEM((gs, rows, CHUNK_COLS), F32),
            pltpu.VMEM((gs, rows, CHUNK_COLS), F32),
            pltpu.VMEM((gs, rows_c, CHUNK_COLS), F32),
        ],
        compiler_params=pltpu.CompilerParams(
            dimension_semantics=("parallel",), vmem_limit_bytes=VMEM_LIMIT),
        name="s5_scan",
    )(x_lat, x_ctx, t_mat, m_mat, r_mat, a_pow)


GROUPS_PER_STEP = SLOTS


def _cpow(kk, log_mag, ang):
    mag = jnp.exp(kk * log_mag)
    return mag * jnp.cos(kk * ang), mag * jnp.sin(kk * ang)


def _repeat_rows(a, reps):
    return jnp.concatenate(
        [jnp.broadcast_to(a[r:r + 1, :], (reps, a.shape[1])) for r in range(a.shape[0])], axis=0)


def _tile_rows(a, reps):
    return jnp.concatenate([a] * reps, axis=0)


def _dot3_nt(a, b):
    dn = (((1,), (1,)), ((), ()))
    a_hi = a.astype(BF16)
    a_lo = (a - a_hi.astype(F32)).astype(BF16)
    b_hi = b.astype(BF16)
    b_lo = (b - b_hi.astype(F32)).astype(BF16)

    def nt(u, v):
        return lax.dot_general(u, v, dn, preferred_element_type=F32)

    return nt(a_hi, b_hi) + nt(a_lo, b_hi) + nt(a_hi, b_lo)


def _chunk_row_order(a, s):
    blocks = []
    for half in range(2):
        for slot in range(SLOTS):
            q = 8 * half + (slot - s) % SLOTS
            blocks.append(a[q * S5_GROUP:(q + 1) * S5_GROUP, :])
    return jnp.concatenate(blocks, axis=0)


def _s5_prep_kernel(lre_ref, lim_ref, ldt_ref, btr_ref, bti_ref, ctr_ref, cti_ref, dw_ref,
                    t_ref, m_ref, r_ref, a_ref):
    half_l = 2 * S5_STATE
    n_lag = 2 * CHUNK - 1
    is_fwd16 = lax.broadcasted_iota(jnp.int32, (CHUNK, half_l), 1) < S5_STATE
    is_fwd32 = lax.broadcasted_iota(jnp.int32, (2 * CHUNK, half_l), 1) < S5_STATE
    row16 = lax.broadcasted_iota(jnp.int32, (CHUNK, half_l), 0).astype(F32)
    row32 = lax.broadcasted_iota(jnp.int32, (2 * CHUNK, half_l), 0)
    slot_of_lane = lax.broadcasted_iota(jnp.int32, (CHUNK_COLS, LANES), 1) // S5_GROUP
    eye = (lax.broadcasted_iota(jnp.int32, (CHUNK_COLS, CHUNK_COLS), 0)
           == lax.broadcasted_iota(jnp.int32, (CHUNK_COLS, CHUNK_COLS), 1))

    for s in range(GROUPS_PER_STEP):
        lre, lim = lre_ref[s], lim_ref[s]
        dt = jnp.exp(ldt_ref[s])
        log_mag, ang = lre * dt, lim * dt
        a1_re, a1_im = _cpow(jnp.ones((1, half_l), F32), log_mag, ang)
        den = lre * lre + lim * lim
        num_re = a1_re - 1.0
        f_re = (num_re * lre + a1_im * lim) / den
        f_im = (a1_im * lre - num_re * lim) / den
        bt_re, bt_im = btr_ref[s], bti_ref[s]
        bb_re = f_re * bt_re - f_im * bt_im
        bb_im = f_re * bt_im + f_im * bt_re
        ct_re, ct_im = ctr_ref[s], cti_ref[s]

        lag = jnp.abs(row32 - (CHUNK - 1))
        aj_re, aj_im = _cpow(lag.astype(F32), log_mag, ang)
        aj_re = jnp.where(is_fwd32, jnp.where(row32 <= CHUNK - 1, aj_re, 0.0),
                          jnp.where(row32 >= CHUNK - 1, aj_re, 0.0))[:n_lag]
        aj_im = jnp.where(is_fwd32, jnp.where(row32 <= CHUNK - 1, aj_im, 0.0),
                          jnp.where(row32 >= CHUNK - 1, aj_im, 0.0))[:n_lag]
        aj_re, aj_im = _repeat_rows(aj_re, S5_GROUP), _repeat_rows(aj_im, S5_GROUP)
        bl_re, bl_im = _tile_rows(bb_re, n_lag), _tile_rows(bb_im, n_lag)
        l_re = aj_re * bl_re - aj_im * bl_im
        l_im = aj_re * bl_im + aj_im * bl_re
        cw_re, cw_im = _tile_rows(ct_re, CHUNK), _tile_rows(ct_im, CHUNK)
        kwide = _dot3_nt(jnp.concatenate([l_re, -l_im], axis=1),
                         jnp.concatenate([cw_re, cw_im], axis=1))
        halves = []
        for half in range(2):
            col = kwide[:, half * LANES:(half + 1) * LANES]
            out = None
            for slot in range(SLOTS):
                q = 8 * half + (slot - s) % SLOTS
                start = (CHUNK - 1 - q) * S5_GROUP
                win = col[start:start + CHUNK_COLS, :]
                out = win if out is None else jnp.where(slot_of_lane == slot, win, out)
            halves.append(out)
        t_nat = jnp.concatenate(halves, axis=1)
        t_mat = _chunk_row_order(t_nat, s) + jnp.where(eye, dw_ref[s], 0.0)
        t_ref[s] = t_mat.astype(BF16)

        am_re, am_im = _cpow(jnp.where(is_fwd16, CHUNK - 1.0 - row16, row16), log_mag, ang)
        am_re, am_im = _repeat_rows(am_re, S5_GROUP), _repeat_rows(am_im, S5_GROUP)
        bm_re, bm_im = _tile_rows(bb_re, CHUNK), _tile_rows(bb_im, CHUNK)
        m_nat = jnp.concatenate([am_re * bm_re - am_im * bm_im, am_re * bm_im + am_im * bm_re], axis=1)
        m_ref[s] = _chunk_row_order(m_nat, s).astype(BF16)

        ar_re, ar_im = _cpow(jnp.where(is_fwd16, row16 + 1.0, CHUNK - row16), log_mag, ang)
        ar_re, ar_im = _repeat_rows(ar_re, S5_GROUP), _repeat_rows(ar_im, S5_GROUP)
        rt_nat = jnp.concatenate([ar_re * cw_re - ar_im * cw_im, -(ar_re * cw_im + ar_im * cw_re)], axis=1)
        r_ref[s] = _chunk_row_order(rt_nat, s).T.astype(BF16)

        ap_re, ap_im = _cpow(jnp.full((1, half_l), float(CHUNK), F32), log_mag, ang)
        a_ref[s] = jnp.concatenate([ap_re, ap_im], axis=0)


def _s5_operators(lam_re, lam_im, log_dt, b_re, b_im, c_re, c_im, d_skip):
    half_l = 2 * S5_STATE

    def lanes(v):
        return jnp.transpose(v, (1, 0, 2)).reshape(S5_GROUPS, 1, half_l)

    ldt = lanes(jnp.broadcast_to(log_dt[..., None], lam_re.shape))
    bt_re = jnp.transpose(b_re, (1, 3, 0, 2)).reshape(S5_GROUPS, S5_GROUP, half_l)
    bt_im = jnp.transpose(b_im, (1, 3, 0, 2)).reshape(S5_GROUPS, S5_GROUP, half_l)
    ct_re = jnp.transpose(c_re, (1, 2, 0, 3)).reshape(S5_GROUPS, S5_GROUP, half_l)
    ct_im = jnp.transpose(c_im, (1, 2, 0, 3)).reshape(S5_GROUPS, S5_GROUP, half_l)
    dw = jnp.tile(d_skip, (1, CHUNK)).reshape(S5_GROUPS, 1, CHUNK_COLS)

    gs = GROUPS_PER_STEP
    vec = pl.BlockSpec((gs, 1, half_l), lambda i: (i, 0, 0))
    par = pl.BlockSpec((gs, S5_GROUP, half_l), lambda i: (i, 0, 0))
    mat = pl.BlockSpec((gs, CHUNK_COLS, CHUNK_COLS), lambda i: (i, 0, 0))
    mat_shape = jax.ShapeDtypeStruct((S5_GROUPS, CHUNK_COLS, CHUNK_COLS), BF16)
    return pl.pallas_call(
        _s5_prep_kernel,
        grid=(S5_GROUPS // gs,),
        in_specs=[vec, vec, vec, par, par, par, par,
                  pl.BlockSpec((gs, 1, CHUNK_COLS), lambda i: (i, 0, 0))],
        out_specs=[mat, mat, mat, pl.BlockSpec((gs, 2, half_l), lambda i: (i, 0, 0))],
        out_shape=[mat_shape, mat_shape, mat_shape,
                   jax.ShapeDtypeStruct((S5_GROUPS, 2, half_l), F32)],
        compiler_params=pltpu.CompilerParams(
            dimension_semantics=("parallel",), vmem_limit_bytes=VMEM_LIMIT),
        name="s5_operators",
    )(lanes(lam_re), lanes(lam_im), ldt, bt_re, bt_im, ct_re, ct_im, dw)


CONV_HALF = CONV_K // 2
CONV_LAG = CONV_HALF + 1
RING_SLOTS = 32
CONV_SUB = CONV_ROWS // 8
CONV_BLOCKS = STEP_ROWS // CONV_ROWS


BF16_ROWS = 16
TAP_GROUP = 8


def _aligned(x, m):
    return x if isinstance(x, int) else pl.multiple_of(x, m)


def _conv_block(ring, slots, blk, wb_ref, cb_ref, lg_ref, lb_ref):
    b = blk // (TOK_ROWS // CONV_ROWS)
    t0 = _aligned((blk % (TOK_ROWS // CONV_ROWS)) * CONV_ROWS, CONV_ROWS)
    packed = (CONV_ROWS // BF16_ROWS, BF16_ROWS, CONV_WIDTH)
    acc = jnp.zeros(packed, F32)
    for k0 in range(0, CONV_K, TAP_GROUP):
        group = None
        for k in range(k0, min(k0 + TAP_GROUP, CONV_K)):
            prod = wb_ref[k] * ring[slots[k], b, pl.ds(t0, CONV_ROWS), :].reshape(packed)
            group = prod if group is None else group + prod
        acc = acc + group.astype(F32)
    acc = acc.reshape(CONV_ROWS, CONV_WIDTH) + cb_ref[...]
    mu = jnp.mean(acc, axis=-1, keepdims=True)
    dev = acc - mu
    var = jnp.mean(dev * dev, axis=-1, keepdims=True)
    y = dev * lax.rsqrt(var + EPS_LN) * lg_ref[...] + lb_ref[...]
    return (y * jax.nn.sigmoid(y)).astype(BF16)


def _out_kernel(n_rows, x_ref, ys_ref, hc_ref, g1_ref, sh2_ref, sc2_ref, g2_ref, n2_ref, fg_ref,
                wb_ref, cb_ref, lg_ref, lb_ref, wglu_in, wout_in, w1_in, w2_in, o_ref,
                u_scr, ring, yc_scr, wglu_ref, wout_ref, w1_ref, w2_ref):
    s = pl.program_id(0)
    zero_row = jnp.zeros((BATCH, TOK_ROWS, CONV_WIDTH), BF16)

    @pl.when(s < CONV_LAG)
    def _():
        for w_in_ref, w_bf in ((wglu_in, wglu_ref), (wout_in, wout_ref), (w1_in, w1_ref), (w2_in, w2_ref)):
            rows = w_in_ref.shape[0]
            w_bf[pl.ds(pl.multiple_of(s * rows, rows), rows), :] = w_in_ref[...].astype(BF16)

    @pl.when(s == 0)
    def _():
        for i in range(RING_SLOTS - CONV_HALF, RING_SLOTS):
            ring[i] = zero_row

    @pl.when(s < n_rows)
    def _():
        ring[s % RING_SLOTS] = hc_ref[...]

    @pl.when(s >= n_rows)
    def _():
        ring[s % RING_SLOTS] = zero_row

    slots = [(s + (RING_SLOTS - 2 * CONV_HALF) + k) % RING_SLOTS for k in range(CONV_K)]
    conv_args = (wb_ref, cb_ref, lg_ref, lb_ref)

    @pl.when(s == CONV_HALF)
    def _():
        def body(blk, carry):
            rows = pl.ds(_aligned(blk * CONV_ROWS, CONV_ROWS), CONV_ROWS)
            yc_scr[rows, :] = _conv_block(ring, slots, blk, *conv_args)
            return carry
        lax.fori_loop(0, CONV_BLOCKS, body, 0)

    @pl.when(s >= CONV_LAG)
    def _():
        yc = yc_scr[...]
        ys = jnp.concatenate([ys_ref[g].astype(F32) for g in range(S5_GROUPS)], axis=1)
        _from_chunk_layout(ys, u_scr)
        y_s5 = jnp.concatenate(
            [jnp.concatenate([u_scr[vb, b * PITCH:b * PITCH + TOK_ROWS, :] for b in range(BATCH)], axis=0)
             for vb in range(LANE_BLOCKS)], axis=1)
        gl = jax.nn.gelu(y_s5)
        s5o = gl * jax.nn.sigmoid(_dot(gl.astype(BF16), wglu_ref[...]))
        mix = _dot(s5o.astype(BF16), wout_ref[:S5_WIDTH, :]) + _dot(yc, wout_ref[S5_WIDTH:, :])
        h1 = x_ref[...] + g1_ref[...] * mix.reshape(BATCH, TOK_ROWS, D_MODEL)
        a2 = _rms(h1, n2_ref[...]) * (1.0 + sc2_ref[...]) + sh2_ref[...]
        a2 = a2.reshape(STEP_ROWS, D_MODEL).astype(BF16)
        ff_tile = 1024
        acc = jnp.zeros((STEP_ROWS, D_MODEL), F32)
        for j in range(D_FF // ff_tile):
            hid = jnp.maximum(_dot(a2, w1_ref[:, j * ff_tile:(j + 1) * ff_tile]), 0.0)
            acc = acc + _dot((hid * hid).astype(BF16), w2_ref[j * ff_tile:(j + 1) * ff_tile, :])
        h2 = h1 + g2_ref[...] * acc.reshape(BATCH, TOK_ROWS, D_MODEL)
        o_ref[...] = _rms(h2, fg_ref[...])
        for blk in range(CONV_BLOCKS):
            yc_scr[blk * CONV_ROWS:(blk + 1) * CONV_ROWS, :] = _conv_block(ring, slots, blk, *conv_args)


def _out_block(x, ys, hc, g1, sh2, sc2, g2, n2, fg, wb, cb, lg, lb, wglu, wout, w1, w2):
    bsz, n, _ = x.shape
    assert bsz == BATCH and n % TOK_ROWS == 0
    n_rows = n // TOK_ROWS
    chunk_rows = STEP_CHUNKS * BATCH
    modv = pl.BlockSpec((BATCH, 1, D_MODEL), lambda s: (0, 0, 0))
    vec = pl.BlockSpec((1, D_MODEL), lambda s: (0, 0))
    cvec = pl.BlockSpec((1, CONV_WIDTH), lambda s: (0, 0))

    def const(shape):
        return pl.BlockSpec(shape, lambda s: (0,) * len(shape), pipeline_mode=pl.Buffered(1))

    def lagged(s):
        return jnp.maximum(s - CONV_LAG, 0)

    def staged(w):
        return pl.BlockSpec((w.shape[0] // CONV_LAG, w.shape[1]), lambda s: (jnp.minimum(s, CONV_LAG - 1), 0))

    return pl.pallas_call(
        functools.partial(_out_kernel, n_rows),
        grid=(n_rows + CONV_LAG,),
        in_specs=[
            pl.BlockSpec((BATCH, TOK_ROWS, D_MODEL), lambda s: (0, lagged(s), 0)),
            pl.BlockSpec((S5_GROUPS, chunk_rows, CHUNK_COLS), lambda s: (0, lagged(s), 0)),
            pl.BlockSpec((BATCH, TOK_ROWS, CONV_WIDTH), lambda s: (0, jnp.minimum(s, n_rows - 1), 0)),
            modv, modv, modv, modv, vec, vec,
            const((CONV_K, BF16_ROWS, CONV_WIDTH)), cvec, cvec, cvec,
            staged(wglu), staged(wout), staged(w1), staged(w2),
        ],
        out_specs=pl.BlockSpec((BATCH, TOK_ROWS, D_MODEL), lambda s: (0, lagged(s), 0)),
        out_shape=jax.ShapeDtypeStruct((bsz, n, D_MODEL), F32),
        scratch_shapes=[
            pltpu.VMEM((LANE_BLOCKS, BATCH * PITCH, LANES), F32),
            pltpu.VMEM((RING_SLOTS, BATCH, TOK_ROWS, CONV_WIDTH), BF16),
            pltpu.VMEM((STEP_ROWS, CONV_WIDTH), BF16),
            pltpu.VMEM(wglu.shape, BF16), pltpu.VMEM(wout.shape, BF16),
            pltpu.VMEM(w1.shape, BF16), pltpu.VMEM(w2.shape, BF16),
        ],
        compiler_params=pltpu.CompilerParams(
            dimension_semantics=("arbitrary",), vmem_limit_bytes=OUT_VMEM_LIMIT),
        name="out_block",
    )(x, ys, hc, g1, sh2, sc2, g2, n2, fg, wb, cb, lg, lb, wglu, wout, w1, w2)


def kernel(x, c, ctx, c_ctx, ada_w, ada_b, norm1_g, w_in, s5_lam_re, s5_lam_im, s5_log_dt, s5_b_re, s5_b_im, s5_c_re, s5_c_im, s5_d, s5_w_glu, conv_w, conv_b, conv_ln_g, conv_ln_b, w_out, norm2_g, mlp_w1, mlp_w2, final_g):
    bsz = x.shape[0]
    assert bsz == BATCH and bsz + 1 <= MOD_ROWS
    layer = 0

    cc = jnp.concatenate([c, c_ctx[None, :], jnp.zeros((MOD_ROWS - bsz - 1, D_MODEL), F32)], axis=0)
    mod = _mod_table(cc, ada_w[layer], ada_b[layer])
    sh1, sc1, g1, sh2, sc2, g2 = [m.reshape(bsz, 1, D_MODEL) for m in jnp.split(mod[:bsz], 6, axis=-1)]
    csh1 = jnp.broadcast_to(mod[bsz, :D_MODEL].reshape(1, 1, D_MODEL), (bsz, 1, D_MODEL))
    csc1 = jnp.broadcast_to(mod[bsz, D_MODEL:2 * D_MODEL].reshape(1, 1, D_MODEL), (bsz, 1, D_MODEL))

    n1 = norm1_g[layer].reshape(1, D_MODEL)
    x_lat, hc = _in_proj(x, sh1, sc1, n1, w_in[layer], True)
    x_ctx, = _in_proj(ctx, csh1, csc1, n1, w_in[layer], False)

    t_mat, m_mat, r_mat, a_pow = _s5_operators(
        s5_lam_re[layer], s5_lam_im[layer], s5_log_dt[layer], s5_b_re[layer], s5_b_im[layer],
        s5_c_re[layer], s5_c_im[layer], s5_d[layer])
    y_s5 = _s5_scan(x_lat, x_ctx, t_mat, m_mat, r_mat, a_pow, bsz)

    conv_wb = jnp.broadcast_to(conv_w[layer].astype(BF16)[:, None, :], (CONV_K, BF16_ROWS, CONV_WIDTH))
    return _out_block(
        x, y_s5, hc, g1, sh2, sc2, g2,
        norm2_g[layer].reshape(1, D_MODEL), final_g.reshape(1, D_MODEL),
        conv_wb, conv_b[layer].reshape(1, CONV_WIDTH),
        conv_ln_g[layer].reshape(1, CONV_WIDTH), conv_ln_b[layer].reshape(1, CONV_WIDTH),
        s5_w_glu[layer], w_out[layer], mlp_w1[layer], mlp_w2[layer])
```

```python
import functools

import jax
import jax.numpy as jnp
from jax import lax
from jax.experimental import pallas as pl
from jax.experimental.pallas import tpu as pltpu

F32 = jnp.float32
BF16 = jnp.bfloat16

D_MODEL = 1024
S5_WIDTH = 512
S5_GROUP = 16
S5_GROUPS = 32
S5_STATE = 64
CONV_WIDTH = 512
CONV_K = 31
GRID_W = 64
IN_COLS = S5_WIDTH + 2 * CONV_WIDTH
D_FF = 4 * D_MODEL
EPS_RMS = 1e-6
EPS_LN = 1e-5

CHUNK = 16
CHUNK_COLS = CHUNK * S5_GROUP
MOD_ROWS = 16
BATCH = 8
TOK_ROWS = 64
STEP_ROWS = BATCH * TOK_ROWS
STEP_CHUNKS = TOK_ROWS // CHUNK
LANES = 128
SLOTS = LANES // S5_GROUP
LANE_BLOCKS = S5_WIDTH // LANES
PITCH = 72
CONV_ROWS = 16
VMEM_LIMIT = 56 * 1024 * 1024
OUT_VMEM_LIMIT = 60 * 1024 * 1024


def _dot(a, b):
    return jnp.dot(a, b, preferred_element_type=F32)


def _rms(x, g):
    return x * lax.rsqrt(jnp.mean(x * x, axis=-1, keepdims=True) + EPS_RMS) * g


def _mod_kernel(c_ref, w_ref, b_ref, o_ref):
    c = c_ref[...]
    s = c * jax.nn.sigmoid(c)
    w = w_ref[...]
    s_hi = s.astype(BF16)
    s_lo = (s - s_hi.astype(F32)).astype(BF16)
    w_hi = w.astype(BF16)
    w_lo = (w - w_hi.astype(F32)).astype(BF16)
    o_ref[...] = _dot(s_hi, w_hi) + _dot(s_lo, w_hi) + _dot(s_hi, w_lo) + b_ref[...]


def _mod_table(cc, ada_w, ada_b):
    n_out = ada_w.shape[1]
    tn = 1536
    return pl.pallas_call(
        _mod_kernel,
        grid=(n_out // tn,),
        in_specs=[
            pl.BlockSpec((MOD_ROWS, D_MODEL), lambda j: (0, 0)),
            pl.BlockSpec((D_MODEL, tn), lambda j: (0, j)),
            pl.BlockSpec((1, tn), lambda j: (0, j)),
        ],
        out_specs=pl.BlockSpec((MOD_ROWS, tn), lambda j: (0, j)),
        out_shape=jax.ShapeDtypeStruct((MOD_ROWS, n_out), F32),
        compiler_params=pltpu.CompilerParams(vmem_limit_bytes=VMEM_LIMIT),
        name="mod_table",
    )(cc, ada_w, ada_b.reshape(1, n_out))


def _slot_skew(vregs):
    slot = lax.broadcasted_iota(jnp.int32, (BATCH, LANES), 1) // S5_GROUP
    cur = list(vregs)
    for b in range(3):
        bit_set = (slot & (1 << b)) != 0
        cur = [jnp.where(bit_set, cur[(t + (1 << b)) % SLOTS], cur[t]) for t in range(SLOTS)]
    return cur


def _to_chunk_layout(z_scr, xs_scr):
    for cl in range(STEP_CHUNKS):
        for vb in range(LANE_BLOCKS):
            for half in range(2):
                rot = []
                for ql in range(SLOTS):
                    piece = z_scr[vb, pl.ds(cl * CHUNK + 8 * half + ql, BATCH, stride=PITCH), :]
                    rot.append(pltpu.roll(piece, ql * S5_GROUP, axis=1) if ql else piece)
                skew = _slot_skew(rot)
                for s in range(SLOTS):
                    col = (2 * (SLOTS * vb + s) + half) * LANES
                    xs_scr[cl * BATCH:(cl + 1) * BATCH, col:col + LANES] = skew[(-s) % SLOTS]


def _from_chunk_layout(ys, u_scr):
    for cl in range(STEP_CHUNKS):
        for vb in range(LANE_BLOCKS):
            for half in range(2):
                src = []
                for s in range(SLOTS):
                    col = (2 * (SLOTS * vb + s) + half) * LANES
                    src.append(ys[cl * BATCH:(cl + 1) * BATCH, col:col + LANES])
                skew = _slot_skew(src)
                for ql in range(SLOTS):
                    out = skew[(-ql) % SLOTS]
                    if ql:
                        out = pltpu.roll(out, LANES - ql * S5_GROUP, axis=1)
                    u_scr[vb, pl.ds(cl * CHUNK + 8 * half + ql, BATCH, stride=PITCH), :] = out


def _inproj_kernel(with_conv, x_ref, sh_ref, sc_ref, g_ref, w_ref, xs_ref, *rest):
    hc_ref = rest[0] if with_conv else None
    z_scr, xs_scr, w_bf = rest[-3:]

    @pl.when(pl.program_id(0) == 0)
    def _():
        w_bf[...] = w_ref[...].astype(BF16)

    x = x_ref[...]
    a = _rms(x, g_ref[...]) * (1.0 + sc_ref[...]) + sh_ref[...]
    z = _dot(a.reshape(STEP_ROWS, D_MODEL).astype(BF16), w_bf[...])
    if with_conv:
        v = z[:, S5_WIDTH:S5_WIDTH + CONV_WIDTH]
        gate = z[:, S5_WIDTH + CONV_WIDTH:]
        hc_ref[...] = (v * jax.nn.sigmoid(gate)).astype(BF16).reshape(BATCH, TOK_ROWS, CONV_WIDTH)
    for vb in range(LANE_BLOCKS):
        for b in range(BATCH):
            z_scr[vb, b * PITCH:b * PITCH + TOK_ROWS, :] = z[b * TOK_ROWS:(b + 1) * TOK_ROWS, vb * LANES:(vb + 1) * LANES]
    _to_chunk_layout(z_scr, xs_scr)
    for g in range(S5_GROUPS):
        xs_ref[g] = xs_scr[:, g * CHUNK_COLS:(g + 1) * CHUNK_COLS].astype(BF16)


def _in_proj(x, sh, sc, g, w_in, with_conv):
    bsz, n, _ = x.shape
    assert bsz == BATCH and n % TOK_ROWS == 0
    steps = n // TOK_ROWS
    chunk_rows = STEP_CHUNKS * BATCH
    n_cols = IN_COLS if with_conv else S5_WIDTH
    modv = pl.BlockSpec((BATCH, 1, D_MODEL), lambda i: (0, 0, 0))
    out_specs = [pl.BlockSpec((S5_GROUPS, chunk_rows, CHUNK_COLS), lambda i: (0, i, 0))]
    out_shape = [jax.ShapeDtypeStruct((S5_GROUPS, steps * chunk_rows, CHUNK_COLS), BF16)]
    if with_conv:
        out_specs.append(pl.BlockSpec((BATCH, TOK_ROWS, CONV_WIDTH), lambda i: (0, i, 0)))
        out_shape.append(jax.ShapeDtypeStruct((bsz, n, CONV_WIDTH), BF16))
    return pl.pallas_call(
        functools.partial(_inproj_kernel, with_conv),
        grid=(steps,),
        in_specs=[
            pl.BlockSpec((BATCH, TOK_ROWS, D_MODEL), lambda i: (0, i, 0)),
            modv, modv,
            pl.BlockSpec((1, D_MODEL), lambda i: (0, 0)),
            pl.BlockSpec((D_MODEL, n_cols), lambda i: (0, 0), pipeline_mode=pl.Buffered(1)),
        ],
        out_specs=out_specs,
        out_shape=out_shape,
        scratch_shapes=[
            pltpu.VMEM((LANE_BLOCKS, BATCH * PITCH, LANES), F32),
            pltpu.VMEM((chunk_rows, S5_GROUPS * CHUNK_COLS), F32),
            pltpu.VMEM((D_MODEL, n_cols), BF16),
        ],
        compiler_params=pltpu.CompilerParams(
            dimension_semantics=("arbitrary",), vmem_limit_bytes=VMEM_LIMIT),
        name="in_proj" if with_conv else "in_proj_ctx",
    )(x, sh, sc, g, w_in)


SCAN_GROUPS = 4


def _s5_kernel(n_lat, n_ctx, xl_ref, xc_ref, t_ref, m_ref, r_ref, a_ref, y_ref, e_ref, p_ref, ec_ref):
    half = 2 * S5_STATE
    groups = range(SCAN_GROUPS)
    for g in groups:
        ec_ref[g] = _dot(xc_ref[g], m_ref[g])
        e_ref[g] = _dot(xl_ref[g], m_ref[g])
    a_re = [jnp.broadcast_to(a_ref[g, 0:1, :], (8, half)) for g in groups]
    a_im = [jnp.broadcast_to(a_ref[g, 1:2, :], (8, half)) for g in groups]
    is_fwd = lax.broadcasted_iota(jnp.int32, (8, half), 1) < S5_STATE

    def advance(src_ref, g, i, j, s_re, s_im):
        ef = src_ref[g, pl.ds(pl.multiple_of(i * 8, 8), 8), :]
        eb = src_ref[g, pl.ds(pl.multiple_of(j * 8, 8), 8), :]
        e_re = jnp.where(is_fwd, ef[:, :half], eb[:, :half])
        e_im = jnp.where(is_fwd, ef[:, half:], eb[:, half:])
        return (a_re[g] * s_re - a_im[g] * s_im + e_re, a_re[g] * s_im + a_im[g] * s_re + e_im)

    def ctx_body(i, s):
        return tuple(advance(ec_ref, g, i, n_ctx - 1 - i, *s[g]) for g in groups)

    zero = jnp.zeros((8, half), F32)
    s = lax.fori_loop(0, n_ctx, ctx_body, tuple((zero, zero) for _ in groups))

    def lat_body(i, s):
        j = n_lat - 1 - i
        ri = pl.ds(pl.multiple_of(i * 8, 8), 8)
        rj = pl.ds(pl.multiple_of(j * 8, 8), 8)
        out = []
        for g in groups:
            s_re, s_im = s[g]
            p_ref[g, ri, 0:S5_STATE] = s_re[:, :S5_STATE]
            p_ref[g, rj, S5_STATE:half] = s_re[:, S5_STATE:]
            p_ref[g, ri, half:half + S5_STATE] = s_im[:, :S5_STATE]
            p_ref[g, rj, half + S5_STATE:] = s_im[:, S5_STATE:]
            out.append(advance(e_ref, g, i, j, s_re, s_im))
        return tuple(out)

    lax.fori_loop(0, n_lat, lat_body, s)
    for g in groups:
        y = _dot(xl_ref[g], t_ref[g]) + _dot(p_ref[g].astype(BF16), r_ref[g])
        y_ref[g] = y.astype(BF16)


def _s5_scan(x_lat, x_ctx, t_mat, m_mat, r_mat, a_pow, bsz):
    rows = x_lat.shape[1]
    rows_c = x_ctx.shape[1]
    gs = SCAN_GROUPS

    def per_group(*shape):
        return pl.BlockSpec((gs,) + shape, lambda i: (i,) + (0,) * len(shape))

    mat_spec = per_group(CHUNK_COLS, CHUNK_COLS)
    return pl.pallas_call(
        functools.partial(_s5_kernel, rows // bsz, rows_c // bsz),
        grid=(S5_GROUPS // gs,),
        in_specs=[
            per_group(rows, CHUNK_COLS), per_group(rows_c, CHUNK_COLS),
            mat_spec, mat_spec, mat_spec,
            per_group(2, 2 * S5_STATE),
        ],
        out_specs=per_group(rows, CHUNK_COLS),
        out_shape=jax.ShapeDtypeStruct((S5_GROUPS, rows, CHUNK_COLS), BF16),
        scratch_shapes=[
            pltpu.VMEM((gs, rows, CHUNK_COLS), F32),
            pltpu.VMEM((gs, rows, CHUNK_COLS), F32),
            pltpu.VMEM((gs, rows_c, CHUNK_COLS), F32),
        ],
        compiler_params=pltpu.CompilerParams(
            dimension_semantics=("parallel",), vmem_limit_bytes=VMEM_LIMIT),
        name="s5_scan",
    )(x_lat, x_ctx, t_mat, m_mat, r_mat, a_pow)


GROUPS_PER_STEP = SLOTS


def _cpow(kk, log_mag, ang):
    mag = jnp.exp(kk * log_mag)
    return mag * jnp.cos(kk * ang), mag * jnp.sin(kk * ang)


def _repeat_rows(a, reps):
    return jnp.concatenate(
        [jnp.broadcast_to(a[r:r + 1, :], (reps, a.shape[1])) for r in range(a.shape[0])], axis=0)


def _tile_rows(a, reps):
    return jnp.concatenate([a] * reps, axis=0)


def _dot3_nt(a, b):
    dn = (((1,), (1,)), ((), ()))
    a_hi = a.astype(BF16)
    a_lo = (a - a_hi.astype(F32)).astype(BF16)
    b_hi = b.astype(BF16)
    b_lo = (b - b_hi.astype(F32)).astype(BF16)

    def nt(u, v):
        return lax.dot_general(u, v, dn, preferred_element_type=F32)

    return nt(a_hi, b_hi) + nt(a_lo, b_hi) + nt(a_hi, b_lo)


def _chunk_row_order(a, s):
    blocks = []
    for half in range(2):
        for slot in range(SLOTS):
            q = 8 * half + (slot - s) % SLOTS
            blocks.append(a[q * S5_GROUP:(q + 1) * S5_GROUP, :])
    return jnp.concatenate(blocks, axis=0)


def _s5_prep_kernel(lre_ref, lim_ref, ldt_ref, btr_ref, bti_ref, ctr_ref, cti_ref, dw_ref,
                    t_ref, m_ref, r_ref, a_ref):
    half_l = 2 * S5_STATE
    n_lag = 2 * CHUNK - 1
    is_fwd16 = lax.broadcasted_iota(jnp.int32, (CHUNK, half_l), 1) < S5_STATE
    is_fwd32 = lax.broadcasted_iota(jnp.int32, (2 * CHUNK, half_l), 1) < S5_STATE
    row16 = lax.broadcasted_iota(jnp.int32, (CHUNK, half_l), 0).astype(F32)
    row32 = lax.broadcasted_iota(jnp.int32, (2 * CHUNK, half_l), 0)
    slot_of_lane = lax.broadcasted_iota(jnp.int32, (CHUNK_COLS, LANES), 1) // S5_GROUP
    eye = (lax.broadcasted_iota(jnp.int32, (CHUNK_COLS, CHUNK_COLS), 0)
           == lax.broadcasted_iota(jnp.int32, (CHUNK_COLS, CHUNK_COLS), 1))

    for s in range(GROUPS_PER_STEP):
        lre, lim = lre_ref[s], lim_ref[s]
        dt = jnp.exp(ldt_ref[s])
        log_mag, ang = lre * dt, lim * dt
        a1_re, a1_im = _cpow(jnp.ones((1, half_l), F32), log_mag, ang)
        den = lre * lre + lim * lim
        num_re = a1_re - 1.0
        f_re = (num_re * lre + a1_im * lim) / den
        f_im = (a1_im * lre - num_re * lim) / den
        bt_re, bt_im = btr_ref[s], bti_ref[s]
        bb_re = f_re * bt_re - f_im * bt_im
        bb_im = f_re * bt_im + f_im * bt_re
        ct_re, ct_im = ctr_ref[s], cti_ref[s]

        lag = jnp.abs(row32 - (CHUNK - 1))
        aj_re, aj_im = _cpow(lag.astype(F32), log_mag, ang)
        aj_re = jnp.where(is_fwd32, jnp.where(row32 <= CHUNK - 1, aj_re, 0.0),
                          jnp.where(row32 >= CHUNK - 1, aj_re, 0.0))[:n_lag]
        aj_im = jnp.where(is_fwd32, jnp.where(row32 <= CHUNK - 1, aj_im, 0.0),
                          jnp.where(row32 >= CHUNK - 1, aj_im, 0.0))[:n_lag]
        aj_re, aj_im = _repeat_rows(aj_re, S5_GROUP), _repeat_rows(aj_im, S5_GROUP)
        bl_re, bl_im = _tile_rows(bb_re, n_lag), _tile_rows(bb_im, n_lag)
        l_re = aj_re * bl_re - aj_im * bl_im
        l_im = aj_re * bl_im + aj_im * bl_re
        cw_re, cw_im = _tile_rows(ct_re, CHUNK), _tile_rows(ct_im, CHUNK)
        kwide = _dot3_nt(jnp.concatenate([l_re, -l_im], axis=1),
                         jnp.concatenate([cw_re, cw_im], axis=1))
        halves = []
        for half in range(2):
            col = kwide[:, half * LANES:(half + 1) * LANES]
            out = None
            for slot in range(SLOTS):
                q = 8 * half + (slot - s) % SLOTS
                start = (CHUNK - 1 - q) * S5_GROUP
                win = col[start:start + CHUNK_COLS, :]
                out = win if out is None else jnp.where(slot_of_lane == slot, win, out)
            halves.append(out)
        t_nat = jnp.concatenate(halves, axis=1)
        t_mat = _chunk_row_order(t_nat, s) + jnp.where(eye, dw_ref[s], 0.0)
        t_ref[s] = t_mat.astype(BF16)

        am_re, am_im = _cpow(jnp.where(is_fwd16, CHUNK - 1.0 - row16, row16), log_mag, ang)
        am_re, am_im = _repeat_rows(am_re, S5_GROUP), _repeat_rows(am_im, S5_GROUP)
        bm_re, bm_im = _tile_rows(bb_re, CHUNK), _tile_rows(bb_im, CHUNK)
        m_nat = jnp.concatenate([am_re * bm_re - am_im * bm_im, am_re * bm_im + am_im * bm_re], axis=1)
        m_ref[s] = _chunk_row_order(m_nat, s).astype(BF16)

        ar_re, ar_im = _cpow(jnp.where(is_fwd16, row16 + 1.0, CHUNK - row16), log_mag, ang)
        ar_re, ar_im = _repeat_rows(ar_re, S5_GROUP), _repeat_rows(ar_im, S5_GROUP)
        rt_nat = jnp.concatenate([ar_re * cw_re - ar_im * cw_im, -(ar_re * cw_im + ar_im * cw_re)], axis=1)
        r_ref[s] = _chunk_row_order(rt_nat, s).T.astype(BF16)

        ap_re, ap_im = _cpow(jnp.full((1, half_l), float(CHUNK), F32), log_mag, ang)
        a_ref[s] = jnp.concatenate([ap_re, ap_im], axis=0)


def _s5_operators(lam_re, lam_im, log_dt, b_re, b_im, c_re, c_im, d_skip):
    half_l = 2 * S5_STATE

    def lanes(v):
        return jnp.transpose(v, (1, 0, 2)).reshape(S5_GROUPS, 1, half_l)

    ldt = lanes(jnp.broadcast_to(log_dt[..., None], lam_re.shape))
    bt_re = jnp.transpose(b_re, (1, 3, 0, 2)).reshape(S5_GROUPS, S5_GROUP, half_l)
    bt_im = jnp.transpose(b_im, (1, 3, 0, 2)).reshape(S5_GROUPS, S5_GROUP, half_l)
    ct_re = jnp.transpose(c_re, (1, 2, 0, 3)).reshape(S5_GROUPS, S5_GROUP, half_l)
    ct_im = jnp.transpose(c_im, (1, 2, 0, 3)).reshape(S5_GROUPS, S5_GROUP, half_l)
    dw = jnp.tile(d_skip, (1, CHUNK)).reshape(S5_GROUPS, 1, CHUNK_COLS)

    gs = GROUPS_PER_STEP
    vec = pl.BlockSpec((gs, 1, half_l), lambda i: (i, 0, 0))
    par = pl.BlockSpec((gs, S5_GROUP, half_l), lambda i: (i, 0, 0))
    mat = pl.BlockSpec((gs, CHUNK_COLS, CHUNK_COLS), lambda i: (i, 0, 0))
    mat_shape = jax.ShapeDtypeStruct((S5_GROUPS, CHUNK_COLS, CHUNK_COLS), BF16)
    return pl.pallas_call(
        _s5_prep_kernel,
        grid=(S5_GROUPS // gs,),
        in_specs=[vec, vec, vec, par, par, par, par,
                  pl.BlockSpec((gs, 1, CHUNK_COLS), lambda i: (i, 0, 0))],
        out_specs=[mat, mat, mat, pl.BlockSpec((gs, 2, half_l), lambda i: (i, 0, 0))],
        out_shape=[mat_shape, mat_shape, mat_shape,
                   jax.ShapeDtypeStruct((S5_GROUPS, 2, half_l), F32)],
        compiler_params=pltpu.CompilerParams(
            dimension_semantics=("parallel",), vmem_limit_bytes=VMEM_LIMIT),
        name="s5_operators",
    )(lanes(lam_re), lanes(lam_im), ldt, bt_re, bt_im, ct_re, ct_im, dw)


CONV_HALF = CONV_K // 2
CONV_LAG = CONV_HALF + 1
RING_SLOTS = 32
CONV_SUB = CONV_ROWS // 8
CONV_BLOCKS = STEP_ROWS // CONV_ROWS


BF16_ROWS = 16
TAP_GROUP = 8


def _aligned(x, m):
    return x if isinstance(x, int) else pl.multiple_of(x, m)


def _conv_block(ring, slots, blk, wb_ref, cb_ref, lg_ref, lb_ref, seed=None):
    b = blk // (TOK_ROWS // CONV_ROWS)
    t0 = _aligned((blk % (TOK_ROWS // CONV_ROWS)) * CONV_ROWS, CONV_ROWS)
    packed = (CONV_ROWS // BF16_ROWS, BF16_ROWS, CONV_WIDTH)
    acc = jnp.zeros(packed, F32)
    if seed is not None:
        rows = jnp.concatenate([seed] * (BF16_ROWS // 8), axis=0)
        acc = acc + jnp.concatenate([rows] * (CONV_WIDTH // LANES), axis=1)[None]
    for k0 in range(0, CONV_K, TAP_GROUP):
        group = None
        for k in range(k0, min(k0 + TAP_GROUP, CONV_K)):
            prod = wb_ref[k] * ring[slots[k], b, pl.ds(t0, CONV_ROWS), :].reshape(packed)
            group = prod if group is None else group + prod
        acc = acc + group.astype(F32)
    acc = acc.reshape(CONV_ROWS, CONV_WIDTH) + cb_ref[...]
    mu = jnp.mean(acc, axis=-1, keepdims=True)
    dev = acc - mu
    var = jnp.mean(dev * dev, axis=-1, keepdims=True)
    y = dev * lax.rsqrt(var + EPS_LN) * lg_ref[...] + lb_ref[...]
    return (y * jax.nn.sigmoid(y)).astype(BF16)


def _out_kernel(n_rows, x_ref, ys_ref, hc_ref, g1_ref, sh2_ref, sc2_ref, g2_ref, n2_ref, fg_ref,
                wb_ref, cb_ref, lg_ref, lb_ref, wglu_in, wout_in, w1_in, w2_in, o_ref,
                u_scr, ring, yc_scr, wglu_ref, wout_ref, w1_ref, w2_ref):
    s = pl.program_id(0)
    zero_row = jnp.zeros((BATCH, TOK_ROWS, CONV_WIDTH), BF16)

    @pl.when(s < CONV_LAG)
    def _():
        for w_in_ref, w_bf in ((wglu_in, wglu_ref), (wout_in, wout_ref), (w1_in, w1_ref), (w2_in, w2_ref)):
            rows = w_in_ref.shape[0]
            w_bf[pl.ds(pl.multiple_of(s * rows, rows), rows), :] = w_in_ref[...].astype(BF16)

    @pl.when(s == 0)
    def _():
        for i in range(RING_SLOTS - CONV_HALF, RING_SLOTS):
            ring[i] = zero_row

    @pl.when(s < n_rows)
    def _():
        ring[s % RING_SLOTS] = hc_ref[...]

    @pl.when(s >= n_rows)
    def _():
        ring[s % RING_SLOTS] = zero_row

    slots = [(s + (RING_SLOTS - 2 * CONV_HALF) + k) % RING_SLOTS for k in range(CONV_K)]
    conv_args = (wb_ref, cb_ref, lg_ref, lb_ref)

    @pl.when(s == CONV_HALF)
    def _():
        def body(blk, carry):
            rows = pl.ds(_aligned(blk * CONV_ROWS, CONV_ROWS), CONV_ROWS)
            yc_scr[rows, :] = _conv_block(ring, slots, blk, *conv_args)
            return carry
        lax.fori_loop(0, CONV_BLOCKS, body, 0)

    @pl.when(s >= CONV_LAG)
    def _():
        yc = yc_scr[...]
        ys = jnp.concatenate([ys_ref[g].astype(F32) for g in range(S5_GROUPS)], axis=1)
        _from_chunk_layout(ys, u_scr)
        y_s5 = jnp.concatenate(
            [jnp.concatenate([u_scr[vb, b * PITCH:b * PITCH + TOK_ROWS, :] for b in range(BATCH)], axis=0)
             for vb in range(LANE_BLOCKS)], axis=1)
        gl = jax.nn.gelu(y_s5)
        s5o = gl * jax.nn.sigmoid(_dot(gl.astype(BF16), wglu_ref[...]))
        mix = _dot(s5o.astype(BF16), wout_ref[:S5_WIDTH, :]) + _dot(yc, wout_ref[S5_WIDTH:, :])
        h1 = x_ref[...] + g1_ref[...] * mix.reshape(BATCH, TOK_ROWS, D_MODEL)
        a2 = _rms(h1, n2_ref[...]) * (1.0 + sc2_ref[...]) + sh2_ref[...]
        a2 = a2.reshape(STEP_ROWS, D_MODEL).astype(BF16)
        ff_tile = 1024
        n_ff = D_FF // ff_tile
        per_dot = CONV_BLOCKS // (2 * n_ff)
        never = s < 0

        def seeded_conv(first_blk, mat):
            for i in range(per_dot):
                blk = first_blk + i
                r = i * (STEP_ROWS // per_dot)
                seed = jnp.where(never, mat[r:r + 8, 0:LANES], 0.0)
                yc_scr[blk * CONV_ROWS:(blk + 1) * CONV_ROWS, :] = _conv_block(
                    ring, slots, blk, *conv_args, seed=seed)

        acc = jnp.zeros((STEP_ROWS, D_MODEL), F32)
        for j in range(n_ff):
            up = _dot(a2, w1_ref[:, j * ff_tile:(j + 1) * ff_tile])
            seeded_conv(2 * j * per_dot, up)
            hid = jnp.maximum(up, 0.0)
            down = _dot((hid * hid).astype(BF16), w2_ref[j * ff_tile:(j + 1) * ff_tile, :])
            seeded_conv((2 * j + 1) * per_dot, down)
            acc = acc + down
        h2 = h1 + g2_ref[...] * acc.reshape(BATCH, TOK_ROWS, D_MODEL)
        o_ref[...] = _rms(h2, fg_ref[...])


def _out_block(x, ys, hc, g1, sh2, sc2, g2, n2, fg, wb, cb, lg, lb, wglu, wout, w1, w2):
    bsz, n, _ = x.shape
    assert bsz == BATCH and n % TOK_ROWS == 0
    n_rows = n // TOK_ROWS
    chunk_rows = STEP_CHUNKS * BATCH
    modv = pl.BlockSpec((BATCH, 1, D_MODEL), lambda s: (0, 0, 0))
    vec = pl.BlockSpec((1, D_MODEL), lambda s: (0, 0))
    cvec = pl.BlockSpec((1, CONV_WIDTH), lambda s: (0, 0))

    def const(shape):
        return pl.BlockSpec(shape, lambda s: (0,) * len(shape), pipeline_mode=pl.Buffered(1))

    def lagged(s):
        return jnp.maximum(s - CONV_LAG, 0)

    def staged(w):
        return pl.BlockSpec((w.shape[0] // CONV_LAG, w.shape[1]), lambda s: (jnp.minimum(s, CONV_LAG - 1), 0))

    return pl.pallas_call(
        functools.partial(_out_kernel, n_rows),
        grid=(n_rows + CONV_LAG,),
        in_specs=[
            pl.BlockSpec((BATCH, TOK_ROWS, D_MODEL), lambda s: (0, lagged(s), 0)),
            pl.BlockSpec((S5_GROUPS, chunk_rows, CHUNK_COLS), lambda s: (0, lagged(s), 0)),
            pl.BlockSpec((BATCH, TOK_ROWS, CONV_WIDTH), lambda s: (0, jnp.minimum(s, n_rows - 1), 0)),
            modv, modv, modv, modv, vec, vec,
            const((CONV_K, BF16_ROWS, CONV_WIDTH)), cvec, cvec, cvec,
            staged(wglu), staged(wout), staged(w1), staged(w2),
        ],
        out_specs=pl.BlockSpec((BATCH, TOK_ROWS, D_MODEL), lambda s: (0, lagged(s), 0)),
        out_shape=jax.ShapeDtypeStruct((bsz, n, D_MODEL), F32),
        scratch_shapes=[
            pltpu.VMEM((LANE_BLOCKS, BATCH * PITCH, LANES), F32),
            pltpu.VMEM((RING_SLOTS, BATCH, TOK_ROWS, CONV_WIDTH), BF16),
            pltpu.VMEM((STEP_ROWS, CONV_WIDTH), BF16),
            pltpu.VMEM(wglu.shape, BF16), pltpu.VMEM(wout.shape, BF16),
            pltpu.VMEM(w1.shape, BF16), pltpu.VMEM(w2.shape, BF16),
        ],
        compiler_params=pltpu.CompilerParams(
            dimension_semantics=("arbitrary",), vmem_limit_bytes=OUT_VMEM_LIMIT),
        name="out_block",
    )(x, ys, hc, g1, sh2, sc2, g2, n2, fg, wb, cb, lg, lb, wglu, wout, w1, w2)


def kernel(x, c, ctx, c_ctx, ada_w, ada_b, norm1_g, w_in, s5_lam_re, s5_lam_im, s5_log_dt, s5_b_re, s5_b_im, s5_c_re, s5_c_im, s5_d, s5_w_glu, conv_w, conv_b, conv_ln_g, conv_ln_b, w_out, norm2_g, mlp_w1, mlp_w2, final_g):
    bsz = x.shape[0]
    assert bsz == BATCH and bsz + 1 <= MOD_ROWS
    layer = 0

    cc = jnp.concatenate([c, c_ctx[None, :], jnp.zeros((MOD_ROWS - bsz - 1, D_MODEL), F32)], axis=0)
    mod = _mod_table(cc, ada_w[layer], ada_b[layer])
    sh1, sc1, g1, sh2, sc2, g2 = [m.reshape(bsz, 1, D_MODEL) for m in jnp.split(mod[:bsz], 6, axis=-1)]
    csh1 = jnp.broadcast_to(mod[bsz, :D_MODEL].reshape(1, 1, D_MODEL), (bsz, 1, D_MODEL))
    csc1 = jnp.broadcast_to(mod[bsz, D_MODEL:2 * D_MODEL].reshape(1, 1, D_MODEL), (bsz, 1, D_MODEL))

    n1 = norm1_g[layer].reshape(1, D_MODEL)
    x_lat, hc = _in_proj(x, sh1, sc1, n1, w_in[layer], True)
    x_ctx, = _in_proj(ctx, csh1, csc1, n1, w_in[layer], False)

    t_mat, m_mat, r_mat, a_pow = _s5_operators(
        s5_lam_re[layer], s5_lam_im[layer], s5_log_dt[layer], s5_b_re[layer], s5_b_im[layer],
        s5_c_re[layer], s5_c_im[layer], s5_d[layer])
    y_s5 = _s5_scan(x_lat, x_ctx, t_mat, m_mat, r_mat, a_pow, bsz)

    conv_wb = jnp.broadcast_to(conv_w[layer].astype(BF16)[:, None, :], (CONV_K, BF16_ROWS, CONV_WIDTH))
    return _out_block(
        x, y_s5, hc, g1, sh2, sc2, g2,
        norm2_g[layer].reshape(1, D_MODEL), final_g.reshape(1, D_MODEL),
        conv_wb, conv_b[layer].reshape(1, CONV_WIDTH),
        conv_ln_g[layer].reshape(1, CONV_WIDTH), conv_ln_b[layer].reshape(1, CONV_WIDTH),
        s5_w_glu[layer], w_out[layer], mlp_w1[layer], mlp_w2[layer])
```

```python
import functools

import jax
import jax.numpy as jnp
from jax import lax
from jax.experimental import pallas as pl
from jax.experimental.pallas import tpu as pltpu

F32 = jnp.float32
BF16 = jnp.bfloat16

D_MODEL = 1024
S5_WIDTH = 512
S5_GROUP = 16
S5_GROUPS = 32
S5_STATE = 64
CONV_WIDTH = 512
CONV_K = 31
GRID_W = 64
IN_COLS = S5_WIDTH + 2 * CONV_WIDTH
D_FF = 4 * D_MODEL
EPS_RMS = 1e-6
EPS_LN = 1e-5

CHUNK = 16
CHUNK_COLS = CHUNK * S5_GROUP
MOD_ROWS = 16
BATCH = 8
TOK_ROWS = 64
STEP_ROWS = BATCH * TOK_ROWS
STEP_CHUNKS = TOK_ROWS // CHUNK
LANES = 128
SLOTS = LANES // S5_GROUP
LANE_BLOCKS = S5_WIDTH // LANES
PITCH = 72
CONV_ROWS = 16
VMEM_LIMIT = 56 * 1024 * 1024
OUT_VMEM_LIMIT = 60 * 1024 * 1024


def _dot(a, b):
    return jnp.dot(a, b, preferred_element_type=F32)


def _rms(x, g):
    return x * lax.rsqrt(jnp.mean(x * x, axis=-1, keepdims=True) + EPS_RMS) * g


def _mod_kernel(c_ref, w_ref, b_ref, o_ref):
    c = c_ref[...]
    s = c * jax.nn.sigmoid(c)
    w = w_ref[...]
    s_hi = s.astype(BF16)
    s_lo = (s - s_hi.astype(F32)).astype(BF16)
    w_hi = w.astype(BF16)
    w_lo = (w - w_hi.astype(F32)).astype(BF16)
    o_ref[...] = _dot(s_hi, w_hi) + _dot(s_lo, w_hi) + _dot(s_hi, w_lo) + b_ref[...]


def _mod_table(cc, ada_w, ada_b):
    n_out = ada_w.shape[1]
    tn = 1536
    return pl.pallas_call(
        _mod_kernel,
        grid=(n_out // tn,),
        in_specs=[
            pl.BlockSpec((MOD_ROWS, D_MODEL), lambda j: (0, 0)),
            pl.BlockSpec((D_MODEL, tn), lambda j: (0, j)),
            pl.BlockSpec((1, tn), lambda j: (0, j)),
        ],
        out_specs=pl.BlockSpec((MOD_ROWS, tn), lambda j: (0, j)),
        out_shape=jax.ShapeDtypeStruct((MOD_ROWS, n_out), F32),
        compiler_params=pltpu.CompilerParams(vmem_limit_bytes=VMEM_LIMIT),
        name="mod_table",
    )(cc, ada_w, ada_b.reshape(1, n_out))


def _slot_skew(vregs):
    slot = lax.broadcasted_iota(jnp.int32, (BATCH, LANES), 1) // S5_GROUP
    cur = list(vregs)
    for b in range(3):
        bit_set = (slot & (1 << b)) != 0
        cur = [jnp.where(bit_set, cur[(t + (1 << b)) % SLOTS], cur[t]) for t in range(SLOTS)]
    return cur


def _to_chunk_layout(z_scr, xs_scr):
    for cl in range(STEP_CHUNKS):
        for vb in range(LANE_BLOCKS):
            for half in range(2):
                rot = []
                for ql in range(SLOTS):
                    piece = z_scr[vb, pl.ds(cl * CHUNK + 8 * half + ql, BATCH, stride=PITCH), :]
                    rot.append(pltpu.roll(piece, ql * S5_GROUP, axis=1) if ql else piece)
                skew = _slot_skew(rot)
                for s in range(SLOTS):
                    col = (2 * (SLOTS * vb + s) + half) * LANES
                    xs_scr[cl * BATCH:(cl + 1) * BATCH, col:col + LANES] = skew[(-s) % SLOTS]


def _from_chunk_layout(ys, u_scr):
    for cl in range(STEP_CHUNKS):
        for vb in range(LANE_BLOCKS):
            for half in range(2):
                src = []
                for s in range(SLOTS):
                    col = (2 * (SLOTS * vb + s) + half) * LANES
                    src.append(ys[cl * BATCH:(cl + 1) * BATCH, col:col + LANES])
                skew = _slot_skew(src)
                for ql in range(SLOTS):
                    out = skew[(-ql) % SLOTS]
                    if ql:
                        out = pltpu.roll(out, LANES - ql * S5_GROUP, axis=1)
                    u_scr[vb, pl.ds(cl * CHUNK + 8 * half + ql, BATCH, stride=PITCH), :] = out


def _inproj_kernel(with_conv, x_ref, sh_ref, sc_ref, g_ref, w_ref, xs_ref, *rest):
    hc_ref = rest[0] if with_conv else None
    z_scr, xs_scr, w_bf = rest[-3:]

    @pl.when(pl.program_id(0) == 0)
    def _():
        w_bf[...] = w_ref[...].astype(BF16)

    x = x_ref[...]
    a = _rms(x, g_ref[...]) * (1.0 + sc_ref[...]) + sh_ref[...]
    z = _dot(a.reshape(STEP_ROWS, D_MODEL).astype(BF16), w_bf[...])
    if with_conv:
        v = z[:, S5_WIDTH:S5_WIDTH + CONV_WIDTH]
        gate = z[:, S5_WIDTH + CONV_WIDTH:]
        hc_ref[...] = (v * jax.nn.sigmoid(gate)).astype(BF16).reshape(BATCH, TOK_ROWS, CONV_WIDTH)
    for vb in range(LANE_BLOCKS):
        for b in range(BATCH):
            z_scr[vb, b * PITCH:b * PITCH + TOK_ROWS, :] = z[b * TOK_ROWS:(b + 1) * TOK_ROWS, vb * LANES:(vb + 1) * LANES]
    _to_chunk_layout(z_scr, xs_scr)
    for g in range(S5_GROUPS):
        xs_ref[g] = xs_scr[:, g * CHUNK_COLS:(g + 1) * CHUNK_COLS].astype(BF16)


def _in_proj(x, sh, sc, g, w_in, with_conv):
    bsz, n, _ = x.shape
    assert bsz == BATCH and n % TOK_ROWS == 0
    steps = n // TOK_ROWS
    chunk_rows = STEP_CHUNKS * BATCH
    n_cols = IN_COLS if with_conv else S5_WIDTH
    modv = pl.BlockSpec((BATCH, 1, D_MODEL), lambda i: (0, 0, 0))
    out_specs = [pl.BlockSpec((S5_GROUPS, chunk_rows, CHUNK_COLS), lambda i: (0, i, 0))]
    out_shape = [jax.ShapeDtypeStruct((S5_GROUPS, steps * chunk_rows, CHUNK_COLS), BF16)]
    if with_conv:
        out_specs.append(pl.BlockSpec((BATCH, TOK_ROWS, CONV_WIDTH), lambda i: (0, i, 0)))
        out_shape.append(jax.ShapeDtypeStruct((bsz, n, CONV_WIDTH), BF16))
    return pl.pallas_call(
        functools.partial(_inproj_kernel, with_conv),
        grid=(steps,),
        in_specs=[
            pl.BlockSpec((BATCH, TOK_ROWS, D_MODEL), lambda i: (0, i, 0)),
            modv, modv,
            pl.BlockSpec((1, D_MODEL), lambda i: (0, 0)),
            pl.BlockSpec((D_MODEL, n_cols), lambda i: (0, 0), pipeline_mode=pl.Buffered(1)),
        ],
        out_specs=out_specs,
        out_shape=out_shape,
        scratch_shapes=[
            pltpu.VMEM((LANE_BLOCKS, BATCH * PITCH, LANES), F32),
            pltpu.VMEM((chunk_rows, S5_GROUPS * CHUNK_COLS), F32),
            pltpu.VMEM((D_MODEL, n_cols), BF16),
        ],
        compiler_params=pltpu.CompilerParams(
            dimension_semantics=("arbitrary",), vmem_limit_bytes=VMEM_LIMIT),
        name="in_proj" if with_conv else "in_proj_ctx",
    )(x, sh, sc, g, w_in)


SCAN_GROUPS = 4


def _s5_kernel(n_lat, n_ctx, xl_ref, xc_ref, t_ref, m_ref, r_ref, a_ref, y_ref, e_ref, p_ref, ec_ref):
    half = 2 * S5_STATE
    groups = range(SCAN_GROUPS)
    for g in groups:
        ec_ref[g] = _dot(xc_ref[g], m_ref[g])
        e_ref[g] = _dot(xl_ref[g], m_ref[g])
    a_re = [jnp.broadcast_to(a_ref[g, 0:1, :], (8, half)) for g in groups]
    a_im = [jnp.broadcast_to(a_ref[g, 1:2, :], (8, half)) for g in groups]
    is_fwd = lax.broadcasted_iota(jnp.int32, (8, half), 1) < S5_STATE

    def advance(src_ref, g, i, j, s_re, s_im):
        ef = src_ref[g, pl.ds(pl.multiple_of(i * 8, 8), 8), :]
        eb = src_ref[g, pl.ds(pl.multiple_of(j * 8, 8), 8), :]
        e_re = jnp.where(is_fwd, ef[:, :half], eb[:, :half])
        e_im = jnp.where(is_fwd, ef[:, half:], eb[:, half:])
        return (a_re[g] * s_re - a_im[g] * s_im + e_re, a_re[g] * s_im + a_im[g] * s_re + e_im)

    def ctx_body(i, s):
        return tuple(advance(ec_ref, g, i, n_ctx - 1 - i, *s[g]) for g in groups)

    zero = jnp.zeros((8, half), F32)
    s = lax.fori_loop(0, n_ctx, ctx_body, tuple((zero, zero) for _ in groups))

    def lat_body(i, s):
        j = n_lat - 1 - i
        ri = pl.ds(pl.multiple_of(i * 8, 8), 8)
        rj = pl.ds(pl.multiple_of(j * 8, 8), 8)
        out = []
        for g in groups:
            s_re, s_im = s[g]
            p_ref[g, ri, 0:S5_STATE] = s_re[:, :S5_STATE]
            p_ref[g, rj, S5_STATE:half] = s_re[:, S5_STATE:]
            p_ref[g, ri, half:half + S5_STATE] = s_im[:, :S5_STATE]
            p_ref[g, rj, half + S5_STATE:] = s_im[:, S5_STATE:]
            out.append(advance(e_ref, g, i, j, s_re, s_im))
        return tuple(out)

    lax.fori_loop(0, n_lat, lat_body, s)
    for g in groups:
        y = _dot(xl_ref[g], t_ref[g]) + _dot(p_ref[g].astype(BF16), r_ref[g])
        y_ref[g] = y.astype(BF16)


def _s5_scan(x_lat, x_ctx, t_mat, m_mat, r_mat, a_pow, bsz):
    rows = x_lat.shape[1]
    rows_c = x_ctx.shape[1]
    gs = SCAN_GROUPS

    def per_group(*shape):
        return pl.BlockSpec((gs,) + shape, lambda i: (i,) + (0,) * len(shape))

    mat_spec = per_group(CHUNK_COLS, CHUNK_COLS)
    return pl.pallas_call(
        functools.partial(_s5_kernel, rows // bsz, rows_c // bsz),
        grid=(S5_GROUPS // gs,),
        in_specs=[
            per_group(rows, CHUNK_COLS), per_group(rows_c, CHUNK_COLS),
            mat_spec, mat_spec, mat_spec,
            per_group(2, 2 * S5_STATE),
        ],
        out_specs=per_group(rows, CHUNK_COLS),
        out_shape=jax.ShapeDtypeStruct((S5_GROUPS, rows, CHUNK_COLS), BF16),
        scratch_shapes=[
            pltpu.VMEM((gs, rows, CHUNK_COLS), F32),
            pltpu.VMEM((gs, rows, CHUNK_COLS), F32),
            pltpu.VMEM((gs, rows_c, CHUNK_COLS), F32),
        ],
        compiler_params=pltpu.CompilerParams(
            dimension_semantics=("parallel",), vmem_limit_bytes=VMEM_LIMIT),
        name="s5_scan",
    )(x_lat, x_ctx, t_mat, m_mat, r_mat, a_pow)


GROUPS_PER_STEP = SLOTS


def _cpow(kk, log_mag, ang):
    mag = jnp.exp(kk * log_mag)
    return mag * jnp.cos(kk * ang), mag * jnp.sin(kk * ang)


def _repeat_rows(a, reps):
    return jnp.concatenate(
        [jnp.broadcast_to(a[r:r + 1, :], (reps, a.shape[1])) for r in range(a.shape[0])], axis=0)


def _tile_rows(a, reps):
    return jnp.concatenate([a] * reps, axis=0)


def _dot3_nt(a, b):
    dn = (((1,), (1,)), ((), ()))
    a_hi = a.astype(BF16)
    a_lo = (a - a_hi.astype(F32)).astype(BF16)
    b_hi = b.astype(BF16)
    b_lo = (b - b_hi.astype(F32)).astype(BF16)

    def nt(u, v):
        return lax.dot_general(u, v, dn, preferred_element_type=F32)

    return nt(a_hi, b_hi) + nt(a_lo, b_hi) + nt(a_hi, b_lo)


def _chunk_row_order(a, s):
    blocks = []
    for half in range(2):
        for slot in range(SLOTS):
            q = 8 * half + (slot - s) % SLOTS
            blocks.append(a[q * S5_GROUP:(q + 1) * S5_GROUP, :])
    return jnp.concatenate(blocks, axis=0)


def _s5_prep_kernel(lre_ref, lim_ref, ldt_ref, btr_ref, bti_ref, ctr_ref, cti_ref, dw_ref,
                    t_ref, m_ref, r_ref, a_ref):
    half_l = 2 * S5_STATE
    n_lag = 2 * CHUNK - 1
    is_fwd16 = lax.broadcasted_iota(jnp.int32, (CHUNK, half_l), 1) < S5_STATE
    is_fwd32 = lax.broadcasted_iota(jnp.int32, (2 * CHUNK, half_l), 1) < S5_STATE
    row16 = lax.broadcasted_iota(jnp.int32, (CHUNK, half_l), 0).astype(F32)
    row32 = lax.broadcasted_iota(jnp.int32, (2 * CHUNK, half_l), 0)
    slot_of_lane = lax.broadcasted_iota(jnp.int32, (CHUNK_COLS, LANES), 1) // S5_GROUP
    eye = (lax.broadcasted_iota(jnp.int32, (CHUNK_COLS, CHUNK_COLS), 0)
           == lax.broadcasted_iota(jnp.int32, (CHUNK_COLS, CHUNK_COLS), 1))

    for s in range(GROUPS_PER_STEP):
        lre, lim = lre_ref[s], lim_ref[s]
        dt = jnp.exp(ldt_ref[s])
        log_mag, ang = lre * dt, lim * dt
        a1_re, a1_im = _cpow(jnp.ones((1, half_l), F32), log_mag, ang)
        den = lre * lre + lim * lim
        num_re = a1_re - 1.0
        f_re = (num_re * lre + a1_im * lim) / den
        f_im = (a1_im * lre - num_re * lim) / den
        bt_re, bt_im = btr_ref[s], bti_ref[s]
        bb_re = f_re * bt_re - f_im * bt_im
        bb_im = f_re * bt_im + f_im * bt_re
        ct_re, ct_im = ctr_ref[s], cti_ref[s]

        lag = jnp.abs(row32 - (CHUNK - 1))
        aj_re, aj_im = _cpow(lag.astype(F32), log_mag, ang)
        aj_re = jnp.where(is_fwd32, jnp.where(row32 <= CHUNK - 1, aj_re, 0.0),
                          jnp.where(row32 >= CHUNK - 1, aj_re, 0.0))[:n_lag]
        aj_im = jnp.where(is_fwd32, jnp.where(row32 <= CHUNK - 1, aj_im, 0.0),
                          jnp.where(row32 >= CHUNK - 1, aj_im, 0.0))[:n_lag]
        aj_re, aj_im = _repeat_rows(aj_re, S5_GROUP), _repeat_rows(aj_im, S5_GROUP)
        bl_re, bl_im = _tile_rows(bb_re, n_lag), _tile_rows(bb_im, n_lag)
        l_re = aj_re * bl_re - aj_im * bl_im
        l_im = aj_re * bl_im + aj_im * bl_re
        cw_re, cw_im = _tile_rows(ct_re, CHUNK), _tile_rows(ct_im, CHUNK)
        kwide = _dot3_nt(jnp.concatenate([l_re, -l_im], axis=1),
                         jnp.concatenate([cw_re, cw_im], axis=1))
        halves = []
        for half in range(2):
            col = kwide[:, half * LANES:(half + 1) * LANES]
            out = None
            for slot in range(SLOTS):
                q = 8 * half + (slot - s) % SLOTS
                start = (CHUNK - 1 - q) * S5_GROUP
                win = col[start:start + CHUNK_COLS, :]
                out = win if out is None else jnp.where(slot_of_lane == slot, win, out)
            halves.append(out)
        t_nat = jnp.concatenate(halves, axis=1)
        t_mat = _chunk_row_order(t_nat, s) + jnp.where(eye, dw_ref[s], 0.0)
        t_ref[s] = t_mat.astype(BF16)

        am_re, am_im = _cpow(jnp.where(is_fwd16, CHUNK - 1.0 - row16, row16), log_mag, ang)
        am_re, am_im = _repeat_rows(am_re, S5_GROUP), _repeat_rows(am_im, S5_GROUP)
        bm_re, bm_im = _tile_rows(bb_re, CHUNK), _tile_rows(bb_im, CHUNK)
        m_nat = jnp.concatenate([am_re * bm_re - am_im * bm_im, am_re * bm_im + am_im * bm_re], axis=1)
        m_ref[s] = _chunk_row_order(m_nat, s).astype(BF16)

        ar_re, ar_im = _cpow(jnp.where(is_fwd16, row16 + 1.0, CHUNK - row16), log_mag, ang)
        ar_re, ar_im = _repeat_rows(ar_re, S5_GROUP), _repeat_rows(ar_im, S5_GROUP)
        rt_nat = jnp.concatenate([ar_re * cw_re - ar_im * cw_im, -(ar_re * cw_im + ar_im * cw_re)], axis=1)
        r_ref[s] = _chunk_row_order(rt_nat, s).T.astype(BF16)

        ap_re, ap_im = _cpow(jnp.full((1, half_l), float(CHUNK), F32), log_mag, ang)
        a_ref[s] = jnp.concatenate([ap_re, ap_im], axis=0)


def _s5_operators(lam_re, lam_im, log_dt, b_re, b_im, c_re, c_im, d_skip):
    half_l = 2 * S5_STATE

    def lanes(v):
        return jnp.transpose(v, (1, 0, 2)).reshape(S5_GROUPS, 1, half_l)

    ldt = lanes(jnp.broadcast_to(log_dt[..., None], lam_re.shape))
    bt_re = jnp.transpose(b_re, (1, 3, 0, 2)).reshape(S5_GROUPS, S5_GROUP, half_l)
    bt_im = jnp.transpose(b_im, (1, 3, 0, 2)).reshape(S5_GROUPS, S5_GROUP, half_l)
    ct_re = jnp.transpose(c_re, (1, 2, 0, 3)).reshape(S5_GROUPS, S5_GROUP, half_l)
    ct_im = jnp.transpose(c_im, (1, 2, 0, 3)).reshape(S5_GROUPS, S5_GROUP, half_l)
    dw = jnp.tile(d_skip, (1, CHUNK)).reshape(S5_GROUPS, 1, CHUNK_COLS)

    gs = GROUPS_PER_STEP
    vec = pl.BlockSpec((gs, 1, half_l), lambda i: (i, 0, 0))
    par = pl.BlockSpec((gs, S5_GROUP, half_l), lambda i: (i, 0, 0))
    mat = pl.BlockSpec((gs, CHUNK_COLS, CHUNK_COLS), lambda i: (i, 0, 0))
    mat_shape = jax.ShapeDtypeStruct((S5_GROUPS, CHUNK_COLS, CHUNK_COLS), BF16)
    return pl.pallas_call(
        _s5_prep_kernel,
        grid=(S5_GROUPS // gs,),
        in_specs=[vec, vec, vec, par, par, par, par,
                  pl.BlockSpec((gs, 1, CHUNK_COLS), lambda i: (i, 0, 0))],
        out_specs=[mat, mat, mat, pl.BlockSpec((gs, 2, half_l), lambda i: (i, 0, 0))],
        out_shape=[mat_shape, mat_shape, mat_shape,
                   jax.ShapeDtypeStruct((S5_GROUPS, 2, half_l), F32)],
        compiler_params=pltpu.CompilerParams(
            dimension_semantics=("parallel",), vmem_limit_bytes=VMEM_LIMIT),
        name="s5_operators",
    )(lanes(lam_re), lanes(lam_im), ldt, bt_re, bt_im, ct_re, ct_im, dw)


CONV_HALF = CONV_K // 2
CONV_LAG = CONV_HALF + 1
RING_SLOTS = 32
CONV_BLOCKS = STEP_ROWS // CONV_ROWS
BF16_ROWS = 16
TAP_GROUP = 8
FF_TILE = 512


def _aligned(x, m):
    return x if isinstance(x, int) else pl.multiple_of(x, m)


def _conv_block(ring, slots, blk, wb_ref, cb_ref, lg_ref, lb_ref, seed=None):
    b = blk // (TOK_ROWS // CONV_ROWS)
    t0 = _aligned((blk % (TOK_ROWS // CONV_ROWS)) * CONV_ROWS, CONV_ROWS)
    packed = (CONV_ROWS // BF16_ROWS, BF16_ROWS, CONV_WIDTH)
    acc = jnp.zeros(packed, F32)
    if seed is not None:
        rows = jnp.concatenate([seed] * (BF16_ROWS // 8), axis=0)
        acc = acc + jnp.concatenate([rows] * (CONV_WIDTH // LANES), axis=1)[None]
    for k0 in range(0, CONV_K, TAP_GROUP):
        group = None
        for k in range(k0, min(k0 + TAP_GROUP, CONV_K)):
            prod = wb_ref[k] * ring[slots[k], b, pl.ds(t0, CONV_ROWS), :].reshape(packed)
            group = prod if group is None else group + prod
        acc = acc + group.astype(F32)
    acc = acc.reshape(CONV_ROWS, CONV_WIDTH) + cb_ref[...]
    mu = jnp.mean(acc, axis=-1, keepdims=True)
    dev = acc - mu
    var = jnp.mean(dev * dev, axis=-1, keepdims=True)
    y = dev * lax.rsqrt(var + EPS_LN) * lg_ref[...] + lb_ref[...]
    return (y * jax.nn.sigmoid(y)).astype(BF16)


def _out_kernel(n_rows, x_ref, ys_ref, hc_ref, g1_ref, sh2_ref, sc2_ref, g2_ref, n2_ref, fg_ref,
                wb_ref, cb_ref, lg_ref, lb_ref, wglu_in, wout_in, w1_in, w2_in, o_ref,
                u_scr, ring, yc_scr, wglu_ref, wout_ref, w1_ref, w2_ref):
    s = pl.program_id(0)
    zero_row = jnp.zeros((BATCH, TOK_ROWS, CONV_WIDTH), BF16)

    @pl.when(s < CONV_LAG)
    def _():
        for w_in_ref, w_bf in ((wglu_in, wglu_ref), (wout_in, wout_ref), (w1_in, w1_ref), (w2_in, w2_ref)):
            rows = w_in_ref.shape[0]
            w_bf[pl.ds(pl.multiple_of(s * rows, rows), rows), :] = w_in_ref[...].astype(BF16)

    @pl.when(s == 0)
    def _():
        for i in range(RING_SLOTS - CONV_HALF, RING_SLOTS):
            ring[i] = zero_row

    @pl.when(s < n_rows)
    def _():
        ring[s % RING_SLOTS] = hc_ref[...]

    @pl.when(s >= n_rows)
    def _():
        ring[s % RING_SLOTS] = zero_row

    slots = [(s + (RING_SLOTS - 2 * CONV_HALF) + k) % RING_SLOTS for k in range(CONV_K)]
    conv_args = (wb_ref, cb_ref, lg_ref, lb_ref)

    @pl.when(s == CONV_HALF)
    def _():
        def body(blk, carry):
            rows = pl.ds(_aligned(blk * CONV_ROWS, CONV_ROWS), CONV_ROWS)
            yc_scr[rows, :] = _conv_block(ring, slots, blk, *conv_args)
            return carry
        lax.fori_loop(0, CONV_BLOCKS, body, 0, unroll=4)

    @pl.when(s >= CONV_LAG)
    def _():
        yc = yc_scr[...]
        ys = jnp.concatenate([ys_ref[g].astype(F32) for g in range(S5_GROUPS)], axis=1)
        _from_chunk_layout(ys, u_scr)
        y_s5 = jnp.concatenate(
            [jnp.concatenate([u_scr[vb, b * PITCH:b * PITCH + TOK_ROWS, :] for b in range(BATCH)], axis=0)
             for vb in range(LANE_BLOCKS)], axis=1)
        gl = jax.nn.gelu(y_s5)
        s5o = gl * jax.nn.sigmoid(_dot(gl.astype(BF16), wglu_ref[...]))
        mix = _dot(s5o.astype(BF16), wout_ref[:S5_WIDTH, :]) + _dot(yc, wout_ref[S5_WIDTH:, :])
        h1 = x_ref[...] + g1_ref[...] * mix.reshape(BATCH, TOK_ROWS, D_MODEL)
        a2 = _rms(h1, n2_ref[...]) * (1.0 + sc2_ref[...]) + sh2_ref[...]
        a2 = a2.reshape(STEP_ROWS, D_MODEL).astype(BF16)
        ff_tile = FF_TILE
        n_ff = D_FF // ff_tile
        per_dot = CONV_BLOCKS // (2 * n_ff)
        never = s < 0

        def seeded_conv(first_blk, mat):
            for i in range(per_dot):
                blk = first_blk + i
                r = i * (STEP_ROWS // per_dot)
                seed = jnp.where(never, mat[r:r + 8, 0:LANES], 0.0)
                yc_scr[blk * CONV_ROWS:(blk + 1) * CONV_ROWS, :] = _conv_block(
                    ring, slots, blk, *conv_args, seed=seed)

        acc = jnp.zeros((STEP_ROWS, D_MODEL), F32)
        for j in range(n_ff):
            up = _dot(a2, w1_ref[:, j * ff_tile:(j + 1) * ff_tile])
            seeded_conv(2 * j * per_dot, up)
            hid = jnp.maximum(up, 0.0)
            down = _dot((hid * hid).astype(BF16), w2_ref[j * ff_tile:(j + 1) * ff_tile, :])
            seeded_conv((2 * j + 1) * per_dot, down)
            acc = acc + down
        h2 = h1 + g2_ref[...] * acc.reshape(BATCH, TOK_ROWS, D_MODEL)
        o_ref[...] = _rms(h2, fg_ref[...])


def _out_block(x, ys, hc, g1, sh2, sc2, g2, n2, fg, wb, cb, lg, lb, wglu, wout, w1, w2):
    bsz, n, _ = x.shape
    assert bsz == BATCH and n % TOK_ROWS == 0
    n_rows = n // TOK_ROWS
    chunk_rows = STEP_CHUNKS * BATCH
    modv = pl.BlockSpec((BATCH, 1, D_MODEL), lambda s: (0, 0, 0))
    vec = pl.BlockSpec((1, D_MODEL), lambda s: (0, 0))
    cvec = pl.BlockSpec((1, CONV_WIDTH), lambda s: (0, 0))

    def const(shape):
        return pl.BlockSpec(shape, lambda s: (0,) * len(shape), pipeline_mode=pl.Buffered(1))

    def lagged(s):
        return jnp.maximum(s - CONV_LAG, 0)

    def staged(w):
        return pl.BlockSpec((w.shape[0] // CONV_LAG, w.shape[1]), lambda s: (jnp.minimum(s, CONV_LAG - 1), 0))

    return pl.pallas_call(
        functools.partial(_out_kernel, n_rows),
        grid=(n_rows + CONV_LAG,),
        in_specs=[
            pl.BlockSpec((BATCH, TOK_ROWS, D_MODEL), lambda s: (0, lagged(s), 0)),
            pl.BlockSpec((S5_GROUPS, chunk_rows, CHUNK_COLS), lambda s: (0, lagged(s), 0)),
            pl.BlockSpec((BATCH, TOK_ROWS, CONV_WIDTH), lambda s: (0, jnp.minimum(s, n_rows - 1), 0)),
            modv, modv, modv, modv, vec, vec,
            const((CONV_K, BF16_ROWS, CONV_WIDTH)), cvec, cvec, cvec,
            staged(wglu), staged(wout), staged(w1), staged(w2),
        ],
        out_specs=pl.BlockSpec((BATCH, TOK_ROWS, D_MODEL), lambda s: (0, lagged(s), 0)),
        out_shape=jax.ShapeDtypeStruct((bsz, n, D_MODEL), F32),
        scratch_shapes=[
            pltpu.VMEM((LANE_BLOCKS, BATCH * PITCH, LANES), F32),
            pltpu.VMEM((RING_SLOTS, BATCH, TOK_ROWS, CONV_WIDTH), BF16),
            pltpu.VMEM((STEP_ROWS, CONV_WIDTH), BF16),
            pltpu.VMEM(wglu.shape, BF16), pltpu.VMEM(wout.shape, BF16),
            pltpu.VMEM(w1.shape, BF16), pltpu.VMEM(w2.shape, BF16),
        ],
        compiler_params=pltpu.CompilerParams(
            dimension_semantics=("arbitrary",), vmem_limit_bytes=OUT_VMEM_LIMIT),
        name="out_block",
    )(x, ys, hc, g1, sh2, sc2, g2, n2, fg, wb, cb, lg, lb, wglu, wout, w1, w2)


def kernel(x, c, ctx, c_ctx, ada_w, ada_b, norm1_g, w_in, s5_lam_re, s5_lam_im, s5_log_dt, s5_b_re, s5_b_im, s5_c_re, s5_c_im, s5_d, s5_w_glu, conv_w, conv_b, conv_ln_g, conv_ln_b, w_out, norm2_g, mlp_w1, mlp_w2, final_g):
    bsz = x.shape[0]
    assert bsz == BATCH and bsz + 1 <= MOD_ROWS
    layer = 0

    cc = jnp.concatenate([c, c_ctx[None, :], jnp.zeros((MOD_ROWS - bsz - 1, D_MODEL), F32)], axis=0)
    mod = _mod_table(cc, ada_w[layer], ada_b[layer])
    sh1, sc1, g1, sh2, sc2, g2 = [m.reshape(bsz, 1, D_MODEL) for m in jnp.split(mod[:bsz], 6, axis=-1)]
    csh1 = jnp.broadcast_to(mod[bsz, :D_MODEL].reshape(1, 1, D_MODEL), (bsz, 1, D_MODEL))
    csc1 = jnp.broadcast_to(mod[bsz, D_MODEL:2 * D_MODEL].reshape(1, 1, D_MODEL), (bsz, 1, D_MODEL))

    n1 = norm1_g[layer].reshape(1, D_MODEL)
    x_lat, hc = _in_proj(x, sh1, sc1, n1, w_in[layer], True)
    x_ctx, = _in_proj(ctx, csh1, csc1, n1, w_in[layer], False)

    t_mat, m_mat, r_mat, a_pow = _s5_operators(
        s5_lam_re[layer], s5_lam_im[layer], s5_log_dt[layer], s5_b_re[layer], s5_b_im[layer],
        s5_c_re[layer], s5_c_im[layer], s5_d[layer])
    y_s5 = _s5_scan(x_lat, x_ctx, t_mat, m_mat, r_mat, a_pow, bsz)

    conv_wb = jnp.broadcast_to(conv_w[layer].astype(BF16)[:, None, :], (CONV_K, BF16_ROWS, CONV_WIDTH))
    return _out_block(
        x, y_s5, hc, g1, sh2, sc2, g2,
        norm2_g[layer].reshape(1, D_MODEL), final_g.reshape(1, D_MODEL),
        conv_wb, conv_b[layer].reshape(1, CONV_WIDTH),
        conv_ln_g[layer].reshape(1, CONV_WIDTH), conv_ln_b[layer].reshape(1, CONV_WIDTH),
        s5_w_glu[layer], w_out[layer], mlp_w1[layer], mlp_w2[layer])
```

```python
import functools

import jax
import jax.numpy as jnp
from jax import lax
from jax.experimental import pallas as pl
from jax.experimental.pallas import tpu as pltpu

F32 = jnp.float32
BF16 = jnp.bfloat16

D_MODEL = 1024
S5_WIDTH = 512
S5_GROUP = 16
S5_GROUPS = 32
S5_STATE = 64
CONV_WIDTH = 512
CONV_K = 31
GRID_W = 64
IN_COLS = S5_WIDTH + 2 * CONV_WIDTH
D_FF = 4 * D_MODEL
EPS_RMS = 1e-6
EPS_LN = 1e-5

CHUNK = 16
CHUNK_COLS = CHUNK * S5_GROUP
MOD_ROWS = 16
BATCH = 8
TOK_ROWS = 64
STEP_ROWS = BATCH * TOK_ROWS
STEP_CHUNKS = TOK_ROWS // CHUNK
LANES = 128
SLOTS = LANES // S5_GROUP
LANE_BLOCKS = S5_WIDTH // LANES
PITCH = 72
CONV_ROWS = 16
VMEM_LIMIT = 56 * 1024 * 1024
OUT_VMEM_LIMIT = 60 * 1024 * 1024


def _dot(a, b):
    return jnp.dot(a, b, preferred_element_type=F32)


def _rms(x, g):
    return x * lax.rsqrt(jnp.mean(x * x, axis=-1, keepdims=True) + EPS_RMS) * g


def _mod_kernel(c_ref, w_ref, b_ref, o_ref):
    c = c_ref[...]
    s = c * jax.nn.sigmoid(c)
    w = w_ref[...]
    s_hi = s.astype(BF16)
    s_lo = (s - s_hi.astype(F32)).astype(BF16)
    w_hi = w.astype(BF16)
    w_lo = (w - w_hi.astype(F32)).astype(BF16)
    o_ref[...] = _dot(s_hi, w_hi) + _dot(s_lo, w_hi) + _dot(s_hi, w_lo) + b_ref[...]


def _mod_table(cc, ada_w, ada_b):
    n_out = ada_w.shape[1]
    tn = 1536
    return pl.pallas_call(
        _mod_kernel,
        grid=(n_out // tn,),
        in_specs=[
            pl.BlockSpec((MOD_ROWS, D_MODEL), lambda j: (0, 0)),
            pl.BlockSpec((D_MODEL, tn), lambda j: (0, j)),
            pl.BlockSpec((1, tn), lambda j: (0, j)),
        ],
        out_specs=pl.BlockSpec((MOD_ROWS, tn), lambda j: (0, j)),
        out_shape=jax.ShapeDtypeStruct((MOD_ROWS, n_out), F32),
        compiler_params=pltpu.CompilerParams(vmem_limit_bytes=VMEM_LIMIT),
        name="mod_table",
    )(cc, ada_w, ada_b.reshape(1, n_out))


def _slot_skew(vregs):
    slot = lax.broadcasted_iota(jnp.int32, (BATCH, LANES), 1) // S5_GROUP
    cur = list(vregs)
    for b in range(3):
        bit_set = (slot & (1 << b)) != 0
        cur = [jnp.where(bit_set, cur[(t + (1 << b)) % SLOTS], cur[t]) for t in range(SLOTS)]
    return cur


def _to_chunk_layout(z_scr, xs_scr):
    for cl in range(STEP_CHUNKS):
        for vb in range(LANE_BLOCKS):
            for half in range(2):
                rot = []
                for ql in range(SLOTS):
                    piece = z_scr[vb, pl.ds(cl * CHUNK + 8 * half + ql, BATCH, stride=PITCH), :]
                    rot.append(pltpu.roll(piece, ql * S5_GROUP, axis=1) if ql else piece)
                skew = _slot_skew(rot)
                for s in range(SLOTS):
                    col = (2 * (SLOTS * vb + s) + half) * LANES
                    xs_scr[cl * BATCH:(cl + 1) * BATCH, col:col + LANES] = skew[(-s) % SLOTS]


def _from_chunk_layout(ys, u_scr):
    for cl in range(STEP_CHUNKS):
        for vb in range(LANE_BLOCKS):
            for half in range(2):
                src = []
                for s in range(SLOTS):
                    col = (2 * (SLOTS * vb + s) + half) * LANES
                    src.append(ys[cl * BATCH:(cl + 1) * BATCH, col:col + LANES])
                skew = _slot_skew(src)
                for ql in range(SLOTS):
                    out = skew[(-ql) % SLOTS]
                    if ql:
                        out = pltpu.roll(out, LANES - ql * S5_GROUP, axis=1)
                    u_scr[vb, pl.ds(cl * CHUNK + 8 * half + ql, BATCH, stride=PITCH), :] = out


def _inproj_kernel(with_conv, x_ref, sh_ref, sc_ref, g_ref, w_ref, xs_ref, *rest):
    hc_ref = rest[0] if with_conv else None
    z_scr, xs_scr, w_bf = rest[-3:]

    @pl.when(pl.program_id(0) == 0)
    def _():
        w_bf[...] = w_ref[...].astype(BF16)

    x = x_ref[...]
    a = _rms(x, g_ref[...]) * (1.0 + sc_ref[...]) + sh_ref[...]
    z = _dot(a.reshape(STEP_ROWS, D_MODEL).astype(BF16), w_bf[...])
    if with_conv:
        v = z[:, S5_WIDTH:S5_WIDTH + CONV_WIDTH]
        gate = z[:, S5_WIDTH + CONV_WIDTH:]
        hc_ref[...] = (v * jax.nn.sigmoid(gate)).astype(BF16).reshape(BATCH, TOK_ROWS, CONV_WIDTH)
    for vb in range(LANE_BLOCKS):
        for b in range(BATCH):
            z_scr[vb, b * PITCH:b * PITCH + TOK_ROWS, :] = z[b * TOK_ROWS:(b + 1) * TOK_ROWS, vb * LANES:(vb + 1) * LANES]
    _to_chunk_layout(z_scr, xs_scr)
    for g in range(S5_GROUPS):
        xs_ref[g] = xs_scr[:, g * CHUNK_COLS:(g + 1) * CHUNK_COLS].astype(BF16)


def _in_proj(x, sh, sc, g, w_in, with_conv):
    bsz, n, _ = x.shape
    assert bsz == BATCH and n % TOK_ROWS == 0
    steps = n // TOK_ROWS
    chunk_rows = STEP_CHUNKS * BATCH
    n_cols = IN_COLS if with_conv else S5_WIDTH
    modv = pl.BlockSpec((BATCH, 1, D_MODEL), lambda i: (0, 0, 0))
    out_specs = [pl.BlockSpec((S5_GROUPS, chunk_rows, CHUNK_COLS), lambda i: (0, i, 0))]
    out_shape = [jax.ShapeDtypeStruct((S5_GROUPS, steps * chunk_rows, CHUNK_COLS), BF16)]
    if with_conv:
        out_specs.append(pl.BlockSpec((BATCH, TOK_ROWS, CONV_WIDTH), lambda i: (0, i, 0)))
        out_shape.append(jax.ShapeDtypeStruct((bsz, n, CONV_WIDTH), BF16))
    return pl.pallas_call(
        functools.partial(_inproj_kernel, with_conv),
        grid=(steps,),
        in_specs=[
            pl.BlockSpec((BATCH, TOK_ROWS, D_MODEL), lambda i: (0, i, 0)),
            modv, modv,
            pl.BlockSpec((1, D_MODEL), lambda i: (0, 0)),
            pl.BlockSpec((D_MODEL, n_cols), lambda i: (0, 0), pipeline_mode=pl.Buffered(1)),
        ],
        out_specs=out_specs,
        out_shape=out_shape,
        scratch_shapes=[
            pltpu.VMEM((LANE_BLOCKS, BATCH * PITCH, LANES), F32),
            pltpu.VMEM((chunk_rows, S5_GROUPS * CHUNK_COLS), F32),
            pltpu.VMEM((D_MODEL, n_cols), BF16),
        ],
        compiler_params=pltpu.CompilerParams(
            dimension_semantics=("arbitrary",), vmem_limit_bytes=VMEM_LIMIT),
        name="in_proj" if with_conv else "in_proj_ctx",
    )(x, sh, sc, g, w_in)


SCAN_GROUPS = 4


def _mirror_bwd(a, n_chunks):
    rev = jnp.concatenate([a[(n_chunks - 1 - c) * 8:(n_chunks - c) * 8, :] for c in range(n_chunks)], axis=0)
    lane = lax.broadcasted_iota(jnp.int32, a.shape, 1)
    return jnp.where((lane % (2 * S5_STATE)) < S5_STATE, a, rev)


def _s5_kernel(n_lat, n_ctx, xl_ref, xc_ref, t_ref, m_ref, r_ref, a_ref, y_ref, e_ref, p_ref, ec_ref):
    half = 2 * S5_STATE
    groups = range(SCAN_GROUPS)
    for g in groups:
        ec_ref[g] = _mirror_bwd(_dot(xc_ref[g], m_ref[g]), n_ctx)
        e_ref[g] = _mirror_bwd(_dot(xl_ref[g], m_ref[g]), n_lat)
    a_re = [jnp.broadcast_to(a_ref[g, 0:1, :], (8, half)) for g in groups]
    a_im = [jnp.broadcast_to(a_ref[g, 1:2, :], (8, half)) for g in groups]

    def advance(src_ref, g, rows, s_re, s_im):
        e = src_ref[g, rows, :]
        return (a_re[g] * s_re - a_im[g] * s_im + e[:, :half], a_re[g] * s_im + a_im[g] * s_re + e[:, half:])

    def ctx_body(i, s):
        rows = pl.ds(pl.multiple_of(i * 8, 8), 8)
        return tuple(advance(ec_ref, g, rows, *s[g]) for g in groups)

    zero = jnp.zeros((8, half), F32)
    s = lax.fori_loop(0, n_ctx, ctx_body, tuple((zero, zero) for _ in groups))

    def lat_body(i, s):
        rows = pl.ds(pl.multiple_of(i * 8, 8), 8)
        out = []
        for g in groups:
            s_re, s_im = s[g]
            p_ref[g, rows, :half] = s_re
            p_ref[g, rows, half:] = s_im
            out.append(advance(e_ref, g, rows, s_re, s_im))
        return tuple(out)

    lax.fori_loop(0, n_lat, lat_body, s)
    for g in groups:
        p = _mirror_bwd(p_ref[g], n_lat)
        y = _dot(xl_ref[g], t_ref[g]) + _dot(p.astype(BF16), r_ref[g])
        y_ref[g] = y.astype(BF16)


def _s5_scan(x_lat, x_ctx, t_mat, m_mat, r_mat, a_pow, bsz):
    rows = x_lat.shape[1]
    rows_c = x_ctx.shape[1]
    gs = SCAN_GROUPS

    def per_group(*shape):
        return pl.BlockSpec((gs,) + shape, lambda i: (i,) + (0,) * len(shape))

    mat_spec = per_group(CHUNK_COLS, CHUNK_COLS)
    return pl.pallas_call(
        functools.partial(_s5_kernel, rows // bsz, rows_c // bsz),
        grid=(S5_GROUPS // gs,),
        in_specs=[
            per_group(rows, CHUNK_COLS), per_group(rows_c, CHUNK_COLS),
            mat_spec, mat_spec, mat_spec,
            per_group(2, 2 * S5_STATE),
        ],
        out_specs=per_group(rows, CHUNK_COLS),
        out_shape=jax.ShapeDtypeStruct((S5_GROUPS, rows, CHUNK_COLS), BF16),
        scratch_shapes=[
            pltpu.VMEM((gs, rows, CHUNK_COLS), F32),
            pltpu.VMEM((gs, rows, CHUNK_COLS), F32),
            pltpu.VMEM((gs, rows_c, CHUNK_COLS), F32),
        ],
        compiler_params=pltpu.CompilerParams(
            dimension_semantics=("parallel",), vmem_limit_bytes=VMEM_LIMIT),
        name="s5_scan",
    )(x_lat, x_ctx, t_mat, m_mat, r_mat, a_pow)


GROUPS_PER_STEP = SLOTS


def _cpow(kk, log_mag, ang):
    mag = jnp.exp(kk * log_mag)
    return mag * jnp.cos(kk * ang), mag * jnp.sin(kk * ang)


def _repeat_rows(a, reps):
    return jnp.concatenate(
        [jnp.broadcast_to(a[r:r + 1, :], (reps, a.shape[1])) for r in range(a.shape[0])], axis=0)


def _tile_rows(a, reps):
    return jnp.concatenate([a] * reps, axis=0)


def _dot3_nt(a, b):
    dn = (((1,), (1,)), ((), ()))
    a_hi = a.astype(BF16)
    a_lo = (a - a_hi.astype(F32)).astype(BF16)
    b_hi = b.astype(BF16)
    b_lo = (b - b_hi.astype(F32)).astype(BF16)

    def nt(u, v):
        return lax.dot_general(u, v, dn, preferred_element_type=F32)

    return nt(a_hi, b_hi) + nt(a_lo, b_hi) + nt(a_hi, b_lo)


def _chunk_row_order(a, s):
    blocks = []
    for half in range(2):
        for slot in range(SLOTS):
            q = 8 * half + (slot - s) % SLOTS
            blocks.append(a[q * S5_GROUP:(q + 1) * S5_GROUP, :])
    return jnp.concatenate(blocks, axis=0)


def _s5_prep_kernel(lre_ref, lim_ref, ldt_ref, btr_ref, bti_ref, ctr_ref, cti_ref, dw_ref,
                    t_ref, m_ref, r_ref, a_ref):
    half_l = 2 * S5_STATE
    n_lag = 2 * CHUNK - 1
    is_fwd16 = lax.broadcasted_iota(jnp.int32, (CHUNK, half_l), 1) < S5_STATE
    is_fwd32 = lax.broadcasted_iota(jnp.int32, (2 * CHUNK, half_l), 1) < S5_STATE
    row16 = lax.broadcasted_iota(jnp.int32, (CHUNK, half_l), 0).astype(F32)
    row32 = lax.broadcasted_iota(jnp.int32, (2 * CHUNK, half_l), 0)
    slot_of_lane = lax.broadcasted_iota(jnp.int32, (CHUNK_COLS, LANES), 1) // S5_GROUP
    eye = (lax.broadcasted_iota(jnp.int32, (CHUNK_COLS, CHUNK_COLS), 0)
           == lax.broadcasted_iota(jnp.int32, (CHUNK_COLS, CHUNK_COLS), 1))

    for s in range(GROUPS_PER_STEP):
        lre, lim = lre_ref[s], lim_ref[s]
        dt = jnp.exp(ldt_ref[s])
        log_mag, ang = lre * dt, lim * dt
        a1_re, a1_im = _cpow(jnp.ones((1, half_l), F32), log_mag, ang)
        den = lre * lre + lim * lim
        num_re = a1_re - 1.0
        f_re = (num_re * lre + a1_im * lim) / den
        f_im = (a1_im * lre - num_re * lim) / den
        bt_re, bt_im = btr_ref[s], bti_ref[s]
        bb_re = f_re * bt_re - f_im * bt_im
        bb_im = f_re * bt_im + f_im * bt_re
        ct_re, ct_im = ctr_ref[s], cti_ref[s]

        lag = jnp.abs(row32 - (CHUNK - 1))
        aj_re, aj_im = _cpow(lag.astype(F32), log_mag, ang)
        aj_re = jnp.where(is_fwd32, jnp.where(row32 <= CHUNK - 1, aj_re, 0.0),
                          jnp.where(row32 >= CHUNK - 1, aj_re, 0.0))[:n_lag]
        aj_im = jnp.where(is_fwd32, jnp.where(row32 <= CHUNK - 1, aj_im, 0.0),
                          jnp.where(row32 >= CHUNK - 1, aj_im, 0.0))[:n_lag]
        aj_re, aj_im = _repeat_rows(aj_re, S5_GROUP), _repeat_rows(aj_im, S5_GROUP)
        bl_re, bl_im = _tile_rows(bb_re, n_lag), _tile_rows(bb_im, n_lag)
        l_re = aj_re * bl_re - aj_im * bl_im
        l_im = aj_re * bl_im + aj_im * bl_re
        cw_re, cw_im = _tile_rows(ct_re, CHUNK), _tile_rows(ct_im, CHUNK)
        kwide = _dot3_nt(jnp.concatenate([l_re, -l_im], axis=1),
                         jnp.concatenate([cw_re, cw_im], axis=1))
        halves = []
        for half in range(2):
            col = kwide[:, half * LANES:(half + 1) * LANES]
            out = None
            for slot in range(SLOTS):
                q = 8 * half + (slot - s) % SLOTS
                start = (CHUNK - 1 - q) * S5_GROUP
                win = col[start:start + CHUNK_COLS, :]
                out = win if out is None else jnp.where(slot_of_lane == slot, win, out)
            halves.append(out)
        t_nat = jnp.concatenate(halves, axis=1)
        t_mat = _chunk_row_order(t_nat, s) + jnp.where(eye, dw_ref[s], 0.0)
        t_ref[s] = t_mat.astype(BF16)

        am_re, am_im = _cpow(jnp.where(is_fwd16, CHUNK - 1.0 - row16, row16), log_mag, ang)
        am_re, am_im = _repeat_rows(am_re, S5_GROUP), _repeat_rows(am_im, S5_GROUP)
        bm_re, bm_im = _tile_rows(bb_re, CHUNK), _tile_rows(bb_im, CHUNK)
        m_nat = jnp.concatenate([am_re * bm_re - am_im * bm_im, am_re * bm_im + am_im * bm_re], axis=1)
        m_ref[s] = _chunk_row_order(m_nat, s).astype(BF16)

        ar_re, ar_im = _cpow(jnp.where(is_fwd16, row16 + 1.0, CHUNK - row16), log_mag, ang)
        ar_re, ar_im = _repeat_rows(ar_re, S5_GROUP), _repeat_rows(ar_im, S5_GROUP)
        rt_nat = jnp.concatenate([ar_re * cw_re - ar_im * cw_im, -(ar_re * cw_im + ar_im * cw_re)], axis=1)
        r_ref[s] = _chunk_row_order(rt_nat, s).T.astype(BF16)

        ap_re, ap_im = _cpow(jnp.full((1, half_l), float(CHUNK), F32), log_mag, ang)
        a_ref[s] = jnp.concatenate([ap_re, ap_im], axis=0)


def _s5_operators(lam_re, lam_im, log_dt, b_re, b_im, c_re, c_im, d_skip):
    half_l = 2 * S5_STATE

    def lanes(v):
        return jnp.transpose(v, (1, 0, 2)).reshape(S5_GROUPS, 1, half_l)

    ldt = lanes(jnp.broadcast_to(log_dt[..., None], lam_re.shape))
    bt_re = jnp.transpose(b_re, (1, 3, 0, 2)).reshape(S5_GROUPS, S5_GROUP, half_l)
    bt_im = jnp.transpose(b_im, (1, 3, 0, 2)).reshape(S5_GROUPS, S5_GROUP, half_l)
    ct_re = jnp.transpose(c_re, (1, 2, 0, 3)).reshape(S5_GROUPS, S5_GROUP, half_l)
    ct_im = jnp.transpose(c_im, (1, 2, 0, 3)).reshape(S5_GROUPS, S5_GROUP, half_l)
    dw = jnp.tile(d_skip, (1, CHUNK)).reshape(S5_GROUPS, 1, CHUNK_COLS)

    gs = GROUPS_PER_STEP
    vec = pl.BlockSpec((gs, 1, half_l), lambda i: (i, 0, 0))
    par = pl.BlockSpec((gs, S5_GROUP, half_l), lambda i: (i, 0, 0))
    mat = pl.BlockSpec((gs, CHUNK_COLS, CHUNK_COLS), lambda i: (i, 0, 0))
    mat_shape = jax.ShapeDtypeStruct((S5_GROUPS, CHUNK_COLS, CHUNK_COLS), BF16)
    return pl.pallas_call(
        _s5_prep_kernel,
        grid=(S5_GROUPS // gs,),
        in_specs=[vec, vec, vec, par, par, par, par,
                  pl.BlockSpec((gs, 1, CHUNK_COLS), lambda i: (i, 0, 0))],
        out_specs=[mat, mat, mat, pl.BlockSpec((gs, 2, half_l), lambda i: (i, 0, 0))],
        out_shape=[mat_shape, mat_shape, mat_shape,
                   jax.ShapeDtypeStruct((S5_GROUPS, 2, half_l), F32)],
        compiler_params=pltpu.CompilerParams(
            dimension_semantics=("parallel",), vmem_limit_bytes=VMEM_LIMIT),
        name="s5_operators",
    )(lanes(lam_re), lanes(lam_im), ldt, bt_re, bt_im, ct_re, ct_im, dw)


CONV_HALF = CONV_K // 2
CONV_LAG = CONV_HALF + 1
RING_SLOTS = 32
CONV_BLOCKS = STEP_ROWS // CONV_ROWS
BF16_ROWS = 16
TAP_GROUP = 8
FF_TILE = 512


def _aligned(x, m):
    return x if isinstance(x, int) else pl.multiple_of(x, m)


def _conv_block(ring, slots, blk, wb_ref, cb_ref, lg_ref, lb_ref, seed=None):
    b = blk // (TOK_ROWS // CONV_ROWS)
    t0 = _aligned((blk % (TOK_ROWS // CONV_ROWS)) * CONV_ROWS, CONV_ROWS)
    packed = (CONV_ROWS // BF16_ROWS, BF16_ROWS, CONV_WIDTH)
    acc = jnp.zeros(packed, F32)
    if seed is not None:
        rows = jnp.concatenate([seed] * (BF16_ROWS // 8), axis=0)
        acc = acc + jnp.concatenate([rows] * (CONV_WIDTH // LANES), axis=1)[None]
    for k0 in range(0, CONV_K, TAP_GROUP):
        group = None
        for k in range(k0, min(k0 + TAP_GROUP, CONV_K)):
            prod = wb_ref[k] * ring[slots[k], b, pl.ds(t0, CONV_ROWS), :].reshape(packed)
            group = prod if group is None else group + prod
        acc = acc + group.astype(F32)
    acc = acc.reshape(CONV_ROWS, CONV_WIDTH) + cb_ref[...]
    mu = jnp.mean(acc, axis=-1, keepdims=True)
    dev = acc - mu
    var = jnp.mean(dev * dev, axis=-1, keepdims=True)
    y = dev * lax.rsqrt(var + EPS_LN) * lg_ref[...] + lb_ref[...]
    return (y * jax.nn.sigmoid(y)).astype(BF16)


def _out_kernel(n_rows, x_ref, ys_ref, hc_ref, g1_ref, sh2_ref, sc2_ref, g2_ref, n2_ref, fg_ref,
                wb_ref, cb_ref, lg_ref, lb_ref, wglu_in, wout_in, w1_in, w2_in, o_ref,
                u_scr, ring, yc_scr, wglu_ref, wout_ref, w1_ref, w2_ref):
    s = pl.program_id(0)
    zero_row = jnp.zeros((BATCH, TOK_ROWS, CONV_WIDTH), BF16)

    @pl.when(s < CONV_LAG)
    def _():
        for w_in_ref, w_bf in ((wglu_in, wglu_ref), (wout_in, wout_ref), (w1_in, w1_ref), (w2_in, w2_ref)):
            rows = w_in_ref.shape[0]
            w_bf[pl.ds(pl.multiple_of(s * rows, rows), rows), :] = w_in_ref[...].astype(BF16)

    @pl.when(s == 0)
    def _():
        for i in range(RING_SLOTS - CONV_HALF, RING_SLOTS):
            ring[i] = zero_row

    @pl.when(s < n_rows)
    def _():
        ring[s % RING_SLOTS] = hc_ref[...]

    @pl.when(s >= n_rows)
    def _():
        ring[s % RING_SLOTS] = zero_row

    slots = [(s + (RING_SLOTS - 2 * CONV_HALF) + k) % RING_SLOTS for k in range(CONV_K)]
    conv_args = (wb_ref, cb_ref, lg_ref, lb_ref)

    @pl.when(s == CONV_HALF)
    def _():
        def body(blk, carry):
            rows = pl.ds(_aligned(blk * CONV_ROWS, CONV_ROWS), CONV_ROWS)
            yc_scr[rows, :] = _conv_block(ring, slots, blk, *conv_args)
            return carry
        lax.fori_loop(0, CONV_BLOCKS, body, 0, unroll=4)

    @pl.when(s >= CONV_LAG)
    def _():
        yc = yc_scr[...]
        ys = jnp.concatenate([ys_ref[g].astype(F32) for g in range(S5_GROUPS)], axis=1)
        _from_chunk_layout(ys, u_scr)
        y_s5 = jnp.concatenate(
            [jnp.concatenate([u_scr[vb, b * PITCH:b * PITCH + TOK_ROWS, :] for b in range(BATCH)], axis=0)
             for vb in range(LANE_BLOCKS)], axis=1)
        gl = jax.nn.gelu(y_s5)
        s5o = gl * jax.nn.sigmoid(_dot(gl.astype(BF16), wglu_ref[...]))
        mix = _dot(s5o.astype(BF16), wout_ref[:S5_WIDTH, :]) + _dot(yc, wout_ref[S5_WIDTH:, :])
        h1 = x_ref[...] + g1_ref[...] * mix.reshape(BATCH, TOK_ROWS, D_MODEL)
        a2 = _rms(h1, n2_ref[...]) * (1.0 + sc2_ref[...]) + sh2_ref[...]
        a2 = a2.reshape(STEP_ROWS, D_MODEL).astype(BF16)
        ff_tile = FF_TILE
        n_ff = D_FF // ff_tile
        per_dot = CONV_BLOCKS // (2 * n_ff)
        never = s < 0

        def seeded_conv(first_blk, mat):
            for i in range(per_dot):
                blk = first_blk + i
                r = i * (STEP_ROWS // per_dot)
                seed = jnp.where(never, mat[r:r + 8, 0:LANES], 0.0)
                yc_scr[blk * CONV_ROWS:(blk + 1) * CONV_ROWS, :] = _conv_block(
                    ring, slots, blk, *conv_args, seed=seed)

        acc = jnp.zeros((STEP_ROWS, D_MODEL), F32)
        for j in range(n_ff):
            up = _dot(a2, w1_ref[:, j * ff_tile:(j + 1) * ff_tile])
            seeded_conv(2 * j * per_dot, up)
            hid = jnp.maximum(up, 0.0)
            down = _dot((hid * hid).astype(BF16), w2_ref[j * ff_tile:(j + 1) * ff_tile, :])
            seeded_conv((2 * j + 1) * per_dot, down)
            acc = acc + down
        h2 = h1 + g2_ref[...] * acc.reshape(BATCH, TOK_ROWS, D_MODEL)
        o_ref[...] = _rms(h2, fg_ref[...])


def _out_block(x, ys, hc, g1, sh2, sc2, g2, n2, fg, wb, cb, lg, lb, wglu, wout, w1, w2):
    bsz, n, _ = x.shape
    assert bsz == BATCH and n % TOK_ROWS == 0
    n_rows = n // TOK_ROWS
    chunk_rows = STEP_CHUNKS * BATCH
    modv = pl.BlockSpec((BATCH, 1, D_MODEL), lambda s: (0, 0, 0))
    vec = pl.BlockSpec((1, D_MODEL), lambda s: (0, 0))
    cvec = pl.BlockSpec((1, CONV_WIDTH), lambda s: (0, 0))

    def const(shape):
        return pl.BlockSpec(shape, lambda s: (0,) * len(shape), pipeline_mode=pl.Buffered(1))

    def lagged(s):
        return jnp.maximum(s - CONV_LAG, 0)

    def staged(w):
        return pl.BlockSpec((w.shape[0] // CONV_LAG, w.shape[1]), lambda s: (jnp.minimum(s, CONV_LAG - 1), 0))

    return pl.pallas_call(
        functools.partial(_out_kernel, n_rows),
        grid=(n_rows + CONV_LAG,),
        in_specs=[
            pl.BlockSpec((BATCH, TOK_ROWS, D_MODEL), lambda s: (0, lagged(s), 0)),
            pl.BlockSpec((S5_GROUPS, chunk_rows, CHUNK_COLS), lambda s: (0, lagged(s), 0)),
            pl.BlockSpec((BATCH, TOK_ROWS, CONV_WIDTH), lambda s: (0, jnp.minimum(s, n_rows - 1), 0)),
            modv, modv, modv, modv, vec, vec,
            const((CONV_K, BF16_ROWS, CONV_WIDTH)), cvec, cvec, cvec,
            staged(wglu), staged(wout), staged(w1), staged(w2),
        ],
        out_specs=pl.BlockSpec((BATCH, TOK_ROWS, D_MODEL), lambda s: (0, lagged(s), 0)),
        out_shape=jax.ShapeDtypeStruct((bsz, n, D_MODEL), F32),
        scratch_shapes=[
            pltpu.VMEM((LANE_BLOCKS, BATCH * PITCH, LANES), F32),
            pltpu.VMEM((RING_SLOTS, BATCH, TOK_ROWS, CONV_WIDTH), BF16),
            pltpu.VMEM((STEP_ROWS, CONV_WIDTH), BF16),
            pltpu.VMEM(wglu.shape, BF16), pltpu.VMEM(wout.shape, BF16),
            pltpu.VMEM(w1.shape, BF16), pltpu.VMEM(w2.shape, BF16),
        ],
        compiler_params=pltpu.CompilerParams(
            dimension_semantics=("arbitrary",), vmem_limit_bytes=OUT_VMEM_LIMIT),
        name="out_block",
    )(x, ys, hc, g1, sh2, sc2, g2, n2, fg, wb, cb, lg, lb, wglu, wout, w1, w2)


def kernel(x, c, ctx, c_ctx, ada_w, ada_b, norm1_g, w_in, s5_lam_re, s5_lam_im, s5_log_dt, s5_b_re, s5_b_im, s5_c_re, s5_c_im, s5_d, s5_w_glu, conv_w, conv_b, conv_ln_g, conv_ln_b, w_out, norm2_g, mlp_w1, mlp_w2, final_g):
    bsz = x.shape[0]
    assert bsz == BATCH and bsz + 1 <= MOD_ROWS
    layer = 0

    cc = jnp.concatenate([c, c_ctx[None, :], jnp.zeros((MOD_ROWS - bsz - 1, D_MODEL), F32)], axis=0)
    mod = _mod_table(cc, ada_w[layer], ada_b[layer])
    sh1, sc1, g1, sh2, sc2, g2 = [m.reshape(bsz, 1, D_MODEL) for m in jnp.split(mod[:bsz], 6, axis=-1)]
    csh1 = jnp.broadcast_to(mod[bsz, :D_MODEL].reshape(1, 1, D_MODEL), (bsz, 1, D_MODEL))
    csc1 = jnp.broadcast_to(mod[bsz, D_MODEL:2 * D_MODEL].reshape(1, 1, D_MODEL), (bsz, 1, D_MODEL))

    n1 = norm1_g[layer].reshape(1, D_MODEL)
    x_lat, hc = _in_proj(x, sh1, sc1, n1, w_in[layer], True)
    x_ctx, = _in_proj(ctx, csh1, csc1, n1, w_in[layer], False)

    t_mat, m_mat, r_mat, a_pow = _s5_operators(
        s5_lam_re[layer], s5_lam_im[layer], s5_log_dt[layer], s5_b_re[layer], s5_b_im[layer],
        s5_c_re[layer], s5_c_im[layer], s5_d[layer])
    y_s5 = _s5_scan(x_lat, x_ctx, t_mat, m_mat, r_mat, a_pow, bsz)

    conv_wb = jnp.broadcast_to(conv_w[layer].astype(BF16)[:, None, :], (CONV_K, BF16_ROWS, CONV_WIDTH))
    return _out_block(
        x, y_s5, hc, g1, sh2, sc2, g2,
        norm2_g[layer].reshape(1, D_MODEL), final_g.reshape(1, D_MODEL),
        conv_wb, conv_b[layer].reshape(1, CONV_WIDTH),
        conv_ln_g[layer].reshape(1, CONV_WIDTH), conv_ln_b[layer].reshape(1, CONV_WIDTH),
        s5_w_glu[layer], w_out[layer], mlp_w1[layer], mlp_w2[layer])
```

```python
import functools

import jax
import jax.numpy as jnp
from jax import lax
from jax.experimental import pallas as pl
from jax.experimental.pallas import tpu as pltpu

F32 = jnp.float32
BF16 = jnp.bfloat16

D_MODEL = 1024
S5_WIDTH = 512
S5_GROUP = 16
S5_GROUPS = 32
S5_STATE = 64
CONV_WIDTH = 512
CONV_K = 31
GRID_W = 64
IN_COLS = S5_WIDTH + 2 * CONV_WIDTH
D_FF = 4 * D_MODEL
EPS_RMS = 1e-6
EPS_LN = 1e-5

CHUNK = 16
CHUNK_COLS = CHUNK * S5_GROUP
MOD_ROWS = 16
BATCH = 8
TOK_ROWS = 64
STEP_ROWS = BATCH * TOK_ROWS
STEP_CHUNKS = TOK_ROWS // CHUNK
LANES = 128
SUBLANES = 8
SLOTS = LANES // S5_GROUP
LANE_BLOCKS = S5_WIDTH // LANES
PITCH = 72
CONV_ROWS = 16
VMEM_LIMIT = 56 * 1024 * 1024
OUT_VMEM_LIMIT = 60 * 1024 * 1024


def _dot(a, b):
    return jnp.dot(a, b, preferred_element_type=F32)


def _rms(x, g):
    return x * lax.rsqrt(jnp.mean(x * x, axis=-1, keepdims=True) + EPS_RMS) * g


def _mod_kernel(c_ref, w_ref, b_ref, o_ref):
    c = c_ref[...]
    s = c * jax.nn.sigmoid(c)
    w = w_ref[...]
    s_hi = s.astype(BF16)
    s_lo = (s - s_hi.astype(F32)).astype(BF16)
    w_hi = w.astype(BF16)
    w_lo = (w - w_hi.astype(F32)).astype(BF16)
    o_ref[...] = _dot(s_hi, w_hi) + _dot(s_lo, w_hi) + _dot(s_hi, w_lo) + b_ref[...]


def _mod_table(cc, ada_w, ada_b):
    n_out = ada_w.shape[1]
    tn = 1536
    return pl.pallas_call(
        _mod_kernel,
        grid=(n_out // tn,),
        in_specs=[
            pl.BlockSpec((MOD_ROWS, D_MODEL), lambda j: (0, 0)),
            pl.BlockSpec((D_MODEL, tn), lambda j: (0, j)),
            pl.BlockSpec((1, tn), lambda j: (0, j)),
        ],
        out_specs=pl.BlockSpec((MOD_ROWS, tn), lambda j: (0, j)),
        out_shape=jax.ShapeDtypeStruct((MOD_ROWS, n_out), F32),
        compiler_params=pltpu.CompilerParams(vmem_limit_bytes=VMEM_LIMIT),
        name="mod_table",
    )(cc, ada_w, ada_b.reshape(1, n_out))


def _slot_skew(vregs):
    slot = lax.broadcasted_iota(jnp.int32, (BATCH, LANES), 1) // S5_GROUP
    cur = list(vregs)
    for b in range(3):
        bit_set = (slot & (1 << b)) != 0
        cur = [jnp.where(bit_set, cur[(t + (1 << b)) % SLOTS], cur[t]) for t in range(SLOTS)]
    return cur


def _to_chunk_layout(z_scr, xs_scr):
    for cl in range(STEP_CHUNKS):
        for vb in range(LANE_BLOCKS):
            for half in range(2):
                rot = []
                for ql in range(SLOTS):
                    piece = z_scr[vb, pl.ds(cl * CHUNK + SLOTS * half + ql, BATCH, stride=PITCH), :]
                    rot.append(pltpu.roll(piece, ql * S5_GROUP, axis=1) if ql else piece)
                skew = _slot_skew(rot)
                for s in range(SLOTS):
                    col = (2 * (SLOTS * vb + s) + half) * LANES
                    xs_scr[cl * BATCH:(cl + 1) * BATCH, col:col + LANES] = skew[(-s) % SLOTS]


def _from_chunk_layout(ys, u_scr):
    for cl in range(STEP_CHUNKS):
        for vb in range(LANE_BLOCKS):
            for half in range(2):
                src = []
                for s in range(SLOTS):
                    col = (2 * (SLOTS * vb + s) + half) * LANES
                    src.append(ys[cl * BATCH:(cl + 1) * BATCH, col:col + LANES])
                skew = _slot_skew(src)
                for ql in range(SLOTS):
                    out = skew[(-ql) % SLOTS]
                    if ql:
                        out = pltpu.roll(out, LANES - ql * S5_GROUP, axis=1)
                    u_scr[vb, pl.ds(cl * CHUNK + SLOTS * half + ql, BATCH, stride=PITCH), :] = out


def _inproj_kernel(with_conv, x_ref, sh_ref, sc_ref, g_ref, w_ref, xs_ref, *rest):
    hc_ref = rest[0] if with_conv else None
    z_scr, xs_scr, w_bf = rest[-3:]

    @pl.when(pl.program_id(0) == 0)
    def _():
        w_bf[...] = w_ref[...].astype(BF16)

    x = x_ref[...]
    a = _rms(x, g_ref[...]) * (1.0 + sc_ref[...]) + sh_ref[...]
    z = _dot(a.reshape(STEP_ROWS, D_MODEL).astype(BF16), w_bf[...])
    if with_conv:
        v = z[:, S5_WIDTH:S5_WIDTH + CONV_WIDTH]
        gate = z[:, S5_WIDTH + CONV_WIDTH:]
        hc_ref[...] = (v * jax.nn.sigmoid(gate)).astype(BF16).reshape(BATCH, TOK_ROWS, CONV_WIDTH)
    for vb in range(LANE_BLOCKS):
        for b in range(BATCH):
            z_scr[vb, b * PITCH:b * PITCH + TOK_ROWS, :] = z[b * TOK_ROWS:(b + 1) * TOK_ROWS, vb * LANES:(vb + 1) * LANES]
    _to_chunk_layout(z_scr, xs_scr)
    for g in range(S5_GROUPS):
        xs_ref[g] = xs_scr[:, g * CHUNK_COLS:(g + 1) * CHUNK_COLS].astype(BF16)


def _in_proj(x, sh, sc, g, w_in, with_conv):
    bsz, n, _ = x.shape
    assert bsz == BATCH and n % TOK_ROWS == 0
    steps = n // TOK_ROWS
    chunk_rows = STEP_CHUNKS * BATCH
    n_cols = IN_COLS if with_conv else S5_WIDTH
    modv = pl.BlockSpec((BATCH, 1, D_MODEL), lambda i: (0, 0, 0))
    out_specs = [pl.BlockSpec((S5_GROUPS, chunk_rows, CHUNK_COLS), lambda i: (0, i, 0))]
    out_shape = [jax.ShapeDtypeStruct((S5_GROUPS, steps * chunk_rows, CHUNK_COLS), BF16)]
    if with_conv:
        out_specs.append(pl.BlockSpec((BATCH, TOK_ROWS, CONV_WIDTH), lambda i: (0, i, 0)))
        out_shape.append(jax.ShapeDtypeStruct((bsz, n, CONV_WIDTH), BF16))
    return pl.pallas_call(
        functools.partial(_inproj_kernel, with_conv),
        grid=(steps,),
        in_specs=[
            pl.BlockSpec((BATCH, TOK_ROWS, D_MODEL), lambda i: (0, i, 0)),
            modv, modv,
            pl.BlockSpec((1, D_MODEL), lambda i: (0, 0)),
            pl.BlockSpec((D_MODEL, n_cols), lambda i: (0, 0), pipeline_mode=pl.Buffered(1)),
        ],
        out_specs=out_specs,
        out_shape=out_shape,
        scratch_shapes=[
            pltpu.VMEM((LANE_BLOCKS, BATCH * PITCH, LANES), F32),
            pltpu.VMEM((chunk_rows, S5_GROUPS * CHUNK_COLS), F32),
            pltpu.VMEM((D_MODEL, n_cols), BF16),
        ],
        compiler_params=pltpu.CompilerParams(
            dimension_semantics=("arbitrary",), vmem_limit_bytes=VMEM_LIMIT),
        name="in_proj" if with_conv else "in_proj_ctx",
    )(x, sh, sc, g, w_in)


SCAN_GROUPS = 4


def _mirror_bwd(a, n_chunks):
    rev = jnp.concatenate(
        [a[(n_chunks - 1 - c) * BATCH:(n_chunks - c) * BATCH, :] for c in range(n_chunks)], axis=0)
    lane = lax.broadcasted_iota(jnp.int32, a.shape, 1)
    return jnp.where((lane % (2 * S5_STATE)) < S5_STATE, a, rev)


def _s5_kernel(n_lat, n_ctx, xl_ref, xc_ref, t_ref, m_ref, r_ref, a_ref, y_ref, e_ref, p_ref, ec_ref):
    half = 2 * S5_STATE
    groups = range(SCAN_GROUPS)
    for g in groups:
        ec_ref[g] = _mirror_bwd(_dot(xc_ref[g], m_ref[g]), n_ctx)
        e_ref[g] = _mirror_bwd(_dot(xl_ref[g], m_ref[g]), n_lat)
    a_re = [jnp.broadcast_to(a_ref[g, 0:1, :], (BATCH, half)) for g in groups]
    a_im = [jnp.broadcast_to(a_ref[g, 1:2, :], (BATCH, half)) for g in groups]

    def advance(src_ref, g, rows, s_re, s_im):
        e = src_ref[g, rows, :]
        return (a_re[g] * s_re - a_im[g] * s_im + e[:, :half], a_re[g] * s_im + a_im[g] * s_re + e[:, half:])

    def ctx_body(i, s):
        rows = pl.ds(pl.multiple_of(i * BATCH, BATCH), BATCH)
        return tuple(advance(ec_ref, g, rows, *s[g]) for g in groups)

    zero = jnp.zeros((BATCH, half), F32)
    s = lax.fori_loop(0, n_ctx, ctx_body, tuple((zero, zero) for _ in groups))

    def lat_body(i, s):
        rows = pl.ds(pl.multiple_of(i * BATCH, BATCH), BATCH)
        out = []
        for g in groups:
            s_re, s_im = s[g]
            p_ref[g, rows, :half] = s_re
            p_ref[g, rows, half:] = s_im
            out.append(advance(e_ref, g, rows, s_re, s_im))
        return tuple(out)

    lax.fori_loop(0, n_lat, lat_body, s)
    for g in groups:
        p = _mirror_bwd(p_ref[g], n_lat)
        y = _dot(xl_ref[g], t_ref[g]) + _dot(p.astype(BF16), r_ref[g])
        y_ref[g] = y.astype(BF16)


def _s5_scan(x_lat, x_ctx, t_mat, m_mat, r_mat, a_pow, bsz):
    rows = x_lat.shape[1]
    rows_c = x_ctx.shape[1]
    gs = SCAN_GROUPS

    def per_group(*shape):
        return pl.BlockSpec((gs,) + shape, lambda i: (i,) + (0,) * len(shape))

    mat_spec = per_group(CHUNK_COLS, CHUNK_COLS)
    return pl.pallas_call(
        functools.partial(_s5_kernel, rows // bsz, rows_c // bsz),
        grid=(S5_GROUPS // gs,),
        in_specs=[
            per_group(rows, CHUNK_COLS), per_group(rows_c, CHUNK_COLS),
            mat_spec, mat_spec, mat_spec,
            per_group(2, 2 * S5_STATE),
        ],
        out_specs=per_group(rows, CHUNK_COLS),
        out_shape=jax.ShapeDtypeStruct((S5_GROUPS, rows, CHUNK_COLS), BF16),
        scratch_shapes=[
            pltpu.VMEM((gs, rows, CHUNK_COLS), F32),
            pltpu.VMEM((gs, rows, CHUNK_COLS), F32),
            pltpu.VMEM((gs, rows_c, CHUNK_COLS), F32),
        ],
        compiler_params=pltpu.CompilerParams(
            dimension_semantics=("parallel",), vmem_limit_bytes=VMEM_LIMIT),
        name="s5_scan",
    )(x_lat, x_ctx, t_mat, m_mat, r_mat, a_pow)


GROUPS_PER_STEP = SLOTS


def _cpow(kk, log_mag, ang):
    mag = jnp.exp(kk * log_mag)
    return mag * jnp.cos(kk * ang), mag * jnp.sin(kk * ang)


def _power_rows(pw, is_fwd, ks):
    zero = jnp.zeros((1, pw.shape[1]), F32)
    blocks = []
    for kf, kb in ks:
        row = jnp.where(is_fwd, zero if kf is None else pw[kf:kf + 1, :], zero if kb is None else pw[kb:kb + 1, :])
        blocks.append(jnp.broadcast_to(row, (S5_GROUP, pw.shape[1])))
    return jnp.concatenate(blocks, axis=0)


def _tile_rows(a, reps):
    return jnp.concatenate([a] * reps, axis=0)


def _dot3_nt(a, b):
    dn = (((1,), (1,)), ((), ()))
    a_hi = a.astype(BF16)
    a_lo = (a - a_hi.astype(F32)).astype(BF16)
    b_hi = b.astype(BF16)
    b_lo = (b - b_hi.astype(F32)).astype(BF16)

    def nt(u, v):
        return lax.dot_general(u, v, dn, preferred_element_type=F32)

    return nt(a_hi, b_hi) + nt(a_lo, b_hi) + nt(a_hi, b_lo)


def _chunk_row_order(a, s):
    blocks = []
    for half in range(2):
        for slot in range(SLOTS):
            q = SLOTS * half + (slot - s) % SLOTS
            blocks.append(a[q * S5_GROUP:(q + 1) * S5_GROUP, :])
    return jnp.concatenate(blocks, axis=0)


def _s5_prep_kernel(lre_ref, lim_ref, ldt_ref, btr_ref, bti_ref, ctr_ref, cti_ref, dw_ref,
                    t_ref, m_ref, r_ref, a_ref):
    half_l = 2 * S5_STATE
    n_lag = 2 * CHUNK - 1
    is_fwd = lax.broadcasted_iota(jnp.int32, (1, half_l), 1) < S5_STATE
    n_pow = 24
    k_row = lax.broadcasted_iota(jnp.int32, (n_pow, half_l), 0).astype(F32)
    top = CHUNK - 1
    slot_of_lane = lax.broadcasted_iota(jnp.int32, (CHUNK_COLS, LANES), 1) // S5_GROUP
    eye = (lax.broadcasted_iota(jnp.int32, (CHUNK_COLS, CHUNK_COLS), 0)
           == lax.broadcasted_iota(jnp.int32, (CHUNK_COLS, CHUNK_COLS), 1))

    for s in range(GROUPS_PER_STEP):
        lre, lim = lre_ref[s], lim_ref[s]
        dt = jnp.exp(ldt_ref[s])
        log_mag, ang = lre * dt, lim * dt
        pw_re, pw_im = _cpow(k_row, log_mag, ang)
        a1_re, a1_im = pw_re[1:2, :], pw_im[1:2, :]
        den = lre * lre + lim * lim
        num_re = a1_re - 1.0
        f_re = (num_re * lre + a1_im * lim) / den
        f_im = (a1_im * lre - num_re * lim) / den
        bt_re, bt_im = btr_ref[s], bti_ref[s]
        bb_re = f_re * bt_re - f_im * bt_im
        bb_im = f_re * bt_im + f_im * bt_re
        ct_re, ct_im = ctr_ref[s], cti_ref[s]

        lag_ks = [(top - j if j <= top else None, j - top if j >= top else None) for j in range(n_lag)]
        aj_re, aj_im = _power_rows(pw_re, is_fwd, lag_ks), _power_rows(pw_im, is_fwd, lag_ks)
        bl_re, bl_im = _tile_rows(bb_re, n_lag), _tile_rows(bb_im, n_lag)
        l_re = aj_re * bl_re - aj_im * bl_im
        l_im = aj_re * bl_im + aj_im * bl_re
        cw_re, cw_im = _tile_rows(ct_re, CHUNK), _tile_rows(ct_im, CHUNK)
        kwide = _dot3_nt(jnp.concatenate([l_re, -l_im], axis=1),
                         jnp.concatenate([cw_re, cw_im], axis=1))
        halves = []
        for half in range(2):
            col = kwide[:, half * LANES:(half + 1) * LANES]
            out = None
            for slot in range(SLOTS):
                q = SLOTS * half + (slot - s) % SLOTS
                start = (CHUNK - 1 - q) * S5_GROUP
                win = col[start:start + CHUNK_COLS, :]
                out = win if out is None else jnp.where(slot_of_lane == slot, win, out)
            halves.append(out)
        t_nat = jnp.concatenate(halves, axis=1)
        t_mat = _chunk_row_order(t_nat, s) + jnp.where(eye, dw_ref[s], 0.0)
        t_ref[s] = t_mat.astype(BF16)

        m_ks = [(top - q, q) for q in range(CHUNK)]
        am_re, am_im = _power_rows(pw_re, is_fwd, m_ks), _power_rows(pw_im, is_fwd, m_ks)
        bm_re, bm_im = _tile_rows(bb_re, CHUNK), _tile_rows(bb_im, CHUNK)
        m_nat = jnp.concatenate([am_re * bm_re - am_im * bm_im, am_re * bm_im + am_im * bm_re], axis=1)
        m_ref[s] = _chunk_row_order(m_nat, s).astype(BF16)

        r_ks = [(q + 1, CHUNK - q) for q in range(CHUNK)]
        ar_re, ar_im = _power_rows(pw_re, is_fwd, r_ks), _power_rows(pw_im, is_fwd, r_ks)
        rt_nat = jnp.concatenate([ar_re * cw_re - ar_im * cw_im, -(ar_re * cw_im + ar_im * cw_re)], axis=1)
        r_ref[s] = _chunk_row_order(rt_nat, s).T.astype(BF16)

        a_ref[s] = jnp.concatenate([pw_re[CHUNK:CHUNK + 1, :], pw_im[CHUNK:CHUNK + 1, :]], axis=0)


def _s5_operators(lam_re, lam_im, log_dt, b_re, b_im, c_re, c_im, d_skip):
    half_l = 2 * S5_STATE

    def lanes(v):
        return jnp.transpose(v, (1, 0, 2)).reshape(S5_GROUPS, 1, half_l)

    ldt = lanes(jnp.broadcast_to(log_dt[..., None], lam_re.shape))
    bt_re = jnp.transpose(b_re, (1, 3, 0, 2)).reshape(S5_GROUPS, S5_GROUP, half_l)
    bt_im = jnp.transpose(b_im, (1, 3, 0, 2)).reshape(S5_GROUPS, S5_GROUP, half_l)
    ct_re = jnp.transpose(c_re, (1, 2, 0, 3)).reshape(S5_GROUPS, S5_GROUP, half_l)
    ct_im = jnp.transpose(c_im, (1, 2, 0, 3)).reshape(S5_GROUPS, S5_GROUP, half_l)
    dw = jnp.tile(d_skip, (1, CHUNK)).reshape(S5_GROUPS, 1, CHUNK_COLS)

    gs = GROUPS_PER_STEP
    vec = pl.BlockSpec((gs, 1, half_l), lambda i: (i, 0, 0))
    par = pl.BlockSpec((gs, S5_GROUP, half_l), lambda i: (i, 0, 0))
    mat = pl.BlockSpec((gs, CHUNK_COLS, CHUNK_COLS), lambda i: (i, 0, 0))
    mat_shape = jax.ShapeDtypeStruct((S5_GROUPS, CHUNK_COLS, CHUNK_COLS), BF16)
    return pl.pallas_call(
        _s5_prep_kernel,
        grid=(S5_GROUPS // gs,),
        in_specs=[vec, vec, vec, par, par, par, par,
                  pl.BlockSpec((gs, 1, CHUNK_COLS), lambda i: (i, 0, 0))],
        out_specs=[mat, mat, mat, pl.BlockSpec((gs, 2, half_l), lambda i: (i, 0, 0))],
        out_shape=[mat_shape, mat_shape, mat_shape,
                   jax.ShapeDtypeStruct((S5_GROUPS, 2, half_l), F32)],
        compiler_params=pltpu.CompilerParams(
            dimension_semantics=("parallel",), vmem_limit_bytes=VMEM_LIMIT),
        name="s5_operators",
    )(lanes(lam_re), lanes(lam_im), ldt, bt_re, bt_im, ct_re, ct_im, dw)


CONV_HALF = CONV_K // 2
CONV_LAG = CONV_HALF + 1
RING_SLOTS = 32
CONV_BLOCKS = STEP_ROWS // CONV_ROWS
BF16_ROWS = 16
TAP_GROUP = 8
FF_TILE = 512


def _aligned(x, m):
    return x if isinstance(x, int) else pl.multiple_of(x, m)


def _conv_block(ring, slots, blk, wb_ref, cb_ref, lg_ref, lb_ref, seed=None):
    b = blk // (TOK_ROWS // CONV_ROWS)
    t0 = _aligned((blk % (TOK_ROWS // CONV_ROWS)) * CONV_ROWS, CONV_ROWS)
    packed = (CONV_ROWS // BF16_ROWS, BF16_ROWS, CONV_WIDTH)
    acc = jnp.zeros(packed, F32)
    if seed is not None:
        rows = jnp.concatenate([seed] * (BF16_ROWS // SUBLANES), axis=0)
        acc = acc + jnp.concatenate([rows] * (CONV_WIDTH // LANES), axis=1)[None]
    for k0 in range(0, CONV_K, TAP_GROUP):
        group = None
        for k in range(k0, min(k0 + TAP_GROUP, CONV_K)):
            prod = wb_ref[k] * ring[slots[k], b, pl.ds(t0, CONV_ROWS), :].reshape(packed)
            group = prod if group is None else group + prod
        acc = acc + group.astype(F32)
    acc = acc.reshape(CONV_ROWS, CONV_WIDTH) + cb_ref[...]
    mu = jnp.mean(acc, axis=-1, keepdims=True)
    dev = acc - mu
    var = jnp.mean(dev * dev, axis=-1, keepdims=True)
    y = dev * lax.rsqrt(var + EPS_LN) * lg_ref[...] + lb_ref[...]
    return (y * jax.nn.sigmoid(y)).astype(BF16)


def _out_kernel(n_rows, x_ref, ys_ref, hc_ref, g1_ref, sh2_ref, sc2_ref, g2_ref, n2_ref, fg_ref,
                wb_ref, cb_ref, lg_ref, lb_ref, wglu_in, wout_in, w1_in, w2_in, o_ref,
                u_scr, ring, yc_scr, wglu_ref, wout_ref, w1_ref, w2_ref):
    s = pl.program_id(0)
    zero_row = jnp.zeros((BATCH, TOK_ROWS, CONV_WIDTH), BF16)

    @pl.when(s < CONV_LAG)
    def _():
        for w_in_ref, w_bf in ((wglu_in, wglu_ref), (wout_in, wout_ref), (w1_in, w1_ref), (w2_in, w2_ref)):
            rows = w_in_ref.shape[0]
            w_bf[pl.ds(pl.multiple_of(s * rows, rows), rows), :] = w_in_ref[...].astype(BF16)

    @pl.when(s == 0)
    def _():
        for i in range(RING_SLOTS - CONV_HALF, RING_SLOTS):
            ring[i] = zero_row

    @pl.when(s < n_rows)
    def _():
        ring[s % RING_SLOTS] = hc_ref[...]

    @pl.when(s >= n_rows)
    def _():
        ring[s % RING_SLOTS] = zero_row

    slots = [(s + (RING_SLOTS - 2 * CONV_HALF) + k) % RING_SLOTS for k in range(CONV_K)]
    conv_args = (wb_ref, cb_ref, lg_ref, lb_ref)

    @pl.when(s == CONV_HALF)
    def _():
        def body(blk, carry):
            rows = pl.ds(_aligned(blk * CONV_ROWS, CONV_ROWS), CONV_ROWS)
            yc_scr[rows, :] = _conv_block(ring, slots, blk, *conv_args)
            return carry
        lax.fori_loop(0, CONV_BLOCKS, body, 0, unroll=4)

    @pl.when(s >= CONV_LAG)
    def _():
        yc = yc_scr[...]
        ys = jnp.concatenate([ys_ref[g].astype(F32) for g in range(S5_GROUPS)], axis=1)
        _from_chunk_layout(ys, u_scr)
        y_s5 = jnp.concatenate(
            [jnp.concatenate([u_scr[vb, b * PITCH:b * PITCH + TOK_ROWS, :] for b in range(BATCH)], axis=0)
             for vb in range(LANE_BLOCKS)], axis=1)
        gl = jax.nn.gelu(y_s5)
        s5o = gl * jax.nn.sigmoid(_dot(gl.astype(BF16), wglu_ref[...]))
        mix = _dot(s5o.astype(BF16), wout_ref[:S5_WIDTH, :]) + _dot(yc, wout_ref[S5_WIDTH:, :])
        h1 = x_ref[...] + g1_ref[...] * mix.reshape(BATCH, TOK_ROWS, D_MODEL)
        a2 = _rms(h1, n2_ref[...]) * (1.0 + sc2_ref[...]) + sh2_ref[...]
        a2 = a2.reshape(STEP_ROWS, D_MODEL).astype(BF16)
        ff_tile = FF_TILE
        n_ff = D_FF // ff_tile
        per_dot = CONV_BLOCKS // (2 * n_ff)
        never = s < 0

        def seeded_conv(first_blk, mat):
            for i in range(per_dot):
                blk = first_blk + i
                r = i * (STEP_ROWS // per_dot)
                seed = jnp.where(never, mat[r:r + SUBLANES, 0:LANES], 0.0)
                yc_scr[blk * CONV_ROWS:(blk + 1) * CONV_ROWS, :] = _conv_block(
                    ring, slots, blk, *conv_args, seed=seed)

        acc = jnp.zeros((STEP_ROWS, D_MODEL), F32)
        for j in range(n_ff):
            up = _dot(a2, w1_ref[:, j * ff_tile:(j + 1) * ff_tile])
            seeded_conv(2 * j * per_dot, up)
            hid = jnp.maximum(up, 0.0)
            down = _dot((hid * hid).astype(BF16), w2_ref[j * ff_tile:(j + 1) * ff_tile, :])
            seeded_conv((2 * j + 1) * per_dot, down)
            acc = acc + down
        h2 = h1 + g2_ref[...] * acc.reshape(BATCH, TOK_ROWS, D_MODEL)
        o_ref[...] = _rms(h2, fg_ref[...])


def _out_block(x, ys, hc, g1, sh2, sc2, g2, n2, fg, wb, cb, lg, lb, wglu, wout, w1, w2):
    bsz, n, _ = x.shape
    assert bsz == BATCH and n % TOK_ROWS == 0
    n_rows = n // TOK_ROWS
    chunk_rows = STEP_CHUNKS * BATCH
    modv = pl.BlockSpec((BATCH, 1, D_MODEL), lambda s: (0, 0, 0))
    vec = pl.BlockSpec((1, D_MODEL), lambda s: (0, 0))
    cvec = pl.BlockSpec((1, CONV_WIDTH), lambda s: (0, 0))

    def const(shape):
        return pl.BlockSpec(shape, lambda s: (0,) * len(shape), pipeline_mode=pl.Buffered(1))

    def lagged(s):
        return jnp.maximum(s - CONV_LAG, 0)

    def staged(w):
        return pl.BlockSpec((w.shape[0] // CONV_LAG, w.shape[1]), lambda s: (jnp.minimum(s, CONV_LAG - 1), 0))

    return pl.pallas_call(
        functools.partial(_out_kernel, n_rows),
        grid=(n_rows + CONV_LAG,),
        in_specs=[
            pl.BlockSpec((BATCH, TOK_ROWS, D_MODEL), lambda s: (0, lagged(s), 0)),
            pl.BlockSpec((S5_GROUPS, chunk_rows, CHUNK_COLS), lambda s: (0, lagged(s), 0)),
            pl.BlockSpec((BATCH, TOK_ROWS, CONV_WIDTH), lambda s: (0, jnp.minimum(s, n_rows - 1), 0)),
            modv, modv, modv, modv, vec, vec,
            const((CONV_K, BF16_ROWS, CONV_WIDTH)), cvec, cvec, cvec,
            staged(wglu), staged(wout), staged(w1), staged(w2),
        ],
        out_specs=pl.BlockSpec((BATCH, TOK_ROWS, D_MODEL), lambda s: (0, lagged(s), 0)),
        out_shape=jax.ShapeDtypeStruct((bsz, n, D_MODEL), F32),
        scratch_shapes=[
            pltpu.VMEM((LANE_BLOCKS, BATCH * PITCH, LANES), F32),
            pltpu.VMEM((RING_SLOTS, BATCH, TOK_ROWS, CONV_WIDTH), BF16),
            pltpu.VMEM((STEP_ROWS, CONV_WIDTH), BF16),
            pltpu.VMEM(wglu.shape, BF16), pltpu.VMEM(wout.shape, BF16),
            pltpu.VMEM(w1.shape, BF16), pltpu.VMEM(w2.shape, BF16),
        ],
        compiler_params=pltpu.CompilerParams(
            dimension_semantics=("arbitrary",), vmem_limit_bytes=OUT_VMEM_LIMIT),
        name="out_block",
    )(x, ys, hc, g1, sh2, sc2, g2, n2, fg, wb, cb, lg, lb, wglu, wout, w1, w2)


def kernel(x, c, ctx, c_ctx, ada_w, ada_b, norm1_g, w_in, s5_lam_re, s5_lam_im, s5_log_dt, s5_b_re, s5_b_im, s5_c_re, s5_c_im, s5_d, s5_w_glu, conv_w, conv_b, conv_ln_g, conv_ln_b, w_out, norm2_g, mlp_w1, mlp_w2, final_g):
    bsz = x.shape[0]
    assert bsz == BATCH and bsz + 1 <= MOD_ROWS
    layer = 0

    cc = jnp.concatenate([c, c_ctx[None, :], jnp.zeros((MOD_ROWS - bsz - 1, D_MODEL), F32)], axis=0)
    mod = _mod_table(cc, ada_w[layer], ada_b[layer])
    sh1, sc1, g1, sh2, sc2, g2 = [m.reshape(bsz, 1, D_MODEL) for m in jnp.split(mod[:bsz], 6, axis=-1)]
    csh1 = jnp.broadcast_to(mod[bsz, :D_MODEL].reshape(1, 1, D_MODEL), (bsz, 1, D_MODEL))
    csc1 = jnp.broadcast_to(mod[bsz, D_MODEL:2 * D_MODEL].reshape(1, 1, D_MODEL), (bsz, 1, D_MODEL))

    n1 = norm1_g[layer].reshape(1, D_MODEL)
    x_lat, hc = _in_proj(x, sh1, sc1, n1, w_in[layer], True)
    x_ctx, = _in_proj(ctx, csh1, csc1, n1, w_in[layer], False)

    t_mat, m_mat, r_mat, a_pow = _s5_operators(
        s5_lam_re[layer], s5_lam_im[layer], s5_log_dt[layer], s5_b_re[layer], s5_b_im[layer],
        s5_c_re[layer], s5_c_im[layer], s5_d[layer])
    y_s5 = _s5_scan(x_lat, x_ctx, t_mat, m_mat, r_mat, a_pow, bsz)

    conv_wb = jnp.broadcast_to(conv_w[layer].astype(BF16)[:, None, :], (CONV_K, BF16_ROWS, CONV_WIDTH))
    return _out_block(
        x, y_s5, hc, g1, sh2, sc2, g2,
        norm2_g[layer].reshape(1, D_MODEL), final_g.reshape(1, D_MODEL),
        conv_wb, conv_b[layer].reshape(1, CONV_WIDTH),
        conv_ln_g[layer].reshape(1, CONV_WIDTH), conv_ln_b[layer].reshape(1, CONV_WIDTH),
        s5_w_glu[layer], w_out[layer], mlp_w1[layer], mlp_w2[layer])
```

```python
import functools

import jax
import jax.numpy as jnp
from jax import lax
from jax.experimental import pallas as pl
from jax.experimental.pallas import tpu as pltpu

F32 = jnp.float32
BF16 = jnp.bfloat16

D_MODEL = 1024
S5_WIDTH = 512
S5_GROUP = 16
S5_GROUPS = 32
S5_STATE = 64
CONV_WIDTH = 512
CONV_K = 31
GRID_W = 64
IN_COLS = S5_WIDTH + 2 * CONV_WIDTH
D_FF = 4 * D_MODEL
EPS_RMS = 1e-6
EPS_LN = 1e-5

CHUNK = 16
CHUNK_COLS = CHUNK * S5_GROUP
MOD_ROWS = 16
BATCH = 8
TOK_ROWS = 64
STEP_ROWS = BATCH * TOK_ROWS
STEP_CHUNKS = TOK_ROWS // CHUNK
LANES = 128
SUBLANES = 8
SLOTS = LANES // S5_GROUP
LANE_BLOCKS = S5_WIDTH // LANES
PITCH = 72
CONV_ROWS = 16
VMEM_LIMIT = 56 * 1024 * 1024
OUT_VMEM_LIMIT = 60 * 1024 * 1024


def _dot(a, b):
    return jnp.dot(a, b, preferred_element_type=F32)


def _rms(x, g):
    return x * lax.rsqrt(jnp.mean(x * x, axis=-1, keepdims=True) + EPS_RMS) * g


def _mod_kernel(c_ref, w_ref, b_ref, o_ref):
    c = c_ref[...]
    s = c * jax.nn.sigmoid(c)
    w = w_ref[...]
    s_hi = s.astype(BF16)
    s_lo = (s - s_hi.astype(F32)).astype(BF16)
    w_hi = w.astype(BF16)
    w_lo = (w - w_hi.astype(F32)).astype(BF16)
    o_ref[...] = _dot(s_hi, w_hi) + _dot(s_lo, w_hi) + _dot(s_hi, w_lo) + b_ref[...]


def _mod_table(cc, ada_w, ada_b):
    n_out = ada_w.shape[1]
    tn = 1536
    return pl.pallas_call(
        _mod_kernel,
        grid=(n_out // tn,),
        in_specs=[
            pl.BlockSpec((MOD_ROWS, D_MODEL), lambda j: (0, 0)),
            pl.BlockSpec((D_MODEL, tn), lambda j: (0, j)),
            pl.BlockSpec((1, tn), lambda j: (0, j)),
        ],
        out_specs=pl.BlockSpec((MOD_ROWS, tn), lambda j: (0, j)),
        out_shape=jax.ShapeDtypeStruct((MOD_ROWS, n_out), F32),
        compiler_params=pltpu.CompilerParams(vmem_limit_bytes=VMEM_LIMIT),
        name="mod_table",
    )(cc, ada_w, ada_b.reshape(1, n_out))


def _slot_skew(vregs):
    slot = lax.broadcasted_iota(jnp.int32, (BATCH, LANES), 1) // S5_GROUP
    cur = list(vregs)
    for b in range(3):
        bit_set = (slot & (1 << b)) != 0
        cur = [jnp.where(bit_set, cur[(t + (1 << b)) % SLOTS], cur[t]) for t in range(SLOTS)]
    return cur


def _to_chunk_layout(z_scr, xs_scr):
    for cl in range(STEP_CHUNKS):
        for vb in range(LANE_BLOCKS):
            for half in range(2):
                rot = []
                for ql in range(SLOTS):
                    piece = z_scr[vb, pl.ds(cl * CHUNK + SLOTS * half + ql, BATCH, stride=PITCH), :]
                    rot.append(pltpu.roll(piece, ql * S5_GROUP, axis=1) if ql else piece)
                skew = _slot_skew(rot)
                for s in range(SLOTS):
                    col = (2 * (SLOTS * vb + s) + half) * LANES
                    xs_scr[cl * BATCH:(cl + 1) * BATCH, col:col + LANES] = skew[(-s) % SLOTS]


def _from_chunk_layout(ys, u_scr):
    for cl in range(STEP_CHUNKS):
        for vb in range(LANE_BLOCKS):
            for half in range(2):
                src = []
                for s in range(SLOTS):
                    col = (2 * (SLOTS * vb + s) + half) * LANES
                    src.append(ys[cl * BATCH:(cl + 1) * BATCH, col:col + LANES])
                skew = _slot_skew(src)
                for ql in range(SLOTS):
                    out = skew[(-ql) % SLOTS]
                    if ql:
                        out = pltpu.roll(out, LANES - ql * S5_GROUP, axis=1)
                    u_scr[vb, pl.ds(cl * CHUNK + SLOTS * half + ql, BATCH, stride=PITCH), :] = out


def _inproj_kernel(with_conv, x_ref, sh_ref, sc_ref, g_ref, w_ref, xs_ref, *rest):
    hc_ref = rest[0] if with_conv else None
    z_scr, xs_scr, w_bf, shw = rest[-4:]

    @pl.when(pl.program_id(0) == 0)
    def _():
        w_bf[...] = w_ref[...].astype(BF16)
        shw[...] = _dot(sh_ref[...].astype(BF16), w_bf[...])

    xg, rs = [], []
    for b in range(BATCH):
        xb = x_ref[b]
        gs = g_ref[...] * (1.0 + sc_ref[b:b + 1, :])
        xg.append((xb * gs).astype(BF16))
        rs.append(lax.rsqrt(jnp.mean(xb * xb, axis=-1, keepdims=True) + EPS_RMS))
    z0 = _dot(jnp.concatenate(xg, axis=0), w_bf[...])
    z = jnp.concatenate(
        [z0[b * TOK_ROWS:(b + 1) * TOK_ROWS, :] * rs[b] + shw[b:b + 1, :] for b in range(BATCH)], axis=0)
    if with_conv:
        v = z[:, S5_WIDTH:S5_WIDTH + CONV_WIDTH]
        gate = z[:, S5_WIDTH + CONV_WIDTH:]
        hc_ref[...] = (v * jax.nn.sigmoid(gate)).astype(BF16).reshape(BATCH, TOK_ROWS, CONV_WIDTH)
    for vb in range(LANE_BLOCKS):
        for b in range(BATCH):
            z_scr[vb, b * PITCH:b * PITCH + TOK_ROWS, :] = z[b * TOK_ROWS:(b + 1) * TOK_ROWS, vb * LANES:(vb + 1) * LANES]
    _to_chunk_layout(z_scr, xs_scr)
    for g in range(S5_GROUPS):
        xs_ref[g] = xs_scr[:, g * CHUNK_COLS:(g + 1) * CHUNK_COLS].astype(BF16)


def _in_proj(x, sh, sc, g, w_in, with_conv):
    bsz, n, _ = x.shape
    assert bsz == BATCH and n % TOK_ROWS == 0
    steps = n // TOK_ROWS
    chunk_rows = STEP_CHUNKS * BATCH
    n_cols = IN_COLS if with_conv else S5_WIDTH
    modv = pl.BlockSpec((BATCH, D_MODEL), lambda i: (0, 0))
    out_specs = [pl.BlockSpec((S5_GROUPS, chunk_rows, CHUNK_COLS), lambda i: (0, i, 0))]
    out_shape = [jax.ShapeDtypeStruct((S5_GROUPS, steps * chunk_rows, CHUNK_COLS), BF16)]
    if with_conv:
        out_specs.append(pl.BlockSpec((BATCH, TOK_ROWS, CONV_WIDTH), lambda i: (0, i, 0)))
        out_shape.append(jax.ShapeDtypeStruct((bsz, n, CONV_WIDTH), BF16))
    return pl.pallas_call(
        functools.partial(_inproj_kernel, with_conv),
        grid=(steps,),
        in_specs=[
            pl.BlockSpec((BATCH, TOK_ROWS, D_MODEL), lambda i: (0, i, 0)),
            modv, modv,
            pl.BlockSpec((1, D_MODEL), lambda i: (0, 0)),
            pl.BlockSpec((D_MODEL, n_cols), lambda i: (0, 0), pipeline_mode=pl.Buffered(1)),
        ],
        out_specs=out_specs,
        out_shape=out_shape,
        scratch_shapes=[
            pltpu.VMEM((LANE_BLOCKS, BATCH * PITCH, LANES), F32),
            pltpu.VMEM((chunk_rows, S5_GROUPS * CHUNK_COLS), F32),
            pltpu.VMEM((D_MODEL, n_cols), BF16),
            pltpu.VMEM((BATCH, n_cols), F32),
        ],
        compiler_params=pltpu.CompilerParams(
            dimension_semantics=("arbitrary",), vmem_limit_bytes=VMEM_LIMIT),
        name="in_proj" if with_conv else "in_proj_ctx",
    )(x, sh.reshape(BATCH, D_MODEL), sc.reshape(BATCH, D_MODEL), g, w_in)


SCAN_GROUPS = 4


def _mirror_bwd(a, n_chunks):
    rev = jnp.concatenate(
        [a[(n_chunks - 1 - c) * BATCH:(n_chunks - c) * BATCH, :] for c in range(n_chunks)], axis=0)
    lane = lax.broadcasted_iota(jnp.int32, a.shape, 1)
    return jnp.where((lane % (2 * S5_STATE)) < S5_STATE, a, rev)


def _s5_kernel(n_lat, n_ctx, xl_ref, xc_ref, t_ref, m_ref, r_ref, a_ref, y_ref, e_ref, p_ref, ec_ref):
    half = 2 * S5_STATE
    groups = range(SCAN_GROUPS)
    for g in groups:
        ec_ref[g] = _mirror_bwd(_dot(xc_ref[g], m_ref[g]), n_ctx)
        e_ref[g] = _mirror_bwd(_dot(xl_ref[g], m_ref[g]), n_lat)
    a_re = [jnp.broadcast_to(a_ref[g, 0:1, :], (BATCH, half)) for g in groups]
    a_im = [jnp.broadcast_to(a_ref[g, 1:2, :], (BATCH, half)) for g in groups]

    def advance(src_ref, g, rows, s_re, s_im):
        e = src_ref[g, rows, :]
        return (a_re[g] * s_re - a_im[g] * s_im + e[:, :half], a_re[g] * s_im + a_im[g] * s_re + e[:, half:])

    def ctx_body(i, s):
        rows = pl.ds(pl.multiple_of(i * BATCH, BATCH), BATCH)
        return tuple(advance(ec_ref, g, rows, *s[g]) for g in groups)

    zero = jnp.zeros((BATCH, half), F32)
    s = lax.fori_loop(0, n_ctx, ctx_body, tuple((zero, zero) for _ in groups))

    def lat_body(i, s):
        rows = pl.ds(pl.multiple_of(i * BATCH, BATCH), BATCH)
        out = []
        for g in groups:
            s_re, s_im = s[g]
            p_ref[g, rows, :half] = s_re
            p_ref[g, rows, half:] = s_im
            out.append(advance(e_ref, g, rows, s_re, s_im))
        return tuple(out)

    lax.fori_loop(0, n_lat, lat_body, s)
    for g in groups:
        p = _mirror_bwd(p_ref[g], n_lat)
        y = _dot(xl_ref[g], t_ref[g]) + _dot(p.astype(BF16), r_ref[g])
        y_ref[g] = y.astype(BF16)


def _s5_scan(x_lat, x_ctx, t_mat, m_mat, r_mat, a_pow, bsz):
    rows = x_lat.shape[1]
    rows_c = x_ctx.shape[1]
    gs = SCAN_GROUPS

    def per_group(*shape):
        return pl.BlockSpec((gs,) + shape, lambda i: (i,) + (0,) * len(shape))

    mat_spec = per_group(CHUNK_COLS, CHUNK_COLS)
    return pl.pallas_call(
        functools.partial(_s5_kernel, rows // bsz, rows_c // bsz),
        grid=(S5_GROUPS // gs,),
        in_specs=[
            per_group(rows, CHUNK_COLS), per_group(rows_c, CHUNK_COLS),
            mat_spec, mat_spec, mat_spec,
            per_group(2, 2 * S5_STATE),
        ],
        out_specs=per_group(rows, CHUNK_COLS),
        out_shape=jax.ShapeDtypeStruct((S5_GROUPS, rows, CHUNK_COLS), BF16),
        scratch_shapes=[
            pltpu.VMEM((gs, rows, CHUNK_COLS), F32),
            pltpu.VMEM((gs, rows, CHUNK_COLS), F32),
            pltpu.VMEM((gs, rows_c, CHUNK_COLS), F32),
        ],
        compiler_params=pltpu.CompilerParams(
            dimension_semantics=("parallel",), vmem_limit_bytes=VMEM_LIMIT),
        name="s5_scan",
    )(x_lat, x_ctx, t_mat, m_mat, r_mat, a_pow)


GROUPS_PER_STEP = SLOTS


def _cpow(kk, log_mag, ang):
    mag = jnp.exp(kk * log_mag)
    return mag * jnp.cos(kk * ang), mag * jnp.sin(kk * ang)


def _power_rows(pw, is_fwd, ks):
    zero = jnp.zeros((1, pw.shape[1]), F32)
    blocks = []
    for kf, kb in ks:
        row = jnp.where(is_fwd, zero if kf is None else pw[kf:kf + 1, :], zero if kb is None else pw[kb:kb + 1, :])
        blocks.append(jnp.broadcast_to(row, (S5_GROUP, pw.shape[1])))
    return jnp.concatenate(blocks, axis=0)


def _tile_rows(a, reps):
    return jnp.concatenate([a] * reps, axis=0)


def _dot3_nt(a, b):
    dn = (((1,), (1,)), ((), ()))
    a_hi = a.astype(BF16)
    a_lo = (a - a_hi.astype(F32)).astype(BF16)
    b_hi = b.astype(BF16)
    b_lo = (b - b_hi.astype(F32)).astype(BF16)

    def nt(u, v):
        return lax.dot_general(u, v, dn, preferred_element_type=F32)

    return nt(a_hi, b_hi) + nt(a_lo, b_hi) + nt(a_hi, b_lo)


def _chunk_row_order(a, s):
    blocks = []
    for half in range(2):
        for slot in range(SLOTS):
            q = SLOTS * half + (slot - s) % SLOTS
            blocks.append(a[q * S5_GROUP:(q + 1) * S5_GROUP, :])
    return jnp.concatenate(blocks, axis=0)


def _s5_prep_kernel(lre_ref, lim_ref, ldt_ref, btr_ref, bti_ref, ctr_ref, cti_ref, dw_ref,
                    t_ref, m_ref, r_ref, a_ref):
    half_l = 2 * S5_STATE
    n_lag = 2 * CHUNK - 1
    is_fwd = lax.broadcasted_iota(jnp.int32, (1, half_l), 1) < S5_STATE
    n_pow = 24
    k_row = lax.broadcasted_iota(jnp.int32, (n_pow, half_l), 0).astype(F32)
    top = CHUNK - 1
    slot_of_lane = lax.broadcasted_iota(jnp.int32, (CHUNK_COLS, LANES), 1) // S5_GROUP
    eye = (lax.broadcasted_iota(jnp.int32, (CHUNK_COLS, CHUNK_COLS), 0)
           == lax.broadcasted_iota(jnp.int32, (CHUNK_COLS, CHUNK_COLS), 1))

    for s in range(GROUPS_PER_STEP):
        lre, lim = lre_ref[s], lim_ref[s]
        dt = jnp.exp(ldt_ref[s])
        log_mag, ang = lre * dt, lim * dt
        pw_re, pw_im = _cpow(k_row, log_mag, ang)
        a1_re, a1_im = pw_re[1:2, :], pw_im[1:2, :]
        den = lre * lre + lim * lim
        num_re = a1_re - 1.0
        f_re = (num_re * lre + a1_im * lim) / den
        f_im = (a1_im * lre - num_re * lim) / den
        bt_re, bt_im = btr_ref[s], bti_ref[s]
        bb_re = f_re * bt_re - f_im * bt_im
        bb_im = f_re * bt_im + f_im * bt_re
        ct_re, ct_im = ctr_ref[s], cti_ref[s]

        lag_ks = [(top - j if j <= top else None, j - top if j >= top else None) for j in range(n_lag)]
        aj_re, aj_im = _power_rows(pw_re, is_fwd, lag_ks), _power_rows(pw_im, is_fwd, lag_ks)
        bl_re, bl_im = _tile_rows(bb_re, n_lag), _tile_rows(bb_im, n_lag)
        l_re = aj_re * bl_re - aj_im * bl_im
        l_im = aj_re * bl_im + aj_im * bl_re
        cw_re, cw_im = _tile_rows(ct_re, CHUNK), _tile_rows(ct_im, CHUNK)
        kwide = _dot3_nt(jnp.concatenate([l_re, -l_im], axis=1),
                         jnp.concatenate([cw_re, cw_im], axis=1))
        halves = []
        for half in range(2):
            col = kwide[:, half * LANES:(half + 1) * LANES]
            out = None
            for slot in range(SLOTS):
                q = SLOTS * half + (slot - s) % SLOTS
                start = (CHUNK - 1 - q) * S5_GROUP
                win = col[start:start + CHUNK_COLS, :]
                out = win if out is None else jnp.where(slot_of_lane == slot, win, out)
            halves.append(out)
        t_nat = jnp.concatenate(halves, axis=1)
        t_mat = _chunk_row_order(t_nat, s) + jnp.where(eye, dw_ref[s], 0.0)
        t_ref[s] = t_mat.astype(BF16)

        m_ks = [(top - q, q) for q in range(CHUNK)]
        am_re, am_im = _power_rows(pw_re, is_fwd, m_ks), _power_rows(pw_im, is_fwd, m_ks)
        bm_re, bm_im = _tile_rows(bb_re, CHUNK), _tile_rows(bb_im, CHUNK)
        m_nat = jnp.concatenate([am_re * bm_re - am_im * bm_im, am_re * bm_im + am_im * bm_re], axis=1)
        m_ref[s] = _chunk_row_order(m_nat, s).astype(BF16)

        r_ks = [(q + 1, CHUNK - q) for q in range(CHUNK)]
        ar_re, ar_im = _power_rows(pw_re, is_fwd, r_ks), _power_rows(pw_im, is_fwd, r_ks)
        rt_nat = jnp.concatenate([ar_re * cw_re - ar_im * cw_im, -(ar_re * cw_im + ar_im * cw_re)], axis=1)
        r_ref[s] = _chunk_row_order(rt_nat, s).T.astype(BF16)

        a_ref[s] = jnp.concatenate([pw_re[CHUNK:CHUNK + 1, :], pw_im[CHUNK:CHUNK + 1, :]], axis=0)


def _s5_operators(lam_re, lam_im, log_dt, b_re, b_im, c_re, c_im, d_skip):
    half_l = 2 * S5_STATE

    def lanes(v):
        return jnp.transpose(v, (1, 0, 2)).reshape(S5_GROUPS, 1, half_l)

    ldt = lanes(jnp.broadcast_to(log_dt[..., None], lam_re.shape))
    bt_re = jnp.transpose(b_re, (1, 3, 0, 2)).reshape(S5_GROUPS, S5_GROUP, half_l)
    bt_im = jnp.transpose(b_im, (1, 3, 0, 2)).reshape(S5_GROUPS, S5_GROUP, half_l)
    ct_re = jnp.transpose(c_re, (1, 2, 0, 3)).reshape(S5_GROUPS, S5_GROUP, half_l)
    ct_im = jnp.transpose(c_im, (1, 2, 0, 3)).reshape(S5_GROUPS, S5_GROUP, half_l)
    dw = jnp.tile(d_skip, (1, CHUNK)).reshape(S5_GROUPS, 1, CHUNK_COLS)

    gs = GROUPS_PER_STEP
    vec = pl.BlockSpec((gs, 1, half_l), lambda i: (i, 0, 0))
    par = pl.BlockSpec((gs, S5_GROUP, half_l), lambda i: (i, 0, 0))
    mat = pl.BlockSpec((gs, CHUNK_COLS, CHUNK_COLS), lambda i: (i, 0, 0))
    mat_shape = jax.ShapeDtypeStruct((S5_GROUPS, CHUNK_COLS, CHUNK_COLS), BF16)
    return pl.pallas_call(
        _s5_prep_kernel,
        grid=(S5_GROUPS // gs,),
        in_specs=[vec, vec, vec, par, par, par, par,
                  pl.BlockSpec((gs, 1, CHUNK_COLS), lambda i: (i, 0, 0))],
        out_specs=[mat, mat, mat, pl.BlockSpec((gs, 2, half_l), lambda i: (i, 0, 0))],
        out_shape=[mat_shape, mat_shape, mat_shape,
                   jax.ShapeDtypeStruct((S5_GROUPS, 2, half_l), F32)],
        compiler_params=pltpu.CompilerParams(
            dimension_semantics=("parallel",), vmem_limit_bytes=VMEM_LIMIT),
        name="s5_operators",
    )(lanes(lam_re), lanes(lam_im), ldt, bt_re, bt_im, ct_re, ct_im, dw)


CONV_HALF = CONV_K // 2
CONV_LAG = CONV_HALF + 1
RING_SLOTS = 32
CONV_BLOCKS = STEP_ROWS // CONV_ROWS
BF16_ROWS = 16
TAP_GROUP = 8
FF_TILE = 512


def _aligned(x, m):
    return x if isinstance(x, int) else pl.multiple_of(x, m)


def _conv_block(ring, slots, blk, wb_ref, cb_ref, lg_ref, lb_ref, seed=None):
    b = blk // (TOK_ROWS // CONV_ROWS)
    t0 = _aligned((blk % (TOK_ROWS // CONV_ROWS)) * CONV_ROWS, CONV_ROWS)
    packed = (CONV_ROWS // BF16_ROWS, BF16_ROWS, CONV_WIDTH)
    acc = jnp.zeros(packed, F32)
    if seed is not None:
        rows = jnp.concatenate([seed] * (BF16_ROWS // SUBLANES), axis=0)
        acc = acc + jnp.concatenate([rows] * (CONV_WIDTH // LANES), axis=1)[None]
    for k0 in range(0, CONV_K, TAP_GROUP):
        group = None
        for k in range(k0, min(k0 + TAP_GROUP, CONV_K)):
            prod = wb_ref[k] * ring[slots[k], b, pl.ds(t0, CONV_ROWS), :].reshape(packed)
            group = prod if group is None else group + prod
        acc = acc + group.astype(F32)
    acc = acc.reshape(CONV_ROWS, CONV_WIDTH) + cb_ref[...]
    mu = jnp.mean(acc, axis=-1, keepdims=True)
    dev = acc - mu
    var = jnp.mean(dev * dev, axis=-1, keepdims=True)
    y = dev * lax.rsqrt(var + EPS_LN) * lg_ref[...] + lb_ref[...]
    return (y * jax.nn.sigmoid(y)).astype(BF16)


def _out_kernel(n_rows, x_ref, ys_ref, hc_ref, g1_ref, sh2_ref, sc2_ref, g2_ref, n2_ref, fg_ref,
                wb_ref, cb_ref, lg_ref, lb_ref, wglu_in, wout_in, w1_in, w2_in, o_ref,
                u_scr, ring, yc_scr, wglu_ref, wout_ref, w1_ref, w2_ref):
    s = pl.program_id(0)
    zero_row = jnp.zeros((BATCH, TOK_ROWS, CONV_WIDTH), BF16)

    @pl.when(s < CONV_LAG)
    def _():
        for w_in_ref, w_bf in ((wglu_in, wglu_ref), (wout_in, wout_ref), (w1_in, w1_ref), (w2_in, w2_ref)):
            rows = w_in_ref.shape[0]
            w_bf[pl.ds(pl.multiple_of(s * rows, rows), rows), :] = w_in_ref[...].astype(BF16)

    @pl.when(s == 0)
    def _():
        for i in range(RING_SLOTS - CONV_HALF, RING_SLOTS):
            ring[i] = zero_row

    @pl.when(s < n_rows)
    def _():
        ring[s % RING_SLOTS] = hc_ref[...]

    @pl.when(s >= n_rows)
    def _():
        ring[s % RING_SLOTS] = zero_row

    slots = [(s + (RING_SLOTS - 2 * CONV_HALF) + k) % RING_SLOTS for k in range(CONV_K)]
    conv_args = (wb_ref, cb_ref, lg_ref, lb_ref)

    @pl.when(s == CONV_HALF)
    def _():
        def body(blk, carry):
            rows = pl.ds(_aligned(blk * CONV_ROWS, CONV_ROWS), CONV_ROWS)
            yc_scr[rows, :] = _conv_block(ring, slots, blk, *conv_args)
            return carry
        lax.fori_loop(0, CONV_BLOCKS, body, 0, unroll=4)

    @pl.when(s >= CONV_LAG)
    def _():
        yc = yc_scr[...]
        ys = jnp.concatenate([ys_ref[g].astype(F32) for g in range(S5_GROUPS)], axis=1)
        _from_chunk_layout(ys, u_scr)
        y_s5 = jnp.concatenate(
            [jnp.concatenate([u_scr[vb, b * PITCH:b * PITCH + TOK_ROWS, :] for b in range(BATCH)], axis=0)
             for vb in range(LANE_BLOCKS)], axis=1)
        gl = jax.nn.gelu(y_s5)
        hb = BATCH // 2
        hr = hb * TOK_ROWS
        h1_parts, a2_parts = [], []
        for h in range(2):
            rows = slice(h * hr, (h + 1) * hr)
            bs = slice(h * hb, (h + 1) * hb)
            glh = gl[rows]
            s5o = glh * jax.nn.sigmoid(_dot(glh.astype(BF16), wglu_ref[...]))
            mix = _dot(s5o.astype(BF16), wout_ref[:S5_WIDTH, :]) + _dot(yc[rows], wout_ref[S5_WIDTH:, :])
            h1h = x_ref[bs] + g1_ref[bs] * mix.reshape(hb, TOK_ROWS, D_MODEL)
            a2h = _rms(h1h, n2_ref[...]) * (1.0 + sc2_ref[bs]) + sh2_ref[bs]
            h1_parts.append(h1h)
            a2_parts.append(a2h.reshape(hr, D_MODEL).astype(BF16))
        h1 = jnp.concatenate(h1_parts, axis=0)
        a2 = jnp.concatenate(a2_parts, axis=0)
        ff_tile = FF_TILE
        n_ff = D_FF // ff_tile
        per_dot = CONV_BLOCKS // (2 * n_ff)
        never = s < 0

        def seeded_conv(first_blk, mat):
            for i in range(per_dot):
                blk = first_blk + i
                r = i * (STEP_ROWS // per_dot)
                seed = jnp.where(never, mat[r:r + SUBLANES, 0:LANES], 0.0)
                yc_scr[blk * CONV_ROWS:(blk + 1) * CONV_ROWS, :] = _conv_block(
                    ring, slots, blk, *conv_args, seed=seed)

        acc = jnp.zeros((STEP_ROWS, D_MODEL), F32)
        for j in range(n_ff):
            up = _dot(a2, w1_ref[:, j * ff_tile:(j + 1) * ff_tile])
            seeded_conv(2 * j * per_dot, up)
            hid = jnp.maximum(up, 0.0)
            down = _dot((hid * hid).astype(BF16), w2_ref[j * ff_tile:(j + 1) * ff_tile, :])
            seeded_conv((2 * j + 1) * per_dot, down)
            acc = acc + down
        h2 = h1 + g2_ref[...] * acc.reshape(BATCH, TOK_ROWS, D_MODEL)
        o_ref[...] = _rms(h2, fg_ref[...])


def _out_block(x, ys, hc, g1, sh2, sc2, g2, n2, fg, wb, cb, lg, lb, wglu, wout, w1, w2):
    bsz, n, _ = x.shape
    assert bsz == BATCH and n % TOK_ROWS == 0
    n_rows = n // TOK_ROWS
    chunk_rows = STEP_CHUNKS * BATCH
    modv = pl.BlockSpec((BATCH, 1, D_MODEL), lambda s: (0, 0, 0))
    vec = pl.BlockSpec((1, D_MODEL), lambda s: (0, 0))
    cvec = pl.BlockSpec((1, CONV_WIDTH), lambda s: (0, 0))

    def const(shape):
        return pl.BlockSpec(shape, lambda s: (0,) * len(shape), pipeline_mode=pl.Buffered(1))

    def lagged(s):
        return jnp.maximum(s - CONV_LAG, 0)

    def staged(w):
        return pl.BlockSpec((w.shape[0] // CONV_LAG, w.shape[1]), lambda s: (jnp.minimum(s, CONV_LAG - 1), 0))

    return pl.pallas_call(
        functools.partial(_out_kernel, n_rows),
        grid=(n_rows + CONV_LAG,),
        in_specs=[
            pl.BlockSpec((BATCH, TOK_ROWS, D_MODEL), lambda s: (0, lagged(s), 0)),
            pl.BlockSpec((S5_GROUPS, chunk_rows, CHUNK_COLS), lambda s: (0, lagged(s), 0)),
            pl.BlockSpec((BATCH, TOK_ROWS, CONV_WIDTH), lambda s: (0, jnp.minimum(s, n_rows - 1), 0)),
            modv, modv, modv, modv, vec, vec,
            const((CONV_K, BF16_ROWS, CONV_WIDTH)), cvec, cvec, cvec,
            staged(wglu), staged(wout), staged(w1), staged(w2),
        ],
        out_specs=pl.BlockSpec((BATCH, TOK_ROWS, D_MODEL), lambda s: (0, lagged(s), 0)),
        out_shape=jax.ShapeDtypeStruct((bsz, n, D_MODEL), F32),
        scratch_shapes=[
            pltpu.VMEM((LANE_BLOCKS, BATCH * PITCH, LANES), F32),
            pltpu.VMEM((RING_SLOTS, BATCH, TOK_ROWS, CONV_WIDTH), BF16),
            pltpu.VMEM((STEP_ROWS, CONV_WIDTH), BF16),
            pltpu.VMEM(wglu.shape, BF16), pltpu.VMEM(wout.shape, BF16),
            pltpu.VMEM(w1.shape, BF16), pltpu.VMEM(w2.shape, BF16),
        ],
        compiler_params=pltpu.CompilerParams(
            dimension_semantics=("arbitrary",), vmem_limit_bytes=OUT_VMEM_LIMIT),
        name="out_block",
    )(x, ys, hc, g1, sh2, sc2, g2, n2, fg, wb, cb, lg, lb, wglu, wout, w1, w2)


def kernel(x, c, ctx, c_ctx, ada_w, ada_b, norm1_g, w_in, s5_lam_re, s5_lam_im, s5_log_dt, s5_b_re, s5_b_im, s5_c_re, s5_c_im, s5_d, s5_w_glu, conv_w, conv_b, conv_ln_g, conv_ln_b, w_out, norm2_g, mlp_w1, mlp_w2, final_g):
    bsz = x.shape[0]
    assert bsz == BATCH and bsz + 1 <= MOD_ROWS
    layer = 0

    cc = jnp.concatenate([c, c_ctx[None, :], jnp.zeros((MOD_ROWS - bsz - 1, D_MODEL), F32)], axis=0)
    mod = _mod_table(cc, ada_w[layer], ada_b[layer])
    sh1, sc1, g1, sh2, sc2, g2 = [m.reshape(bsz, 1, D_MODEL) for m in jnp.split(mod[:bsz], 6, axis=-1)]
    csh1 = jnp.broadcast_to(mod[bsz, :D_MODEL].reshape(1, 1, D_MODEL), (bsz, 1, D_MODEL))
    csc1 = jnp.broadcast_to(mod[bsz, D_MODEL:2 * D_MODEL].reshape(1, 1, D_MODEL), (bsz, 1, D_MODEL))

    n1 = norm1_g[layer].reshape(1, D_MODEL)
    x_lat, hc = _in_proj(x, sh1, sc1, n1, w_in[layer], True)
    x_ctx, = _in_proj(ctx, csh1, csc1, n1, w_in[layer], False)

    t_mat, m_mat, r_mat, a_pow = _s5_operators(
        s5_lam_re[layer], s5_lam_im[layer], s5_log_dt[layer], s5_b_re[layer], s5_b_im[layer],
        s5_c_re[layer], s5_c_im[layer], s5_d[layer])
    y_s5 = _s5_scan(x_lat, x_ctx, t_mat, m_mat, r_mat, a_pow, bsz)

    conv_wb = jnp.broadcast_to(conv_w[layer].astype(BF16)[:, None, :], (CONV_K, BF16_ROWS, CONV_WIDTH))
    return _out_block(
        x, y_s5, hc, g1, sh2, sc2, g2,
        norm2_g[layer].reshape(1, D_MODEL), final_g.reshape(1, D_MODEL),
        conv_wb, conv_b[layer].reshape(1, CONV_WIDTH),
        conv_ln_g[layer].reshape(1, CONV_WIDTH), conv_ln_b[layer].reshape(1, CONV_WIDTH),
        s5_w_glu[layer], w_out[layer], mlp_w1[layer], mlp_w2[layer])
```

```python
import functools

import jax
import jax.numpy as jnp
from jax import lax
from jax.experimental import pallas as pl
from jax.experimental.pallas import tpu as pltpu

F32 = jnp.float32
BF16 = jnp.bfloat16

D_MODEL = 1024
S5_WIDTH = 512
S5_GROUP = 16
S5_GROUPS = 32
S5_STATE = 64
CONV_WIDTH = 512
CONV_K = 31
GRID_W = 64
IN_COLS = S5_WIDTH + 2 * CONV_WIDTH
D_FF = 4 * D_MODEL
EPS_RMS = 1e-6
EPS_LN = 1e-5

CHUNK = 16
CHUNK_COLS = CHUNK * S5_GROUP
MOD_ROWS = 16
BATCH = 8
TOK_ROWS = 64
STEP_ROWS = BATCH * TOK_ROWS
STEP_CHUNKS = TOK_ROWS // CHUNK
LANES = 128
SUBLANES = 8
SLOTS = LANES // S5_GROUP
LANE_BLOCKS = S5_WIDTH // LANES
PITCH = 72
IN_TOK_ROWS = 128
IN_PITCH = 136
CONV_ROWS = 16
VMEM_LIMIT = 56 * 1024 * 1024
OUT_VMEM_LIMIT = 60 * 1024 * 1024


def _dot(a, b):
    return jnp.dot(a, b, preferred_element_type=F32)


def _rms(x, g):
    return x * lax.rsqrt(jnp.mean(x * x, axis=-1, keepdims=True) + EPS_RMS) * g


def _mod_kernel(c_ref, w_ref, b_ref, o_ref):
    c = c_ref[...]
    s = c * jax.nn.sigmoid(c)
    w = w_ref[...]
    s_hi = s.astype(BF16)
    s_lo = (s - s_hi.astype(F32)).astype(BF16)
    w_hi = w.astype(BF16)
    w_lo = (w - w_hi.astype(F32)).astype(BF16)
    o_ref[...] = _dot(s_hi, w_hi) + _dot(s_lo, w_hi) + _dot(s_hi, w_lo) + b_ref[...]


def _mod_table(cc, ada_w, ada_b):
    n_out = ada_w.shape[1]
    tn = 1536
    return pl.pallas_call(
        _mod_kernel,
        grid=(n_out // tn,),
        in_specs=[
            pl.BlockSpec((MOD_ROWS, D_MODEL), lambda j: (0, 0)),
            pl.BlockSpec((D_MODEL, tn), lambda j: (0, j)),
            pl.BlockSpec((1, tn), lambda j: (0, j)),
        ],
        out_specs=pl.BlockSpec((MOD_ROWS, tn), lambda j: (0, j)),
        out_shape=jax.ShapeDtypeStruct((MOD_ROWS, n_out), F32),
        compiler_params=pltpu.CompilerParams(vmem_limit_bytes=VMEM_LIMIT),
        name="mod_table",
    )(cc, ada_w, ada_b.reshape(1, n_out))


def _slot_skew(vregs):
    slot = lax.broadcasted_iota(jnp.int32, (BATCH, LANES), 1) // S5_GROUP
    cur = list(vregs)
    for b in range(3):
        bit_set = (slot & (1 << b)) != 0
        cur = [jnp.where(bit_set, cur[(t + (1 << b)) % SLOTS], cur[t]) for t in range(SLOTS)]
    return cur


def _to_chunk_layout(z_scr, xs_scr, n_chunks, pitch):
    for cl in range(n_chunks):
        for vb in range(LANE_BLOCKS):
            for half in range(2):
                rot = []
                for ql in range(SLOTS):
                    piece = z_scr[vb, pl.ds(cl * CHUNK + SLOTS * half + ql, BATCH, stride=pitch), :]
                    rot.append(pltpu.roll(piece, ql * S5_GROUP, axis=1) if ql else piece)
                skew = _slot_skew(rot)
                for s in range(SLOTS):
                    col = (2 * (SLOTS * vb + s) + half) * LANES
                    xs_scr[cl * BATCH:(cl + 1) * BATCH, col:col + LANES] = skew[(-s) % SLOTS]


def _from_chunk_layout(ys, u_scr):
    for cl in range(STEP_CHUNKS):
        for vb in range(LANE_BLOCKS):
            for half in range(2):
                src = []
                for s in range(SLOTS):
                    col = (2 * (SLOTS * vb + s) + half) * LANES
                    src.append(ys[cl * BATCH:(cl + 1) * BATCH, col:col + LANES])
                skew = _slot_skew(src)
                for ql in range(SLOTS):
                    out = skew[(-ql) % SLOTS]
                    if ql:
                        out = pltpu.roll(out, LANES - ql * S5_GROUP, axis=1)
                    u_scr[vb, pl.ds(cl * CHUNK + SLOTS * half + ql, BATCH, stride=PITCH), :] = out


def _inproj_kernel(with_conv, x_ref, sh_ref, sc_ref, g_ref, w_ref, xs_ref, *rest):
    hc_ref = rest[0] if with_conv else None
    z_scr, xs_scr, w_bf = rest[-3:]

    @pl.when(pl.program_id(0) == 0)
    def _():
        w_bf[...] = w_ref[...].astype(BF16)

    tok = IN_TOK_ROWS
    x = x_ref[...]
    a = _rms(x, g_ref[...]) * (1.0 + sc_ref[...]) + sh_ref[...]
    z = _dot(a.reshape(BATCH * tok, D_MODEL).astype(BF16), w_bf[...])
    if with_conv:
        v = z[:, S5_WIDTH:S5_WIDTH + CONV_WIDTH]
        gate = z[:, S5_WIDTH + CONV_WIDTH:]
        hc_ref[...] = (v * jax.nn.sigmoid(gate)).astype(BF16).reshape(BATCH, tok, CONV_WIDTH)
    for vb in range(LANE_BLOCKS):
        for b in range(BATCH):
            z_scr[vb, b * IN_PITCH:b * IN_PITCH + tok, :] = z[b * tok:(b + 1) * tok, vb * LANES:(vb + 1) * LANES]
    _to_chunk_layout(z_scr, xs_scr, tok // CHUNK, IN_PITCH)
    for g in range(S5_GROUPS):
        xs_ref[g] = xs_scr[:, g * CHUNK_COLS:(g + 1) * CHUNK_COLS].astype(BF16)


def _in_proj(x, sh, sc, g, w_in, with_conv):
    bsz, n, _ = x.shape
    tok = IN_TOK_ROWS
    assert bsz == BATCH and n % tok == 0
    steps = n // tok
    chunk_rows = (tok // CHUNK) * BATCH
    n_cols = IN_COLS if with_conv else S5_WIDTH
    modv = pl.BlockSpec((BATCH, 1, D_MODEL), lambda i: (0, 0, 0))
    out_specs = [pl.BlockSpec((S5_GROUPS, chunk_rows, CHUNK_COLS), lambda i: (0, i, 0))]
    out_shape = [jax.ShapeDtypeStruct((S5_GROUPS, steps * chunk_rows, CHUNK_COLS), BF16)]
    if with_conv:
        out_specs.append(pl.BlockSpec((BATCH, tok, CONV_WIDTH), lambda i: (0, i, 0)))
        out_shape.append(jax.ShapeDtypeStruct((bsz, n, CONV_WIDTH), BF16))
    return pl.pallas_call(
        functools.partial(_inproj_kernel, with_conv),
        grid=(steps,),
        in_specs=[
            pl.BlockSpec((BATCH, tok, D_MODEL), lambda i: (0, i, 0)),
            modv, modv,
            pl.BlockSpec((1, D_MODEL), lambda i: (0, 0)),
            pl.BlockSpec((D_MODEL, n_cols), lambda i: (0, 0), pipeline_mode=pl.Buffered(1)),
        ],
        out_specs=out_specs,
        out_shape=out_shape,
        scratch_shapes=[
            pltpu.VMEM((LANE_BLOCKS, BATCH * IN_PITCH, LANES), F32),
            pltpu.VMEM((chunk_rows, S5_GROUPS * CHUNK_COLS), F32),
            pltpu.VMEM((D_MODEL, n_cols), BF16),
        ],
        compiler_params=pltpu.CompilerParams(
            dimension_semantics=("arbitrary",), vmem_limit_bytes=VMEM_LIMIT),
        name="in_proj" if with_conv else "in_proj_ctx",
    )(x, sh, sc, g, w_in)


SCAN_GROUPS = 4


def _mirror_bwd(a, n_chunks):
    rev = jnp.concatenate(
        [a[(n_chunks - 1 - c) * BATCH:(n_chunks - c) * BATCH, :] for c in range(n_chunks)], axis=0)
    lane = lax.broadcasted_iota(jnp.int32, a.shape, 1)
    return jnp.where((lane % (2 * S5_STATE)) < S5_STATE, a, rev)


def _s5_kernel(n_lat, n_ctx, xl_ref, xc_ref, t_ref, m_ref, r_ref, a_ref, y_ref, e_ref, p_ref, ec_ref):
    half = 2 * S5_STATE
    groups = range(SCAN_GROUPS)
    for g in groups:
        ec_ref[g] = _mirror_bwd(_dot(xc_ref[g], m_ref[g]), n_ctx)
        e_ref[g] = _mirror_bwd(_dot(xl_ref[g], m_ref[g]), n_lat)
    a_re = [jnp.broadcast_to(a_ref[g, 0:1, :], (BATCH, half)) for g in groups]
    a_im = [jnp.broadcast_to(a_ref[g, 1:2, :], (BATCH, half)) for g in groups]

    def advance(src_ref, g, rows, s_re, s_im):
        e = src_ref[g, rows, :]
        return (a_re[g] * s_re - a_im[g] * s_im + e[:, :half], a_re[g] * s_im + a_im[g] * s_re + e[:, half:])

    def ctx_body(i, s):
        rows = pl.ds(pl.multiple_of(i * BATCH, BATCH), BATCH)
        return tuple(advance(ec_ref, g, rows, *s[g]) for g in groups)

    zero = jnp.zeros((BATCH, half), F32)
    s = lax.fori_loop(0, n_ctx, ctx_body, tuple((zero, zero) for _ in groups))

    def lat_body(i, s):
        rows = pl.ds(pl.multiple_of(i * BATCH, BATCH), BATCH)
        out = []
        for g in groups:
            s_re, s_im = s[g]
            p_ref[g, rows, :half] = s_re
            p_ref[g, rows, half:] = s_im
            out.append(advance(e_ref, g, rows, s_re, s_im))
        return tuple(out)

    lax.fori_loop(0, n_lat, lat_body, s)
    for g in groups:
        p = _mirror_bwd(p_ref[g], n_lat)
        y = _dot(xl_ref[g], t_ref[g]) + _dot(p.astype(BF16), r_ref[g])
        y_ref[g] = y.astype(BF16)


def _s5_scan(x_lat, x_ctx, t_mat, m_mat, r_mat, a_pow, bsz):
    rows = x_lat.shape[1]
    rows_c = x_ctx.shape[1]
    gs = SCAN_GROUPS

    def per_group(*shape):
        return pl.BlockSpec((gs,) + shape, lambda i: (i,) + (0,) * len(shape))

    mat_spec = per_group(CHUNK_COLS, CHUNK_COLS)
    return pl.pallas_call(
        functools.partial(_s5_kernel, rows // bsz, rows_c // bsz),
        grid=(S5_GROUPS // gs,),
        in_specs=[
            per_group(rows, CHUNK_COLS), per_group(rows_c, CHUNK_COLS),
            mat_spec, mat_spec, mat_spec,
            per_group(2, 2 * S5_STATE),
        ],
        out_specs=per_group(rows, CHUNK_COLS),
        out_shape=jax.ShapeDtypeStruct((S5_GROUPS, rows, CHUNK_COLS), BF16),
        scratch_shapes=[
            pltpu.VMEM((gs, rows, CHUNK_COLS), F32),
            pltpu.VMEM((gs, rows, CHUNK_COLS), F32),
            pltpu.VMEM((gs, rows_c, CHUNK_COLS), F32),
        ],
        compiler_params=pltpu.CompilerParams(
            dimension_semantics=("parallel",), vmem_limit_bytes=VMEM_LIMIT),
        name="s5_scan",
    )(x_lat, x_ctx, t_mat, m_mat, r_mat, a_pow)


GROUPS_PER_STEP = SLOTS


def _cpow(kk, log_mag, ang):
    mag = jnp.exp(kk * log_mag)
    return mag * jnp.cos(kk * ang), mag * jnp.sin(kk * ang)


def _power_rows(pw, is_fwd, ks):
    zero = jnp.zeros((1, pw.shape[1]), F32)
    blocks = []
    for kf, kb in ks:
        row = jnp.where(is_fwd, zero if kf is None else pw[kf:kf + 1, :], zero if kb is None else pw[kb:kb + 1, :])
        blocks.append(jnp.broadcast_to(row, (S5_GROUP, pw.shape[1])))
    return jnp.concatenate(blocks, axis=0)


def _tile_rows(a, reps):
    return jnp.concatenate([a] * reps, axis=0)


def _dot3_nt(a, b):
    dn = (((1,), (1,)), ((), ()))
    a_hi = a.astype(BF16)
    a_lo = (a - a_hi.astype(F32)).astype(BF16)
    b_hi = b.astype(BF16)
    b_lo = (b - b_hi.astype(F32)).astype(BF16)

    def nt(u, v):
        return lax.dot_general(u, v, dn, preferred_element_type=F32)

    return nt(a_hi, b_hi) + nt(a_lo, b_hi) + nt(a_hi, b_lo)


def _chunk_row_order(a, s):
    blocks = []
    for half in range(2):
        for slot in range(SLOTS):
            q = SLOTS * half + (slot - s) % SLOTS
            blocks.append(a[q * S5_GROUP:(q + 1) * S5_GROUP, :])
    return jnp.concatenate(blocks, axis=0)


def _s5_prep_kernel(lre_ref, lim_ref, ldt_ref, btr_ref, bti_ref, ctr_ref, cti_ref, dw_ref,
                    t_ref, m_ref, r_ref, a_ref):
    half_l = 2 * S5_STATE
    n_lag = 2 * CHUNK - 1
    is_fwd = lax.broadcasted_iota(jnp.int32, (1, half_l), 1) < S5_STATE
    n_pow = 24
    k_row = lax.broadcasted_iota(jnp.int32, (n_pow, half_l), 0).astype(F32)
    top = CHUNK - 1
    slot_of_lane = lax.broadcasted_iota(jnp.int32, (CHUNK_COLS, LANES), 1) // S5_GROUP
    eye = (lax.broadcasted_iota(jnp.int32, (CHUNK_COLS, CHUNK_COLS), 0)
           == lax.broadcasted_iota(jnp.int32, (CHUNK_COLS, CHUNK_COLS), 1))

    for s in range(GROUPS_PER_STEP):
        lre, lim = lre_ref[s], lim_ref[s]
        dt = jnp.exp(ldt_ref[s])
        log_mag, ang = lre * dt, lim * dt
        pw_re, pw_im = _cpow(k_row, log_mag, ang)
        a1_re, a1_im = pw_re[1:2, :], pw_im[1:2, :]
        den = lre * lre + lim * lim
        num_re = a1_re - 1.0
        f_re = (num_re * lre + a1_im * lim) / den
        f_im = (a1_im * lre - num_re * lim) / den
        bt_re, bt_im = btr_ref[s], bti_ref[s]
        bb_re = f_re * bt_re - f_im * bt_im
        bb_im = f_re * bt_im + f_im * bt_re
        ct_re, ct_im = ctr_ref[s], cti_ref[s]

        lag_ks = [(top - j if j <= top else None, j - top if j >= top else None) for j in range(n_lag)]
        aj_re, aj_im = _power_rows(pw_re, is_fwd, lag_ks), _power_rows(pw_im, is_fwd, lag_ks)
        bl_re, bl_im = _tile_rows(bb_re, n_lag), _tile_rows(bb_im, n_lag)
        l_re = aj_re * bl_re - aj_im * bl_im
        l_im = aj_re * bl_im + aj_im * bl_re
        cw_re, cw_im = _tile_rows(ct_re, CHUNK), _tile_rows(ct_im, CHUNK)
        kwide = _dot3_nt(jnp.concatenate([l_re, -l_im], axis=1),
                         jnp.concatenate([cw_re, cw_im], axis=1))
        halves = []
        for half in range(2):
            col = kwide[:, half * LANES:(half + 1) * LANES]
            out = None
            for slot in range(SLOTS):
                q = SLOTS * half + (slot - s) % SLOTS
                start = (CHUNK - 1 - q) * S5_GROUP
                win = col[start:start + CHUNK_COLS, :]
                out = win if out is None else jnp.where(slot_of_lane == slot, win, out)
            halves.append(out)
        t_nat = jnp.concatenate(halves, axis=1)
        t_mat = _chunk_row_order(t_nat, s) + jnp.where(eye, dw_ref[s], 0.0)
        t_ref[s] = t_mat.astype(BF16)

        m_ks = [(top - q, q) for q in range(CHUNK)]
        am_re, am_im = _power_rows(pw_re, is_fwd, m_ks), _power_rows(pw_im, is_fwd, m_ks)
        bm_re, bm_im = _tile_rows(bb_re, CHUNK), _tile_rows(bb_im, CHUNK)
        m_nat = jnp.concatenate([am_re * bm_re - am_im * bm_im, am_re * bm_im + am_im * bm_re], axis=1)
        m_ref[s] = _chunk_row_order(m_nat, s).astype(BF16)

        r_ks = [(q + 1, CHUNK - q) for q in range(CHUNK)]
        ar_re, ar_im = _power_rows(pw_re, is_fwd, r_ks), _power_rows(pw_im, is_fwd, r_ks)
        rt_nat = jnp.concatenate([ar_re * cw_re - ar_im * cw_im, -(ar_re * cw_im + ar_im * cw_re)], axis=1)
        r_ref[s] = _chunk_row_order(rt_nat, s).T.astype(BF16)

        a_ref[s] = jnp.concatenate([pw_re[CHUNK:CHUNK + 1, :], pw_im[CHUNK:CHUNK + 1, :]], axis=0)


def _s5_operators(lam_re, lam_im, log_dt, b_re, b_im, c_re, c_im, d_skip):
    half_l = 2 * S5_STATE

    def lanes(v):
        return jnp.transpose(v, (1, 0, 2)).reshape(S5_GROUPS, 1, half_l)

    ldt = lanes(jnp.broadcast_to(log_dt[..., None], lam_re.shape))
    bt_re = jnp.transpose(b_re, (1, 3, 0, 2)).reshape(S5_GROUPS, S5_GROUP, half_l)
    bt_im = jnp.transpose(b_im, (1, 3, 0, 2)).reshape(S5_GROUPS, S5_GROUP, half_l)
    ct_re = jnp.transpose(c_re, (1, 2, 0, 3)).reshape(S5_GROUPS, S5_GROUP, half_l)
    ct_im = jnp.transpose(c_im, (1, 2, 0, 3)).reshape(S5_GROUPS, S5_GROUP, half_l)
    dw = jnp.tile(d_skip, (1, CHUNK)).reshape(S5_GROUPS, 1, CHUNK_COLS)

    gs = GROUPS_PER_STEP
    vec = pl.BlockSpec((gs, 1, half_l), lambda i: (i, 0, 0))
    par = pl.BlockSpec((gs, S5_GROUP, half_l), lambda i: (i, 0, 0))
    mat = pl.BlockSpec((gs, CHUNK_COLS, CHUNK_COLS), lambda i: (i, 0, 0))
    mat_shape = jax.ShapeDtypeStruct((S5_GROUPS, CHUNK_COLS, CHUNK_COLS), BF16)
    return pl.pallas_call(
        _s5_prep_kernel,
        grid=(S5_GROUPS // gs,),
        in_specs=[vec, vec, vec, par, par, par, par,
                  pl.BlockSpec((gs, 1, CHUNK_COLS), lambda i: (i, 0, 0))],
        out_specs=[mat, mat, mat, pl.BlockSpec((gs, 2, half_l), lambda i: (i, 0, 0))],
        out_shape=[mat_shape, mat_shape, mat_shape,
                   jax.ShapeDtypeStruct((S5_GROUPS, 2, half_l), F32)],
        compiler_params=pltpu.CompilerParams(
            dimension_semantics=("parallel",), vmem_limit_bytes=VMEM_LIMIT),
        name="s5_operators",
    )(lanes(lam_re), lanes(lam_im), ldt, bt_re, bt_im, ct_re, ct_im, dw)


CONV_HALF = CONV_K // 2
CONV_LAG = CONV_HALF + 1
RING_SLOTS = 32
CONV_BLOCKS = STEP_ROWS // CONV_ROWS
BF16_ROWS = 16
TAP_GROUP = 8
FF_TILE = 512


def _aligned(x, m):
    return x if isinstance(x, int) else pl.multiple_of(x, m)


def _conv_block(ring, slots, blk, wb_ref, cb_ref, lg_ref, lb_ref, seed=None):
    b = blk // (TOK_ROWS // CONV_ROWS)
    t0 = _aligned((blk % (TOK_ROWS // CONV_ROWS)) * CONV_ROWS, CONV_ROWS)
    packed = (CONV_ROWS // BF16_ROWS, BF16_ROWS, CONV_WIDTH)
    acc = jnp.zeros(packed, F32)
    if seed is not None:
        rows = jnp.concatenate([seed] * (BF16_ROWS // SUBLANES), axis=0)
        acc = acc + jnp.concatenate([rows] * (CONV_WIDTH // LANES), axis=1)[None]
    for k0 in range(0, CONV_K, TAP_GROUP):
        group = None
        for k in range(k0, min(k0 + TAP_GROUP, CONV_K)):
            prod = wb_ref[k] * ring[slots[k], b, pl.ds(t0, CONV_ROWS), :].reshape(packed)
            group = prod if group is None else group + prod
        acc = acc + group.astype(F32)
    acc = acc.reshape(CONV_ROWS, CONV_WIDTH) + cb_ref[...]
    mu = jnp.mean(acc, axis=-1, keepdims=True)
    dev = acc - mu
    var = jnp.mean(dev * dev, axis=-1, keepdims=True)
    y = dev * lax.rsqrt(var + EPS_LN) * lg_ref[...] + lb_ref[...]
    return (y * jax.nn.sigmoid(y)).astype(BF16)


def _out_kernel(n_rows, x_ref, ys_ref, hc_ref, g1_ref, sh2_ref, sc2_ref, g2_ref, n2_ref, fg_ref,
                wb_ref, cb_ref, lg_ref, lb_ref, wglu_in, wout_in, w1_in, w2_in, o_ref,
                u_scr, ring, yc_scr, wglu_ref, wout_ref, w1_ref, w2_ref):
    s = pl.program_id(0)
    zero_row = jnp.zeros((BATCH, TOK_ROWS, CONV_WIDTH), BF16)

    @pl.when(s < CONV_LAG)
    def _():
        for w_in_ref, w_bf in ((wglu_in, wglu_ref), (wout_in, wout_ref), (w1_in, w1_ref), (w2_in, w2_ref)):
            rows = w_in_ref.shape[0]
            w_bf[pl.ds(pl.multiple_of(s * rows, rows), rows), :] = w_in_ref[...].astype(BF16)

    @pl.when(s == 0)
    def _():
        for i in range(RING_SLOTS - CONV_HALF, RING_SLOTS):
            ring[i] = zero_row

    @pl.when(s < n_rows)
    def _():
        ring[s % RING_SLOTS] = hc_ref[...]

    @pl.when(s >= n_rows)
    def _():
        ring[s % RING_SLOTS] = zero_row

    slots = [(s + (RING_SLOTS - 2 * CONV_HALF) + k) % RING_SLOTS for k in range(CONV_K)]
    conv_args = (wb_ref, cb_ref, lg_ref, lb_ref)

    @pl.when(s == CONV_HALF)
    def _():
        def body(blk, carry):
            rows = pl.ds(_aligned(blk * CONV_ROWS, CONV_ROWS), CONV_ROWS)
            yc_scr[rows, :] = _conv_block(ring, slots, blk, *conv_args)
            return carry
        lax.fori_loop(0, CONV_BLOCKS, body, 0, unroll=4)

    @pl.when(s >= CONV_LAG)
    def _():
        yc = yc_scr[...]
        ys = jnp.concatenate([ys_ref[g].astype(F32) for g in range(S5_GROUPS)], axis=1)
        _from_chunk_layout(ys, u_scr)
        y_s5 = jnp.concatenate(
            [jnp.concatenate([u_scr[vb, b * PITCH:b * PITCH + TOK_ROWS, :] for b in range(BATCH)], axis=0)
             for vb in range(LANE_BLOCKS)], axis=1)
        gl = jax.nn.gelu(y_s5)
        s5o = gl * jax.nn.sigmoid(_dot(gl.astype(BF16), wglu_ref[...]))
        mix = _dot(s5o.astype(BF16), wout_ref[:S5_WIDTH, :]) + _dot(yc, wout_ref[S5_WIDTH:, :])
        h1 = x_ref[...] + g1_ref[...] * mix.reshape(BATCH, TOK_ROWS, D_MODEL)
        a2 = _rms(h1, n2_ref[...]) * (1.0 + sc2_ref[...]) + sh2_ref[...]
        a2 = a2.reshape(STEP_ROWS, D_MODEL).astype(BF16)
        ff_tile = FF_TILE
        n_ff = D_FF // ff_tile
        per_dot = CONV_BLOCKS // (2 * n_ff)
        never = s < 0

        def seeded_conv(first_blk, mat):
            for i in range(per_dot):
                blk = first_blk + i
                r = i * (STEP_ROWS // per_dot)
                seed = jnp.where(never, mat[r:r + SUBLANES, 0:LANES], 0.0)
                yc_scr[blk * CONV_ROWS:(blk + 1) * CONV_ROWS, :] = _conv_block(
                    ring, slots, blk, *conv_args, seed=seed)

        acc = jnp.zeros((STEP_ROWS, D_MODEL), F32)
        for j in range(n_ff):
            up = _dot(a2, w1_ref[:, j * ff_tile:(j + 1) * ff_tile])
            seeded_conv(2 * j * per_dot, up)
            hid = jnp.maximum(up, 0.0)
            down = _dot((hid * hid).astype(BF16), w2_ref[j * ff_tile:(j + 1) * ff_tile, :])
            seeded_conv((2 * j + 1) * per_dot, down)
            acc = acc + down
        h2 = h1 + g2_ref[...] * acc.reshape(BATCH, TOK_ROWS, D_MODEL)
        o_ref[...] = _rms(h2, fg_ref[...])


def _out_block(x, ys, hc, g1, sh2, sc2, g2, n2, fg, wb, cb, lg, lb, wglu, wout, w1, w2):
    bsz, n, _ = x.shape
    assert bsz == BATCH and n % TOK_ROWS == 0
    n_rows = n // TOK_ROWS
    chunk_rows = STEP_CHUNKS * BATCH
    modv = pl.BlockSpec((BATCH, 1, D_MODEL), lambda s: (0, 0, 0))
    vec = pl.BlockSpec((1, D_MODEL), lambda s: (0, 0))
    cvec = pl.BlockSpec((1, CONV_WIDTH), lambda s: (0, 0))

    def const(shape):
        return pl.BlockSpec(shape, lambda s: (0,) * len(shape), pipeline_mode=pl.Buffered(1))

    def lagged(s):
        return jnp.maximum(s - CONV_LAG, 0)

    def staged(w):
        return pl.BlockSpec((w.shape[0] // CONV_LAG, w.shape[1]), lambda s: (jnp.minimum(s, CONV_LAG - 1), 0))

    return pl.pallas_call(
        functools.partial(_out_kernel, n_rows),
        grid=(n_rows + CONV_LAG,),
        in_specs=[
            pl.BlockSpec((BATCH, TOK_ROWS, D_MODEL), lambda s: (0, lagged(s), 0)),
            pl.BlockSpec((S5_GROUPS, chunk_rows, CHUNK_COLS), lambda s: (0, lagged(s), 0)),
            pl.BlockSpec((BATCH, TOK_ROWS, CONV_WIDTH), lambda s: (0, jnp.minimum(s, n_rows - 1), 0)),
            modv, modv, modv, modv, vec, vec,
            const((CONV_K, BF16_ROWS, CONV_WIDTH)), cvec, cvec, cvec,
            staged(wglu), staged(wout), staged(w1), staged(w2),
        ],
        out_specs=pl.BlockSpec((BATCH, TOK_ROWS, D_MODEL), lambda s: (0, lagged(s), 0)),
        out_shape=jax.ShapeDtypeStruct((bsz, n, D_MODEL), F32),
        scratch_shapes=[
            pltpu.VMEM((LANE_BLOCKS, BATCH * PITCH, LANES), F32),
            pltpu.VMEM((RING_SLOTS, BATCH, TOK_ROWS, CONV_WIDTH), BF16),
            pltpu.VMEM((STEP_ROWS, CONV_WIDTH), BF16),
            pltpu.VMEM(wglu.shape, BF16), pltpu.VMEM(wout.shape, BF16),
            pltpu.VMEM(w1.shape, BF16), pltpu.VMEM(w2.shape, BF16),
        ],
        compiler_params=pltpu.CompilerParams(
            dimension_semantics=("arbitrary",), vmem_limit_bytes=OUT_VMEM_LIMIT),
        name="out_block",
    )(x, ys, hc, g1, sh2, sc2, g2, n2, fg, wb, cb, lg, lb, wglu, wout, w1, w2)


def kernel(x, c, ctx, c_ctx, ada_w, ada_b, norm1_g, w_in, s5_lam_re, s5_lam_im, s5_log_dt, s5_b_re, s5_b_im, s5_c_re, s5_c_im, s5_d, s5_w_glu, conv_w, conv_b, conv_ln_g, conv_ln_b, w_out, norm2_g, mlp_w1, mlp_w2, final_g):
    bsz = x.shape[0]
    assert bsz == BATCH and bsz + 1 <= MOD_ROWS
    layer = 0

    cc = jnp.concatenate([c, c_ctx[None, :], jnp.zeros((MOD_ROWS - bsz - 1, D_MODEL), F32)], axis=0)
    mod = _mod_table(cc, ada_w[layer], ada_b[layer])
    sh1, sc1, g1, sh2, sc2, g2 = [m.reshape(bsz, 1, D_MODEL) for m in jnp.split(mod[:bsz], 6, axis=-1)]
    csh1 = jnp.broadcast_to(mod[bsz, :D_MODEL].reshape(1, 1, D_MODEL), (bsz, 1, D_MODEL))
    csc1 = jnp.broadcast_to(mod[bsz, D_MODEL:2 * D_MODEL].reshape(1, 1, D_MODEL), (bsz, 1, D_MODEL))

    n1 = norm1_g[layer].reshape(1, D_MODEL)
    x_lat, hc = _in_proj(x, sh1, sc1, n1, w_in[layer], True)
    x_ctx, = _in_proj(ctx, csh1, csc1, n1, w_in[layer], False)

    t_mat, m_mat, r_mat, a_pow = _s5_operators(
        s5_lam_re[layer], s5_lam_im[layer], s5_log_dt[layer], s5_b_re[layer], s5_b_im[layer],
        s5_c_re[layer], s5_c_im[layer], s5_d[layer])
    y_s5 = _s5_scan(x_lat, x_ctx, t_mat, m_mat, r_mat, a_pow, bsz)

    conv_wb = jnp.broadcast_to(conv_w[layer].astype(BF16)[:, None, :], (CONV_K, BF16_ROWS, CONV_WIDTH))
    return _out_block(
        x, y_s5, hc, g1, sh2, sc2, g2,
        norm2_g[layer].reshape(1, D_MODEL), final_g.reshape(1, D_MODEL),
        conv_wb, conv_b[layer].reshape(1, CONV_WIDTH),
        conv_ln_g[layer].reshape(1, CONV_WIDTH), conv_ln_b[layer].reshape(1, CONV_WIDTH),
        s5_w_glu[layer], w_out[layer], mlp_w1[layer], mlp_w2[layer])
```

```python
import functools

import jax
import jax.numpy as jnp
from jax import lax
from jax.experimental import pallas as pl
from jax.experimental.pallas import tpu as pltpu

F32 = jnp.float32
BF16 = jnp.bfloat16

D_MODEL = 1024
S5_WIDTH = 512
S5_GROUP = 16
S5_GROUPS = 32
S5_STATE = 64
CONV_WIDTH = 512
CONV_K = 31
GRID_W = 64
IN_COLS = S5_WIDTH + 2 * CONV_WIDTH
D_FF = 4 * D_MODEL
EPS_RMS = 1e-6
EPS_LN = 1e-5

CHUNK = 16
CHUNK_COLS = CHUNK * S5_GROUP
MOD_ROWS = 16
BATCH = 8
TOK_ROWS = 64
STEP_ROWS = BATCH * TOK_ROWS
STEP_CHUNKS = TOK_ROWS // CHUNK
LANES = 128
SUBLANES = 8
SLOTS = LANES // S5_GROUP
LANE_BLOCKS = S5_WIDTH // LANES
PITCH = 72
IN_TOK_ROWS = 128
IN_PITCH = 136
CONV_ROWS = 16
VMEM_LIMIT = 56 * 1024 * 1024
OUT_VMEM_LIMIT = 60 * 1024 * 1024


def _dot(a, b):
    return jnp.dot(a, b, preferred_element_type=F32)


def _rms(x, g):
    return x * lax.rsqrt(jnp.mean(x * x, axis=-1, keepdims=True) + EPS_RMS) * g


def _mod_kernel(c_ref, w_ref, b_ref, o_ref):
    c = c_ref[...]
    s = c * jax.nn.sigmoid(c)
    w = w_ref[...]
    s_hi = s.astype(BF16)
    s_lo = (s - s_hi.astype(F32)).astype(BF16)
    w_hi = w.astype(BF16)
    w_lo = (w - w_hi.astype(F32)).astype(BF16)
    o_ref[...] = _dot(s_hi, w_hi) + _dot(s_lo, w_hi) + _dot(s_hi, w_lo) + b_ref[...]


def _mod_table(cc, ada_w, ada_b):
    n_out = ada_w.shape[1]
    tn = 1536
    return pl.pallas_call(
        _mod_kernel,
        grid=(n_out // tn,),
        in_specs=[
            pl.BlockSpec((MOD_ROWS, D_MODEL), lambda j: (0, 0)),
            pl.BlockSpec((D_MODEL, tn), lambda j: (0, j)),
            pl.BlockSpec((1, tn), lambda j: (0, j)),
        ],
        out_specs=pl.BlockSpec((MOD_ROWS, tn), lambda j: (0, j)),
        out_shape=jax.ShapeDtypeStruct((MOD_ROWS, n_out), F32),
        compiler_params=pltpu.CompilerParams(vmem_limit_bytes=VMEM_LIMIT),
        name="mod_table",
    )(cc, ada_w, ada_b.reshape(1, n_out))


def _slot_skew(vregs):
    slot = lax.broadcasted_iota(jnp.int32, (BATCH, LANES), 1) // S5_GROUP
    cur = list(vregs)
    for b in range(3):
        bit_set = (slot & (1 << b)) != 0
        cur = [jnp.where(bit_set, cur[(t + (1 << b)) % SLOTS], cur[t]) for t in range(SLOTS)]
    return cur


def _to_chunk_layout(z_scr, xs_scr, n_chunks, pitch):
    for cl in range(n_chunks):
        for vb in range(LANE_BLOCKS):
            for half in range(2):
                rot = []
                for ql in range(SLOTS):
                    piece = z_scr[vb, pl.ds(cl * CHUNK + SLOTS * half + ql, BATCH, stride=pitch), :]
                    rot.append(pltpu.roll(piece, ql * S5_GROUP, axis=1) if ql else piece)
                skew = _slot_skew(rot)
                for s in range(SLOTS):
                    col = (2 * (SLOTS * vb + s) + half) * LANES
                    xs_scr[cl * BATCH:(cl + 1) * BATCH, col:col + LANES] = skew[(-s) % SLOTS]


def _from_chunk_layout(ys, u_scr):
    for cl in range(STEP_CHUNKS):
        for vb in range(LANE_BLOCKS):
            for half in range(2):
                src = []
                for s in range(SLOTS):
                    col = (2 * (SLOTS * vb + s) + half) * LANES
                    src.append(ys[cl * BATCH:(cl + 1) * BATCH, col:col + LANES])
                skew = _slot_skew(src)
                for ql in range(SLOTS):
                    out = skew[(-ql) % SLOTS]
                    if ql:
                        out = pltpu.roll(out, LANES - ql * S5_GROUP, axis=1)
                    u_scr[vb, pl.ds(cl * CHUNK + SLOTS * half + ql, BATCH, stride=PITCH), :] = out


def _inproj_kernel(with_conv, x_ref, sh_ref, sc_ref, g_ref, w_ref, xs_ref, *rest):
    hc_ref = rest[0] if with_conv else None
    z_scr, xs_scr, w_bf = rest[-3:]

    @pl.when(pl.program_id(0) == 0)
    def _():
        w_bf[...] = w_ref[...].astype(BF16)

    tok = IN_TOK_ROWS
    never = pl.program_id(0) < 0
    hb = BATCH // 2
    zs = []
    for h in range(2):
        xh = x_ref[h * hb:(h + 1) * hb]
        if h:
            parts = []
            for b in range(hb):
                seed = jnp.where(never, zs[0][b * tok:b * tok + SUBLANES, 0:LANES], 0.0)
                xb = xh[b]
                parts.append(jnp.concatenate(
                    [xb[:, :LANES] + jnp.concatenate([seed] * (tok // SUBLANES), axis=0), xb[:, LANES:]], axis=1)[None])
            xh = jnp.concatenate(parts, axis=0)
        a = _rms(xh, g_ref[...]) * (1.0 + sc_ref[h * hb:(h + 1) * hb]) + sh_ref[h * hb:(h + 1) * hb]
        zs.append(_dot(a.reshape(hb * tok, D_MODEL).astype(BF16), w_bf[...]))
    z = jnp.concatenate(zs, axis=0)
    if with_conv:
        v = z[:, S5_WIDTH:S5_WIDTH + CONV_WIDTH]
        gate = z[:, S5_WIDTH + CONV_WIDTH:]
        hc_ref[...] = (v * jax.nn.sigmoid(gate)).astype(BF16).reshape(BATCH, tok, CONV_WIDTH)
    for vb in range(LANE_BLOCKS):
        for b in range(BATCH):
            z_scr[vb, b * IN_PITCH:b * IN_PITCH + tok, :] = z[b * tok:(b + 1) * tok, vb * LANES:(vb + 1) * LANES]
    _to_chunk_layout(z_scr, xs_scr, tok // CHUNK, IN_PITCH)
    for g in range(S5_GROUPS):
        xs_ref[g] = xs_scr[:, g * CHUNK_COLS:(g + 1) * CHUNK_COLS].astype(BF16)


def _in_proj(x, sh, sc, g, w_in, with_conv):
    bsz, n, _ = x.shape
    tok = IN_TOK_ROWS
    assert bsz == BATCH and n % tok == 0
    steps = n // tok
    chunk_rows = (tok // CHUNK) * BATCH
    n_cols = IN_COLS if with_conv else S5_WIDTH
    modv = pl.BlockSpec((BATCH, 1, D_MODEL), lambda i: (0, 0, 0))
    out_specs = [pl.BlockSpec((S5_GROUPS, chunk_rows, CHUNK_COLS), lambda i: (0, i, 0))]
    out_shape = [jax.ShapeDtypeStruct((S5_GROUPS, steps * chunk_rows, CHUNK_COLS), BF16)]
    if with_conv:
        out_specs.append(pl.BlockSpec((BATCH, tok, CONV_WIDTH), lambda i: (0, i, 0)))
        out_shape.append(jax.ShapeDtypeStruct((bsz, n, CONV_WIDTH), BF16))
    return pl.pallas_call(
        functools.partial(_inproj_kernel, with_conv),
        grid=(steps,),
        in_specs=[
            pl.BlockSpec((BATCH, tok, D_MODEL), lambda i: (0, i, 0)),
            modv, modv,
            pl.BlockSpec((1, D_MODEL), lambda i: (0, 0)),
            pl.BlockSpec((D_MODEL, n_cols), lambda i: (0, 0), pipeline_mode=pl.Buffered(1)),
        ],
        out_specs=out_specs,
        out_shape=out_shape,
        scratch_shapes=[
            pltpu.VMEM((LANE_BLOCKS, BATCH * IN_PITCH, LANES), F32),
            pltpu.VMEM((chunk_rows, S5_GROUPS * CHUNK_COLS), F32),
            pltpu.VMEM((D_MODEL, n_cols), BF16),
        ],
        compiler_params=pltpu.CompilerParams(
            dimension_semantics=("arbitrary",), vmem_limit_bytes=VMEM_LIMIT),
        name="in_proj" if with_conv else "in_proj_ctx",
    )(x, sh, sc, g, w_in)


SCAN_GROUPS = 4


def _mirror_bwd(a, n_chunks):
    rev = jnp.concatenate(
        [a[(n_chunks - 1 - c) * BATCH:(n_chunks - c) * BATCH, :] for c in range(n_chunks)], axis=0)
    lane = lax.broadcasted_iota(jnp.int32, a.shape, 1)
    return jnp.where((lane % (2 * S5_STATE)) < S5_STATE, a, rev)


def _s5_kernel(n_lat, n_ctx, xl_ref, xc_ref, t_ref, m_ref, r_ref, a_ref, y_ref, e_ref, p_ref, ec_ref):
    half = 2 * S5_STATE
    groups = range(SCAN_GROUPS)
    for g in groups:
        ec_ref[g] = _mirror_bwd(_dot(xc_ref[g], m_ref[g]), n_ctx)
        e_ref[g] = _mirror_bwd(_dot(xl_ref[g], m_ref[g]), n_lat)
    a_re = [jnp.broadcast_to(a_ref[g, 0:1, :], (BATCH, half)) for g in groups]
    a_im = [jnp.broadcast_to(a_ref[g, 1:2, :], (BATCH, half)) for g in groups]

    def advance(src_ref, g, rows, s_re, s_im):
        e = src_ref[g, rows, :]
        return (a_re[g] * s_re - a_im[g] * s_im + e[:, :half], a_re[g] * s_im + a_im[g] * s_re + e[:, half:])

    def ctx_body(i, s):
        rows = pl.ds(pl.multiple_of(i * BATCH, BATCH), BATCH)
        return tuple(advance(ec_ref, g, rows, *s[g]) for g in groups)

    zero = jnp.zeros((BATCH, half), F32)
    s = lax.fori_loop(0, n_ctx, ctx_body, tuple((zero, zero) for _ in groups))

    def lat_body(i, s):
        rows = pl.ds(pl.multiple_of(i * BATCH, BATCH), BATCH)
        out = []
        for g in groups:
            s_re, s_im = s[g]
            p_ref[g, rows, :half] = s_re
            p_ref[g, rows, half:] = s_im
            out.append(advance(e_ref, g, rows, s_re, s_im))
        return tuple(out)

    lax.fori_loop(0, n_lat, lat_body, s)
    for g in groups:
        p = _mirror_bwd(p_ref[g], n_lat)
        y = _dot(xl_ref[g], t_ref[g]) + _dot(p.astype(BF16), r_ref[g])
        y_ref[g] = y.astype(BF16)


def _s5_scan(x_lat, x_ctx, t_mat, m_mat, r_mat, a_pow, bsz):
    rows = x_lat.shape[1]
    rows_c = x_ctx.shape[1]
    gs = SCAN_GROUPS

    def per_group(*shape):
        return pl.BlockSpec((gs,) + shape, lambda i: (i,) + (0,) * len(shape))

    mat_spec = per_group(CHUNK_COLS, CHUNK_COLS)
    return pl.pallas_call(
        functools.partial(_s5_kernel, rows // bsz, rows_c // bsz),
        grid=(S5_GROUPS // gs,),
        in_specs=[
            per_group(rows, CHUNK_COLS), per_group(rows_c, CHUNK_COLS),
            mat_spec, mat_spec, mat_spec,
            per_group(2, 2 * S5_STATE),
        ],
        out_specs=per_group(rows, CHUNK_COLS),
        out_shape=jax.ShapeDtypeStruct((S5_GROUPS, rows, CHUNK_COLS), BF16),
        scratch_shapes=[
            pltpu.VMEM((gs, rows, CHUNK_COLS), F32),
            pltpu.VMEM((gs, rows, CHUNK_COLS), F32),
            pltpu.VMEM((gs, rows_c, CHUNK_COLS), F32),
        ],
        compiler_params=pltpu.CompilerParams(
            dimension_semantics=("parallel",), vmem_limit_bytes=VMEM_LIMIT),
        name="s5_scan",
    )(x_lat, x_ctx, t_mat, m_mat, r_mat, a_pow)


GROUPS_PER_STEP = SLOTS


def _cpow(kk, log_mag, ang):
    mag = jnp.exp(kk * log_mag)
    return mag * jnp.cos(kk * ang), mag * jnp.sin(kk * ang)


def _power_rows(pw, is_fwd, ks):
    zero = jnp.zeros((1, pw.shape[1]), F32)
    blocks = []
    for kf, kb in ks:
        row = jnp.where(is_fwd, zero if kf is None else pw[kf:kf + 1, :], zero if kb is None else pw[kb:kb + 1, :])
        blocks.append(jnp.broadcast_to(row, (S5_GROUP, pw.shape[1])))
    return jnp.concatenate(blocks, axis=0)


def _tile_rows(a, reps):
    return jnp.concatenate([a] * reps, axis=0)


def _dot3_nt(a, b):
    dn = (((1,), (1,)), ((), ()))
    a_hi = a.astype(BF16)
    a_lo = (a - a_hi.astype(F32)).astype(BF16)
    b_hi = b.astype(BF16)
    b_lo = (b - b_hi.astype(F32)).astype(BF16)

    def nt(u, v):
        return lax.dot_general(u, v, dn, preferred_element_type=F32)

    return nt(a_hi, b_hi) + nt(a_lo, b_hi) + nt(a_hi, b_lo)


def _chunk_row_order(a, s):
    blocks = []
    for half in range(2):
        for slot in range(SLOTS):
            q = SLOTS * half + (slot - s) % SLOTS
            blocks.append(a[q * S5_GROUP:(q + 1) * S5_GROUP, :])
    return jnp.concatenate(blocks, axis=0)


def _s5_prep_kernel(lre_ref, lim_ref, ldt_ref, btr_ref, bti_ref, ctr_ref, cti_ref, dw_ref,
                    t_ref, m_ref, r_ref, a_ref):
    half_l = 2 * S5_STATE
    n_lag = 2 * CHUNK - 1
    is_fwd = lax.broadcasted_iota(jnp.int32, (1, half_l), 1) < S5_STATE
    n_pow = 24
    k_row = lax.broadcasted_iota(jnp.int32, (n_pow, half_l), 0).astype(F32)
    top = CHUNK - 1
    slot_of_lane = lax.broadcasted_iota(jnp.int32, (CHUNK_COLS, LANES), 1) // S5_GROUP
    eye = (lax.broadcasted_iota(jnp.int32, (CHUNK_COLS, CHUNK_COLS), 0)
           == lax.broadcasted_iota(jnp.int32, (CHUNK_COLS, CHUNK_COLS), 1))

    for s in range(GROUPS_PER_STEP):
        lre, lim = lre_ref[s], lim_ref[s]
        dt = jnp.exp(ldt_ref[s])
        log_mag, ang = lre * dt, lim * dt
        pw_re, pw_im = _cpow(k_row, log_mag, ang)
        a1_re, a1_im = pw_re[1:2, :], pw_im[1:2, :]
        den = lre * lre + lim * lim
        num_re = a1_re - 1.0
        f_re = (num_re * lre + a1_im * lim) / den
        f_im = (a1_im * lre - num_re * lim) / den
        bt_re, bt_im = btr_ref[s], bti_ref[s]
        bb_re = f_re * bt_re - f_im * bt_im
        bb_im = f_re * bt_im + f_im * bt_re
        ct_re, ct_im = ctr_ref[s], cti_ref[s]

        lag_ks = [(top - j if j <= top else None, j - top if j >= top else None) for j in range(n_lag)]
        aj_re, aj_im = _power_rows(pw_re, is_fwd, lag_ks), _power_rows(pw_im, is_fwd, lag_ks)
        bl_re, bl_im = _tile_rows(bb_re, n_lag), _tile_rows(bb_im, n_lag)
        l_re = aj_re * bl_re - aj_im * bl_im
        l_im = aj_re * bl_im + aj_im * bl_re
        cw_re, cw_im = _tile_rows(ct_re, CHUNK), _tile_rows(ct_im, CHUNK)
        kwide = _dot3_nt(jnp.concatenate([l_re, -l_im], axis=1),
                         jnp.concatenate([cw_re, cw_im], axis=1))
        halves = []
        for half in range(2):
            col = kwide[:, half * LANES:(half + 1) * LANES]
            out = None
            for slot in range(SLOTS):
                q = SLOTS * half + (slot - s) % SLOTS
                start = (CHUNK - 1 - q) * S5_GROUP
                win = col[start:start + CHUNK_COLS, :]
                out = win if out is None else jnp.where(slot_of_lane == slot, win, out)
            halves.append(out)
        t_nat = jnp.concatenate(halves, axis=1)
        t_mat = _chunk_row_order(t_nat, s) + jnp.where(eye, dw_ref[s], 0.0)
        t_ref[s] = t_mat.astype(BF16)

        m_ks = [(top - q, q) for q in range(CHUNK)]
        am_re, am_im = _power_rows(pw_re, is_fwd, m_ks), _power_rows(pw_im, is_fwd, m_ks)
        bm_re, bm_im = _tile_rows(bb_re, CHUNK), _tile_rows(bb_im, CHUNK)
        m_nat = jnp.concatenate([am_re * bm_re - am_im * bm_im, am_re * bm_im + am_im * bm_re], axis=1)
        m_ref[s] = _chunk_row_order(m_nat, s).astype(BF16)

        r_ks = [(q + 1, CHUNK - q) for q in range(CHUNK)]
        ar_re, ar_im = _power_rows(pw_re, is_fwd, r_ks), _power_rows(pw_im, is_fwd, r_ks)
        rt_nat = jnp.concatenate([ar_re * cw_re - ar_im * cw_im, -(ar_re * cw_im + ar_im * cw_re)], axis=1)
        r_ref[s] = _chunk_row_order(rt_nat, s).T.astype(BF16)

        a_ref[s] = jnp.concatenate([pw_re[CHUNK:CHUNK + 1, :], pw_im[CHUNK:CHUNK + 1, :]], axis=0)


def _s5_operators(lam_re, lam_im, log_dt, b_re, b_im, c_re, c_im, d_skip):
    half_l = 2 * S5_STATE

    def lanes(v):
        return jnp.transpose(v, (1, 0, 2)).reshape(S5_GROUPS, 1, half_l)

    ldt = lanes(jnp.broadcast_to(log_dt[..., None], lam_re.shape))
    bt_re = jnp.transpose(b_re, (1, 3, 0, 2)).reshape(S5_GROUPS, S5_GROUP, half_l)
    bt_im = jnp.transpose(b_im, (1, 3, 0, 2)).reshape(S5_GROUPS, S5_GROUP, half_l)
    ct_re = jnp.transpose(c_re, (1, 2, 0, 3)).reshape(S5_GROUPS, S5_GROUP, half_l)
    ct_im = jnp.transpose(c_im, (1, 2, 0, 3)).reshape(S5_GROUPS, S5_GROUP, half_l)
    dw = jnp.tile(d_skip, (1, CHUNK)).reshape(S5_GROUPS, 1, CHUNK_COLS)

    gs = GROUPS_PER_STEP
    vec = pl.BlockSpec((gs, 1, half_l), lambda i: (i, 0, 0))
    par = pl.BlockSpec((gs, S5_GROUP, half_l), lambda i: (i, 0, 0))
    mat = pl.BlockSpec((gs, CHUNK_COLS, CHUNK_COLS), lambda i: (i, 0, 0))
    mat_shape = jax.ShapeDtypeStruct((S5_GROUPS, CHUNK_COLS, CHUNK_COLS), BF16)
    return pl.pallas_call(
        _s5_prep_kernel,
        grid=(S5_GROUPS // gs,),
        in_specs=[vec, vec, vec, par, par, par, par,
                  pl.BlockSpec((gs, 1, CHUNK_COLS), lambda i: (i, 0, 0))],
        out_specs=[mat, mat, mat, pl.BlockSpec((gs, 2, half_l), lambda i: (i, 0, 0))],
        out_shape=[mat_shape, mat_shape, mat_shape,
                   jax.ShapeDtypeStruct((S5_GROUPS, 2, half_l), F32)],
        compiler_params=pltpu.CompilerParams(
            dimension_semantics=("parallel",), vmem_limit_bytes=VMEM_LIMIT),
        name="s5_operators",
    )(lanes(lam_re), lanes(lam_im), ldt, bt_re, bt_im, ct_re, ct_im, dw)


CONV_HALF = CONV_K // 2
CONV_LAG = CONV_HALF + 1
RING_SLOTS = 32
CONV_BLOCKS = STEP_ROWS // CONV_ROWS
BF16_ROWS = 16
TAP_GROUP = 8
FF_TILE = 512


def _aligned(x, m):
    return x if isinstance(x, int) else pl.multiple_of(x, m)


def _conv_block(ring, slots, blk, wb_ref, cb_ref, lg_ref, lb_ref, seed=None):
    b = blk // (TOK_ROWS // CONV_ROWS)
    t0 = _aligned((blk % (TOK_ROWS // CONV_ROWS)) * CONV_ROWS, CONV_ROWS)
    packed = (CONV_ROWS // BF16_ROWS, BF16_ROWS, CONV_WIDTH)
    acc = jnp.zeros(packed, F32)
    if seed is not None:
        rows = jnp.concatenate([seed] * (BF16_ROWS // SUBLANES), axis=0)
        acc = acc + jnp.concatenate([rows] * (CONV_WIDTH // LANES), axis=1)[None]
    for k0 in range(0, CONV_K, TAP_GROUP):
        group = None
        for k in range(k0, min(k0 + TAP_GROUP, CONV_K)):
            prod = wb_ref[k] * ring[slots[k], b, pl.ds(t0, CONV_ROWS), :].reshape(packed)
            group = prod if group is None else group + prod
        acc = acc + group.astype(F32)
    acc = acc.reshape(CONV_ROWS, CONV_WIDTH) + cb_ref[...]
    mu = jnp.mean(acc, axis=-1, keepdims=True)
    dev = acc - mu
    var = jnp.mean(dev * dev, axis=-1, keepdims=True)
    y = dev * lax.rsqrt(var + EPS_LN) * lg_ref[...] + lb_ref[...]
    return (y * jax.nn.sigmoid(y)).astype(BF16)


def _out_kernel(n_rows, x_ref, ys_ref, hc_ref, g1_ref, sh2_ref, sc2_ref, g2_ref, n2_ref, fg_ref,
                wb_ref, cb_ref, lg_ref, lb_ref, wglu_in, wout_in, w1_in, w2_in, o_ref,
                u_scr, ring, yc_scr, wglu_ref, wout_ref, w1_ref, w2_ref):
    s = pl.program_id(0)
    zero_row = jnp.zeros((BATCH, TOK_ROWS, CONV_WIDTH), BF16)

    @pl.when(s < CONV_LAG)
    def _():
        for w_in_ref, w_bf in ((wglu_in, wglu_ref), (wout_in, wout_ref), (w1_in, w1_ref), (w2_in, w2_ref)):
            rows = w_in_ref.shape[0]
            w_bf[pl.ds(pl.multiple_of(s * rows, rows), rows), :] = w_in_ref[...].astype(BF16)

    @pl.when(s == 0)
    def _():
        for i in range(RING_SLOTS - CONV_HALF, RING_SLOTS):
            ring[i] = zero_row

    @pl.when(s < n_rows)
    def _():
        ring[s % RING_SLOTS] = hc_ref[...]

    @pl.when(s >= n_rows)
    def _():
        ring[s % RING_SLOTS] = zero_row

    slots = [(s + (RING_SLOTS - 2 * CONV_HALF) + k) % RING_SLOTS for k in range(CONV_K)]
    conv_args = (wb_ref, cb_ref, lg_ref, lb_ref)

    @pl.when(s == CONV_HALF)
    def _():
        def body(blk, carry):
            rows = pl.ds(_aligned(blk * CONV_ROWS, CONV_ROWS), CONV_ROWS)
            yc_scr[rows, :] = _conv_block(ring, slots, blk, *conv_args)
            return carry
        lax.fori_loop(0, CONV_BLOCKS, body, 0, unroll=4)

    @pl.when(s >= CONV_LAG)
    def _():
        yc = yc_scr[...]
        ys = jnp.concatenate([ys_ref[g].astype(F32) for g in range(S5_GROUPS)], axis=1)
        _from_chunk_layout(ys, u_scr)
        y_s5 = jnp.concatenate(
            [jnp.concatenate([u_scr[vb, b * PITCH:b * PITCH + TOK_ROWS, :] for b in range(BATCH)], axis=0)
             for vb in range(LANE_BLOCKS)], axis=1)
        gl = jax.nn.gelu(y_s5)
        s5o = gl * jax.nn.sigmoid(_dot(gl.astype(BF16), wglu_ref[...]))
        mix = _dot(s5o.astype(BF16), wout_ref[:S5_WIDTH, :]) + _dot(yc, wout_ref[S5_WIDTH:, :])
        h1 = x_ref[...] + g1_ref[...] * mix.reshape(BATCH, TOK_ROWS, D_MODEL)
        a2 = _rms(h1, n2_ref[...]) * (1.0 + sc2_ref[...]) + sh2_ref[...]
        a2 = a2.reshape(STEP_ROWS, D_MODEL).astype(BF16)
        ff_tile = FF_TILE
        n_ff = D_FF // ff_tile
        per_dot = CONV_BLOCKS // (2 * n_ff)
        never = s < 0

        def seeded_conv(first_blk, mat):
            for i in range(per_dot):
                blk = first_blk + i
                r = i * (STEP_ROWS // per_dot)
                seed = jnp.where(never, mat[r:r + SUBLANES, 0:LANES], 0.0)
                yc_scr[blk * CONV_ROWS:(blk + 1) * CONV_ROWS, :] = _conv_block(
                    ring, slots, blk, *conv_args, seed=seed)

        acc = jnp.zeros((STEP_ROWS, D_MODEL), F32)
        for j in range(n_ff):
            up = _dot(a2, w1_ref[:, j * ff_tile:(j + 1) * ff_tile])
            seeded_conv(2 * j * per_dot, up)
            hid = jnp.maximum(up, 0.0)
            down = _dot((hid * hid).astype(BF16), w2_ref[j * ff_tile:(j + 1) * ff_tile, :])
            seeded_conv((2 * j + 1) * per_dot, down)
            acc = acc + down
        h2 = h1 + g2_ref[...] * acc.reshape(BATCH, TOK_ROWS, D_MODEL)
        o_ref[...] = _rms(h2, fg_ref[...])


def _out_block(x, ys, hc, g1, sh2, sc2, g2, n2, fg, wb, cb, lg, lb, wglu, wout, w1, w2):
    bsz, n, _ = x.shape
    assert bsz == BATCH and n % TOK_ROWS == 0
    n_rows = n // TOK_ROWS
    chunk_rows = STEP_CHUNKS * BATCH
    modv = pl.BlockSpec((BATCH, 1, D_MODEL), lambda s: (0, 0, 0))
    vec = pl.BlockSpec((1, D_MODEL), lambda s: (0, 0))
    cvec = pl.BlockSpec((1, CONV_WIDTH), lambda s: (0, 0))

    def const(shape):
        return pl.BlockSpec(shape, lambda s: (0,) * len(shape), pipeline_mode=pl.Buffered(1))

    def lagged(s):
        return jnp.maximum(s - CONV_LAG, 0)

    def staged(w):
        return pl.BlockSpec((w.shape[0] // CONV_LAG, w.shape[1]), lambda s: (jnp.minimum(s, CONV_LAG - 1), 0))

    return pl.pallas_call(
        functools.partial(_out_kernel, n_rows),
        grid=(n_rows + CONV_LAG,),
        in_specs=[
            pl.BlockSpec((BATCH, TOK_ROWS, D_MODEL), lambda s: (0, lagged(s), 0)),
            pl.BlockSpec((S5_GROUPS, chunk_rows, CHUNK_COLS), lambda s: (0, lagged(s), 0)),
            pl.BlockSpec((BATCH, TOK_ROWS, CONV_WIDTH), lambda s: (0, jnp.minimum(s, n_rows - 1), 0)),
            modv, modv, modv, modv, vec, vec,
            const((CONV_K, BF16_ROWS, CONV_WIDTH)), cvec, cvec, cvec,
            staged(wglu), staged(wout), staged(w1), staged(w2),
        ],
        out_specs=pl.BlockSpec((BATCH, TOK_ROWS, D_MODEL), lambda s: (0, lagged(s), 0)),
        out_shape=jax.ShapeDtypeStruct((bsz, n, D_MODEL), F32),
        scratch_shapes=[
            pltpu.VMEM((LANE_BLOCKS, BATCH * PITCH, LANES), F32),
            pltpu.VMEM((RING_SLOTS, BATCH, TOK_ROWS, CONV_WIDTH), BF16),
            pltpu.VMEM((STEP_ROWS, CONV_WIDTH), BF16),
            pltpu.VMEM(wglu.shape, BF16), pltpu.VMEM(wout.shape, BF16),
            pltpu.VMEM(w1.shape, BF16), pltpu.VMEM(w2.shape, BF16),
        ],
        compiler_params=pltpu.CompilerParams(
            dimension_semantics=("arbitrary",), vmem_limit_bytes=OUT_VMEM_LIMIT),
        name="out_block",
    )(x, ys, hc, g1, sh2, sc2, g2, n2, fg, wb, cb, lg, lb, wglu, wout, w1, w2)


def kernel(x, c, ctx, c_ctx, ada_w, ada_b, norm1_g, w_in, s5_lam_re, s5_lam_im, s5_log_dt, s5_b_re, s5_b_im, s5_c_re, s5_c_im, s5_d, s5_w_glu, conv_w, conv_b, conv_ln_g, conv_ln_b, w_out, norm2_g, mlp_w1, mlp_w2, final_g):
    bsz = x.shape[0]
    assert bsz == BATCH and bsz + 1 <= MOD_ROWS
    layer = 0

    cc = jnp.concatenate([c, c_ctx[None, :], jnp.zeros((MOD_ROWS - bsz - 1, D_MODEL), F32)], axis=0)
    mod = _mod_table(cc, ada_w[layer], ada_b[layer])
    sh1, sc1, g1, sh2, sc2, g2 = [m.reshape(bsz, 1, D_MODEL) for m in jnp.split(mod[:bsz], 6, axis=-1)]
    csh1 = jnp.broadcast_to(mod[bsz, :D_MODEL].reshape(1, 1, D_MODEL), (bsz, 1, D_MODEL))
    csc1 = jnp.broadcast_to(mod[bsz, D_MODEL:2 * D_MODEL].reshape(1, 1, D_MODEL), (bsz, 1, D_MODEL))

    n1 = norm1_g[layer].reshape(1, D_MODEL)
    x_lat, hc = _in_proj(x, sh1, sc1, n1, w_in[layer], True)
    x_ctx, = _in_proj(ctx, csh1, csc1, n1, w_in[layer], False)

    t_mat, m_mat, r_mat, a_pow = _s5_operators(
        s5_lam_re[layer], s5_lam_im[layer], s5_log_dt[layer], s5_b_re[layer], s5_b_im[layer],
        s5_c_re[layer], s5_c_im[layer], s5_d[layer])
    y_s5 = _s5_scan(x_lat, x_ctx, t_mat, m_mat, r_mat, a_pow, bsz)

    conv_wb = jnp.broadcast_to(conv_w[layer].astype(BF16)[:, None, :], (CONV_K, BF16_ROWS, CONV_WIDTH))
    return _out_block(
        x, y_s5, hc, g1, sh2, sc2, g2,
        norm2_g[layer].reshape(1, D_MODEL), final_g.reshape(1, D_MODEL),
        conv_wb, conv_b[layer].reshape(1, CONV_WIDTH),
        conv_ln_g[layer].reshape(1, CONV_WIDTH), conv_ln_b[layer].reshape(1, CONV_WIDTH),
        s5_w_glu[layer], w_out[layer], mlp_w1[layer], mlp_w2[layer])
```

```python
import functools

import jax
import jax.numpy as jnp
from jax import lax
from jax.experimental import pallas as pl
from jax.experimental.pallas import tpu as pltpu

F32 = jnp.float32
BF16 = jnp.bfloat16

D_MODEL = 1024
S5_WIDTH = 512
S5_GROUP = 16
S5_GROUPS = 32
S5_STATE = 64
CONV_WIDTH = 512
CONV_K = 31
GRID_W = 64
IN_COLS = S5_WIDTH + 2 * CONV_WIDTH
D_FF = 4 * D_MODEL
EPS_RMS = 1e-6
EPS_LN = 1e-5

CHUNK = 16
CHUNK_COLS = CHUNK * S5_GROUP
MOD_ROWS = 16
BATCH = 8
TOK_ROWS = 64
STEP_ROWS = BATCH * TOK_ROWS
STEP_CHUNKS = TOK_ROWS // CHUNK
LANES = 128
SUBLANES = 8
SLOTS = LANES // S5_GROUP
LANE_BLOCKS = S5_WIDTH // LANES
PITCH = 72
IN_TOK_ROWS = 128
IN_PITCH = 136
CONV_ROWS = 16
VMEM_LIMIT = 56 * 1024 * 1024
OUT_VMEM_LIMIT = 60 * 1024 * 1024


def _dot(a, b):
    return jnp.dot(a, b, preferred_element_type=F32)


def _rms(x, g):
    return x * lax.rsqrt(jnp.mean(x * x, axis=-1, keepdims=True) + EPS_RMS) * g


def _mod_kernel(c_ref, w_ref, b_ref, o_ref):
    c = c_ref[...]
    s = c * jax.nn.sigmoid(c)
    w = w_ref[...]
    s_hi = s.astype(BF16)
    s_lo = (s - s_hi.astype(F32)).astype(BF16)
    w_hi = w.astype(BF16)
    w_lo = (w - w_hi.astype(F32)).astype(BF16)
    o_ref[...] = _dot(s_hi, w_hi) + _dot(s_lo, w_hi) + _dot(s_hi, w_lo) + b_ref[...]


def _slot_skew(vregs):
    slot = lax.broadcasted_iota(jnp.int32, (BATCH, LANES), 1) // S5_GROUP
    cur = list(vregs)
    for b in range(3):
        bit_set = (slot & (1 << b)) != 0
        cur = [jnp.where(bit_set, cur[(t + (1 << b)) % SLOTS], cur[t]) for t in range(SLOTS)]
    return cur


def _to_chunk_layout(z_scr, xs_scr, n_chunks, pitch):
    for cl in range(n_chunks):
        for vb in range(LANE_BLOCKS):
            for half in range(2):
                rot = []
                for ql in range(SLOTS):
                    piece = z_scr[vb, pl.ds(cl * CHUNK + SLOTS * half + ql, BATCH, stride=pitch), :]
                    rot.append(pltpu.roll(piece, ql * S5_GROUP, axis=1) if ql else piece)
                skew = _slot_skew(rot)
                for s in range(SLOTS):
                    col = (2 * (SLOTS * vb + s) + half) * LANES
                    xs_scr[cl * BATCH:(cl + 1) * BATCH, col:col + LANES] = skew[(-s) % SLOTS]


def _from_chunk_layout(ys, u_scr):
    for cl in range(STEP_CHUNKS):
        for vb in range(LANE_BLOCKS):
            for half in range(2):
                src = []
                for s in range(SLOTS):
                    col = (2 * (SLOTS * vb + s) + half) * LANES
                    src.append(ys[cl * BATCH:(cl + 1) * BATCH, col:col + LANES])
                skew = _slot_skew(src)
                for ql in range(SLOTS):
                    out = skew[(-ql) % SLOTS]
                    if ql:
                        out = pltpu.roll(out, LANES - ql * S5_GROUP, axis=1)
                    u_scr[vb, pl.ds(cl * CHUNK + SLOTS * half + ql, BATCH, stride=PITCH), :] = out


def _inproj_kernel(with_conv, x_ref, sh_ref, sc_ref, g_ref, w_ref, xs_ref, *rest):
    hc_ref = rest[0] if with_conv else None
    z_scr, xs_scr, w_bf = rest[-3:]

    @pl.when(pl.program_id(0) == 0)
    def _():
        w_bf[...] = w_ref[...].astype(BF16)

    tok = IN_TOK_ROWS
    never = pl.program_id(0) < 0
    hb = BATCH // 2
    zs = []
    for h in range(2):
        xh = x_ref[h * hb:(h + 1) * hb]
        if h:
            parts = []
            for b in range(hb):
                seed = jnp.where(never, zs[0][b * tok:b * tok + SUBLANES, 0:LANES], 0.0)
                xb = xh[b]
                parts.append(jnp.concatenate(
                    [xb[:, :LANES] + jnp.concatenate([seed] * (tok // SUBLANES), axis=0), xb[:, LANES:]], axis=1)[None])
            xh = jnp.concatenate(parts, axis=0)
        a = _rms(xh, g_ref[...]) * (1.0 + sc_ref[h * hb:(h + 1) * hb]) + sh_ref[h * hb:(h + 1) * hb]
        zs.append(_dot(a.reshape(hb * tok, D_MODEL).astype(BF16), w_bf[...]))
    z = jnp.concatenate(zs, axis=0)
    if with_conv:
        v = z[:, S5_WIDTH:S5_WIDTH + CONV_WIDTH]
        gate = z[:, S5_WIDTH + CONV_WIDTH:]
        hc_ref[...] = (v * jax.nn.sigmoid(gate)).astype(BF16).reshape(BATCH, tok, CONV_WIDTH)
    for vb in range(LANE_BLOCKS):
        for b in range(BATCH):
            z_scr[vb, b * IN_PITCH:b * IN_PITCH + tok, :] = z[b * tok:(b + 1) * tok, vb * LANES:(vb + 1) * LANES]
    _to_chunk_layout(z_scr, xs_scr, tok // CHUNK, IN_PITCH)
    for g in range(S5_GROUPS):
        xs_ref[g] = xs_scr[:, g * CHUNK_COLS:(g + 1) * CHUNK_COLS].astype(BF16)


def _mod_spec(chunk):
    return pl.BlockSpec((BATCH, None, 1, D_MODEL), lambda i: (0, chunk, 0, 0))


def _in_proj(x, mod4, shift_chunk, scale_chunk, g, w_in, with_conv):
    bsz, n, _ = x.shape
    tok = IN_TOK_ROWS
    assert bsz == BATCH and n % tok == 0
    steps = n // tok
    chunk_rows = (tok // CHUNK) * BATCH
    n_cols = IN_COLS if with_conv else S5_WIDTH
    out_specs = [pl.BlockSpec((S5_GROUPS, chunk_rows, CHUNK_COLS), lambda i: (0, i, 0))]
    out_shape = [jax.ShapeDtypeStruct((S5_GROUPS, steps * chunk_rows, CHUNK_COLS), BF16)]
    if with_conv:
        out_specs.append(pl.BlockSpec((BATCH, tok, CONV_WIDTH), lambda i: (0, i, 0)))
        out_shape.append(jax.ShapeDtypeStruct((bsz, n, CONV_WIDTH), BF16))
    return pl.pallas_call(
        functools.partial(_inproj_kernel, with_conv),
        grid=(steps,),
        in_specs=[
            pl.BlockSpec((BATCH, tok, D_MODEL), lambda i: (0, i, 0)),
            _mod_spec(shift_chunk), _mod_spec(scale_chunk),
            pl.BlockSpec((1, D_MODEL), lambda i: (0, 0)),
            pl.BlockSpec((D_MODEL, n_cols), lambda i: (0, 0), pipeline_mode=pl.Buffered(1)),
        ],
        out_specs=out_specs,
        out_shape=out_shape,
        scratch_shapes=[
            pltpu.VMEM((LANE_BLOCKS, BATCH * IN_PITCH, LANES), F32),
            pltpu.VMEM((chunk_rows, S5_GROUPS * CHUNK_COLS), F32),
            pltpu.VMEM((D_MODEL, n_cols), BF16),
        ],
        compiler_params=pltpu.CompilerParams(
            dimension_semantics=("arbitrary",), vmem_limit_bytes=VMEM_LIMIT),
        name="in_proj" if with_conv else "in_proj_ctx",
    )(x, mod4, mod4, g, w_in)


SCAN_GROUPS = 4


def _mirror_bwd(a, n_chunks):
    rev = jnp.concatenate(
        [a[(n_chunks - 1 - c) * BATCH:(n_chunks - c) * BATCH, :] for c in range(n_chunks)], axis=0)
    lane = lax.broadcasted_iota(jnp.int32, a.shape, 1)
    return jnp.where((lane % (2 * S5_STATE)) < S5_STATE, a, rev)


def _s5_kernel(n_lat, n_ctx, xl_ref, xc_ref, t_ref, m_ref, r_ref, a_ref, y_ref, e_ref, p_ref, ec_ref):
    half = 2 * S5_STATE
    groups = range(SCAN_GROUPS)
    for g in groups:
        ec_ref[g] = _mirror_bwd(_dot(xc_ref[g], m_ref[g]), n_ctx)
        e_ref[g] = _mirror_bwd(_dot(xl_ref[g], m_ref[g]), n_lat)
    a_re = [jnp.broadcast_to(a_ref[g, 0:1, :], (BATCH, half)) for g in groups]
    a_im = [jnp.broadcast_to(a_ref[g, 1:2, :], (BATCH, half)) for g in groups]

    def advance(src_ref, g, rows, s_re, s_im):
        e = src_ref[g, rows, :]
        return (a_re[g] * s_re - a_im[g] * s_im + e[:, :half], a_re[g] * s_im + a_im[g] * s_re + e[:, half:])

    def ctx_body(i, s):
        rows = pl.ds(pl.multiple_of(i * BATCH, BATCH), BATCH)
        return tuple(advance(ec_ref, g, rows, *s[g]) for g in groups)

    zero = jnp.zeros((BATCH, half), F32)
    s = lax.fori_loop(0, n_ctx, ctx_body, tuple((zero, zero) for _ in groups))

    def lat_body(i, s):
        rows = pl.ds(pl.multiple_of(i * BATCH, BATCH), BATCH)
        out = []
        for g in groups:
            s_re, s_im = s[g]
            p_ref[g, rows, :half] = s_re
            p_ref[g, rows, half:] = s_im
            out.append(advance(e_ref, g, rows, s_re, s_im))
        return tuple(out)

    lax.fori_loop(0, n_lat, lat_body, s)
    for g in groups:
        p = _mirror_bwd(p_ref[g], n_lat)
        y = _dot(xl_ref[g], t_ref[g]) + _dot(p.astype(BF16), r_ref[g])
        y_ref[g] = y.astype(BF16)


def _s5_scan(x_lat, x_ctx, t_mat, m_mat, r_mat, a_pow, bsz):
    rows = x_lat.shape[1]
    rows_c = x_ctx.shape[1]
    gs = SCAN_GROUPS

    def per_group(*shape):
        return pl.BlockSpec((gs,) + shape, lambda i: (i,) + (0,) * len(shape))

    mat_spec = per_group(CHUNK_COLS, CHUNK_COLS)
    return pl.pallas_call(
        functools.partial(_s5_kernel, rows // bsz, rows_c // bsz),
        grid=(S5_GROUPS // gs,),
        in_specs=[
            per_group(rows, CHUNK_COLS), per_group(rows_c, CHUNK_COLS),
            mat_spec, mat_spec, mat_spec,
            per_group(2, 2 * S5_STATE),
        ],
        out_specs=per_group(rows, CHUNK_COLS),
        out_shape=jax.ShapeDtypeStruct((S5_GROUPS, rows, CHUNK_COLS), BF16),
        scratch_shapes=[
            pltpu.VMEM((gs, rows, CHUNK_COLS), F32),
            pltpu.VMEM((gs, rows, CHUNK_COLS), F32),
            pltpu.VMEM((gs, rows_c, CHUNK_COLS), F32),
        ],
        compiler_params=pltpu.CompilerParams(
            dimension_semantics=("parallel",), vmem_limit_bytes=VMEM_LIMIT),
        name="s5_scan",
    )(x_lat, x_ctx, t_mat, m_mat, r_mat, a_pow)


GROUPS_PER_STEP = SLOTS


def _cpow(kk, log_mag, ang):
    mag = jnp.exp(kk * log_mag)
    return mag * jnp.cos(kk * ang), mag * jnp.sin(kk * ang)


def _power_rows(pw, is_fwd, ks):
    zero = jnp.zeros((1, pw.shape[1]), F32)
    blocks = []
    for kf, kb in ks:
        row = jnp.where(is_fwd, zero if kf is None else pw[kf:kf + 1, :], zero if kb is None else pw[kb:kb + 1, :])
        blocks.append(jnp.broadcast_to(row, (S5_GROUP, pw.shape[1])))
    return jnp.concatenate(blocks, axis=0)


def _tile_rows(a, reps):
    return jnp.concatenate([a] * reps, axis=0)


def _dot3_nt(a, b):
    dn = (((1,), (1,)), ((), ()))
    a_hi = a.astype(BF16)
    a_lo = (a - a_hi.astype(F32)).astype(BF16)
    b_hi = b.astype(BF16)
    b_lo = (b - b_hi.astype(F32)).astype(BF16)

    def nt(u, v):
        return lax.dot_general(u, v, dn, preferred_element_type=F32)

    return nt(a_hi, b_hi) + nt(a_lo, b_hi) + nt(a_hi, b_lo)


def _chunk_row_order(a, s):
    blocks = []
    for half in range(2):
        for slot in range(SLOTS):
            q = SLOTS * half + (slot - s) % SLOTS
            blocks.append(a[q * S5_GROUP:(q + 1) * S5_GROUP, :])
    return jnp.concatenate(blocks, axis=0)


def _s5_prep_kernel(lre_ref, lim_ref, ldt_ref, btr_ref, bti_ref, ctr_ref, cti_ref, dw_ref,
                    t_ref, m_ref, r_ref, a_ref):
    half_l = 2 * S5_STATE
    n_lag = 2 * CHUNK - 1
    is_fwd = lax.broadcasted_iota(jnp.int32, (1, half_l), 1) < S5_STATE
    n_pow = 24
    k_row = lax.broadcasted_iota(jnp.int32, (n_pow, half_l), 0).astype(F32)
    top = CHUNK - 1
    slot_of_lane = lax.broadcasted_iota(jnp.int32, (CHUNK_COLS, LANES), 1) // S5_GROUP
    eye = (lax.broadcasted_iota(jnp.int32, (CHUNK_COLS, CHUNK_COLS), 0)
           == lax.broadcasted_iota(jnp.int32, (CHUNK_COLS, CHUNK_COLS), 1))

    for s in range(GROUPS_PER_STEP):
        lre, lim = lre_ref[s], lim_ref[s]
        dt = jnp.exp(ldt_ref[s])
        log_mag, ang = lre * dt, lim * dt
        pw_re, pw_im = _cpow(k_row, log_mag, ang)
        a1_re, a1_im = pw_re[1:2, :], pw_im[1:2, :]
        den = lre * lre + lim * lim
        num_re = a1_re - 1.0
        f_re = (num_re * lre + a1_im * lim) / den
        f_im = (a1_im * lre - num_re * lim) / den
        bt_re, bt_im = btr_ref[s], bti_ref[s]
        bb_re = f_re * bt_re - f_im * bt_im
        bb_im = f_re * bt_im + f_im * bt_re
        ct_re, ct_im = ctr_ref[s], cti_ref[s]

        lag_ks = [(top - j if j <= top else None, j - top if j >= top else None) for j in range(n_lag)]
        aj_re, aj_im = _power_rows(pw_re, is_fwd, lag_ks), _power_rows(pw_im, is_fwd, lag_ks)
        bl_re, bl_im = _tile_rows(bb_re, n_lag), _tile_rows(bb_im, n_lag)
        l_re = aj_re * bl_re - aj_im * bl_im
        l_im = aj_re * bl_im + aj_im * bl_re
        cw_re, cw_im = _tile_rows(ct_re, CHUNK), _tile_rows(ct_im, CHUNK)
        kwide = _dot3_nt(jnp.concatenate([l_re, -l_im], axis=1),
                         jnp.concatenate([cw_re, cw_im], axis=1))
        halves = []
        for half in range(2):
            col = kwide[:, half * LANES:(half + 1) * LANES]
            out = None
            for slot in range(SLOTS):
                q = SLOTS * half + (slot - s) % SLOTS
                start = (CHUNK - 1 - q) * S5_GROUP
                win = col[start:start + CHUNK_COLS, :]
                out = win if out is None else jnp.where(slot_of_lane == slot, win, out)
            halves.append(out)
        t_nat = jnp.concatenate(halves, axis=1)
        t_mat = _chunk_row_order(t_nat, s) + jnp.where(eye, dw_ref[s], 0.0)
        t_ref[s] = t_mat.astype(BF16)

        m_ks = [(top - q, q) for q in range(CHUNK)]
        am_re, am_im = _power_rows(pw_re, is_fwd, m_ks), _power_rows(pw_im, is_fwd, m_ks)
        bm_re, bm_im = _tile_rows(bb_re, CHUNK), _tile_rows(bb_im, CHUNK)
        m_nat = jnp.concatenate([am_re * bm_re - am_im * bm_im, am_re * bm_im + am_im * bm_re], axis=1)
        m_ref[s] = _chunk_row_order(m_nat, s).astype(BF16)

        r_ks = [(q + 1, CHUNK - q) for q in range(CHUNK)]
        ar_re, ar_im = _power_rows(pw_re, is_fwd, r_ks), _power_rows(pw_im, is_fwd, r_ks)
        rt_nat = jnp.concatenate([ar_re * cw_re - ar_im * cw_im, -(ar_re * cw_im + ar_im * cw_re)], axis=1)
        r_ref[s] = _chunk_row_order(rt_nat, s).T.astype(BF16)

        a_ref[s] = jnp.concatenate([pw_re[CHUNK:CHUNK + 1, :], pw_im[CHUNK:CHUNK + 1, :]], axis=0)


def _tables(cc, ada_w, ada_b, lam_re, lam_im, log_dt, b_re, b_im, c_re, c_im, d_skip):
    half_l = 2 * S5_STATE

    def lanes(v):
        return jnp.transpose(v, (1, 0, 2)).reshape(S5_GROUPS, 1, half_l)

    ldt = lanes(jnp.broadcast_to(log_dt[..., None], lam_re.shape))
    bt_re = jnp.transpose(b_re, (1, 3, 0, 2)).reshape(S5_GROUPS, S5_GROUP, half_l)
    bt_im = jnp.transpose(b_im, (1, 3, 0, 2)).reshape(S5_GROUPS, S5_GROUP, half_l)
    ct_re = jnp.transpose(c_re, (1, 2, 0, 3)).reshape(S5_GROUPS, S5_GROUP, half_l)
    ct_im = jnp.transpose(c_im, (1, 2, 0, 3)).reshape(S5_GROUPS, S5_GROUP, half_l)
    dw = jnp.tile(d_skip, (1, CHUNK)).reshape(S5_GROUPS, 1, CHUNK_COLS)

    gs = GROUPS_PER_STEP
    steps = S5_GROUPS // gs
    vec = pl.BlockSpec((gs, 1, half_l), lambda i: (i, 0, 0))
    par = pl.BlockSpec((gs, S5_GROUP, half_l), lambda i: (i, 0, 0))
    mat = pl.BlockSpec((gs, CHUNK_COLS, CHUNK_COLS), lambda i: (i, 0, 0))
    mat_shape = jax.ShapeDtypeStruct((S5_GROUPS, CHUNK_COLS, CHUNK_COLS), BF16)
    n_out = ada_w.shape[1]
    tn = n_out // steps

    def both(c_ref, w_ref, b_ref, *rest):
        o_ref, prep_refs = rest[8], rest[:8] + rest[9:]
        _mod_kernel(c_ref, w_ref, b_ref, o_ref)
        _s5_prep_kernel(*prep_refs)

    out = pl.pallas_call(
        both,
        grid=(steps,),
        in_specs=[pl.BlockSpec((MOD_ROWS, D_MODEL), lambda i: (0, 0)),
                  pl.BlockSpec((D_MODEL, tn), lambda i: (0, i)),
                  pl.BlockSpec((1, tn), lambda i: (0, i)),
                  vec, vec, vec, par, par, par, par,
                  pl.BlockSpec((gs, 1, CHUNK_COLS), lambda i: (i, 0, 0))],
        out_specs=[pl.BlockSpec((MOD_ROWS, tn), lambda i: (0, i)),
                   mat, mat, mat, pl.BlockSpec((gs, 2, half_l), lambda i: (i, 0, 0))],
        out_shape=[jax.ShapeDtypeStruct((MOD_ROWS, n_out), F32),
                   mat_shape, mat_shape, mat_shape,
                   jax.ShapeDtypeStruct((S5_GROUPS, 2, half_l), F32)],
        compiler_params=pltpu.CompilerParams(
            dimension_semantics=("parallel",), vmem_limit_bytes=VMEM_LIMIT),
        name="mod_and_s5_operators",
    )(cc, ada_w, ada_b.reshape(1, n_out), lanes(lam_re), lanes(lam_im), ldt, bt_re, bt_im, ct_re, ct_im, dw)
    return out[0], out[1:]


CONV_HALF = CONV_K // 2
CONV_LAG = CONV_HALF + 1
RING_SLOTS = 32
CONV_BLOCKS = STEP_ROWS // CONV_ROWS
BF16_ROWS = 16
TAP_GROUP = 8
FF_TILE = 512


def _aligned(x, m):
    return x if isinstance(x, int) else pl.multiple_of(x, m)


def _conv_block(ring, slots, blk, wb_ref, cb_ref, lg_ref, lb_ref, seed=None):
    b = blk // (TOK_ROWS // CONV_ROWS)
    t0 = _aligned((blk % (TOK_ROWS // CONV_ROWS)) * CONV_ROWS, CONV_ROWS)
    packed = (CONV_ROWS // BF16_ROWS, BF16_ROWS, CONV_WIDTH)
    acc = jnp.zeros(packed, F32)
    if seed is not None:
        rows = jnp.concatenate([seed] * (BF16_ROWS // SUBLANES), axis=0)
        acc = acc + jnp.concatenate([rows] * (CONV_WIDTH // LANES), axis=1)[None]
    for k0 in range(0, CONV_K, TAP_GROUP):
        group = None
        for k in range(k0, min(k0 + TAP_GROUP, CONV_K)):
            prod = wb_ref[k] * ring[slots[k], b, pl.ds(t0, CONV_ROWS), :].reshape(packed)
            group = prod if group is None else group + prod
        acc = acc + group.astype(F32)
    acc = acc.reshape(CONV_ROWS, CONV_WIDTH) + cb_ref[...]
    mu = jnp.mean(acc, axis=-1, keepdims=True)
    dev = acc - mu
    var = jnp.mean(dev * dev, axis=-1, keepdims=True)
    y = dev * lax.rsqrt(var + EPS_LN) * lg_ref[...] + lb_ref[...]
    return (y * jax.nn.sigmoid(y)).astype(BF16)


def _out_kernel(n_rows, x_ref, ys_ref, hc_ref, g1_ref, sh2_ref, sc2_ref, g2_ref, n2_ref, fg_ref,
                wb_ref, cb_ref, lg_ref, lb_ref, wglu_in, wout_in, w1_in, w2_in, o_ref,
                u_scr, ring, yc_scr, wglu_ref, wout_ref, w1_ref, w2_ref):
    s = pl.program_id(0)
    zero_row = jnp.zeros((BATCH, TOK_ROWS, CONV_WIDTH), BF16)

    @pl.when(s < CONV_LAG)
    def _():
        for w_in_ref, w_bf in ((wglu_in, wglu_ref), (wout_in, wout_ref), (w1_in, w1_ref), (w2_in, w2_ref)):
            rows = w_in_ref.shape[0]
            w_bf[pl.ds(pl.multiple_of(s * rows, rows), rows), :] = w_in_ref[...].astype(BF16)

    @pl.when(s == 0)
    def _():
        for i in range(RING_SLOTS - CONV_HALF, RING_SLOTS):
            ring[i] = zero_row

    @pl.when(s < n_rows)
    def _():
        ring[s % RING_SLOTS] = hc_ref[...]

    @pl.when(s >= n_rows)
    def _():
        ring[s % RING_SLOTS] = zero_row

    slots = [(s + (RING_SLOTS - 2 * CONV_HALF) + k) % RING_SLOTS for k in range(CONV_K)]
    conv_args = (wb_ref, cb_ref, lg_ref, lb_ref)

    @pl.when(s == CONV_HALF)
    def _():
        def body(blk, carry):
            rows = pl.ds(_aligned(blk * CONV_ROWS, CONV_ROWS), CONV_ROWS)
            yc_scr[rows, :] = _conv_block(ring, slots, blk, *conv_args)
            return carry
        lax.fori_loop(0, CONV_BLOCKS, body, 0, unroll=4)

    @pl.when(s >= CONV_LAG)
    def _():
        yc = yc_scr[...]
        ys = jnp.concatenate([ys_ref[g].astype(F32) for g in range(S5_GROUPS)], axis=1)
        _from_chunk_layout(ys, u_scr)
        y_s5 = jnp.concatenate(
            [jnp.concatenate([u_scr[vb, b * PITCH:b * PITCH + TOK_ROWS, :] for b in range(BATCH)], axis=0)
             for vb in range(LANE_BLOCKS)], axis=1)
        gl = jax.nn.gelu(y_s5)
        s5o = gl * jax.nn.sigmoid(_dot(gl.astype(BF16), wglu_ref[...]))
        mix = _dot(s5o.astype(BF16), wout_ref[:S5_WIDTH, :]) + _dot(yc, wout_ref[S5_WIDTH:, :])
        h1 = x_ref[...] + g1_ref[...] * mix.reshape(BATCH, TOK_ROWS, D_MODEL)
        a2 = _rms(h1, n2_ref[...]) * (1.0 + sc2_ref[...]) + sh2_ref[...]
        a2 = a2.reshape(STEP_ROWS, D_MODEL).astype(BF16)
        ff_tile = FF_TILE
        n_ff = D_FF // ff_tile
        per_dot = CONV_BLOCKS // (2 * n_ff)
        never = s < 0

        def seeded_conv(first_blk, mat):
            for i in range(per_dot):
                blk = first_blk + i
                r = i * (STEP_ROWS // per_dot)
                seed = jnp.where(never, mat[r:r + SUBLANES, 0:LANES], 0.0)
                yc_scr[blk * CONV_ROWS:(blk + 1) * CONV_ROWS, :] = _conv_block(
                    ring, slots, blk, *conv_args, seed=seed)

        acc = jnp.zeros((STEP_ROWS, D_MODEL), F32)
        for j in range(n_ff):
            up = _dot(a2, w1_ref[:, j * ff_tile:(j + 1) * ff_tile])
            seeded_conv(2 * j * per_dot, up)
            hid = jnp.maximum(up, 0.0)
            down = _dot((hid * hid).astype(BF16), w2_ref[j * ff_tile:(j + 1) * ff_tile, :])
            seeded_conv((2 * j + 1) * per_dot, down)
            acc = acc + down
        h2 = h1 + g2_ref[...] * acc.reshape(BATCH, TOK_ROWS, D_MODEL)
        o_ref[...] = _rms(h2, fg_ref[...])


def _out_block(x, ys, hc, mod4, n2, fg, wb, cb, lg, lb, wglu, wout, w1, w2):
    bsz, n, _ = x.shape
    assert bsz == BATCH and n % TOK_ROWS == 0
    n_rows = n // TOK_ROWS
    chunk_rows = STEP_CHUNKS * BATCH
    vec = pl.BlockSpec((1, D_MODEL), lambda s: (0, 0))
    cvec = pl.BlockSpec((1, CONV_WIDTH), lambda s: (0, 0))

    def const(shape):
        return pl.BlockSpec(shape, lambda s: (0,) * len(shape), pipeline_mode=pl.Buffered(1))

    def lagged(s):
        return jnp.maximum(s - CONV_LAG, 0)

    def staged(w):
        return pl.BlockSpec((w.shape[0] // CONV_LAG, w.shape[1]), lambda s: (jnp.minimum(s, CONV_LAG - 1), 0))

    return pl.pallas_call(
        functools.partial(_out_kernel, n_rows),
        grid=(n_rows + CONV_LAG,),
        in_specs=[
            pl.BlockSpec((BATCH, TOK_ROWS, D_MODEL), lambda s: (0, lagged(s), 0)),
            pl.BlockSpec((S5_GROUPS, chunk_rows, CHUNK_COLS), lambda s: (0, lagged(s), 0)),
            pl.BlockSpec((BATCH, TOK_ROWS, CONV_WIDTH), lambda s: (0, jnp.minimum(s, n_rows - 1), 0)),
            _mod_spec(2), _mod_spec(3), _mod_spec(4), _mod_spec(5), vec, vec,
            const((CONV_K, BF16_ROWS, CONV_WIDTH)), cvec, cvec, cvec,
            staged(wglu), staged(wout), staged(w1), staged(w2),
        ],
        out_specs=pl.BlockSpec((BATCH, TOK_ROWS, D_MODEL), lambda s: (0, lagged(s), 0)),
        out_shape=jax.ShapeDtypeStruct((bsz, n, D_MODEL), F32),
        scratch_shapes=[
            pltpu.VMEM((LANE_BLOCKS, BATCH * PITCH, LANES), F32),
            pltpu.VMEM((RING_SLOTS, BATCH, TOK_ROWS, CONV_WIDTH), BF16),
            pltpu.VMEM((STEP_ROWS, CONV_WIDTH), BF16),
            pltpu.VMEM(wglu.shape, BF16), pltpu.VMEM(wout.shape, BF16),
            pltpu.VMEM(w1.shape, BF16), pltpu.VMEM(w2.shape, BF16),
        ],
        compiler_params=pltpu.CompilerParams(
            dimension_semantics=("arbitrary",), vmem_limit_bytes=OUT_VMEM_LIMIT),
        name="out_block",
    )(x, ys, hc, mod4, mod4, mod4, mod4, n2, fg, wb, cb, lg, lb, wglu, wout, w1, w2)


def kernel(x, c, ctx, c_ctx, ada_w, ada_b, norm1_g, w_in, s5_lam_re, s5_lam_im, s5_log_dt, s5_b_re, s5_b_im, s5_c_re, s5_c_im, s5_d, s5_w_glu, conv_w, conv_b, conv_ln_g, conv_ln_b, w_out, norm2_g, mlp_w1, mlp_w2, final_g):
    bsz = x.shape[0]
    assert bsz == BATCH and bsz + 1 <= MOD_ROWS
    layer = 0

    cc = jnp.concatenate([c, c_ctx[None, :], jnp.zeros((MOD_ROWS - bsz - 1, D_MODEL), F32)], axis=0)
    mod, (t_mat, m_mat, r_mat, a_pow) = _tables(
        cc, ada_w[layer], ada_b[layer],
        s5_lam_re[layer], s5_lam_im[layer], s5_log_dt[layer], s5_b_re[layer], s5_b_im[layer],
        s5_c_re[layer], s5_c_im[layer], s5_d[layer])
    mod4 = mod[:bsz].reshape(bsz, 6, 1, D_MODEL)
    mod4_ctx = jnp.broadcast_to(mod[bsz, :2 * D_MODEL].reshape(1, 2, 1, D_MODEL), (bsz, 2, 1, D_MODEL))

    n1 = norm1_g[layer].reshape(1, D_MODEL)
    x_lat, hc = _in_proj(x, mod4, 0, 1, n1, w_in[layer], True)
    x_ctx, = _in_proj(ctx, mod4_ctx, 0, 1, n1, w_in[layer], False)

    y_s5 = _s5_scan(x_lat, x_ctx, t_mat, m_mat, r_mat, a_pow, bsz)

    conv_wb = jnp.broadcast_to(conv_w[layer].astype(BF16)[:, None, :], (CONV_K, BF16_ROWS, CONV_WIDTH))
    return _out_block(
        x, y_s5, hc, mod4,
        norm2_g[layer].reshape(1, D_MODEL), final_g.reshape(1, D_MODEL),
        conv_wb, conv_b[layer].reshape(1, CONV_WIDTH),
        conv_ln_g[layer].reshape(1, CONV_WIDTH), conv_ln_b[layer].reshape(1, CONV_WIDTH),
        s5_w_glu[layer], w_out[layer], mlp_w1[layer], mlp_w2[layer])
```
